```python
import math
import jax, jax.numpy as jnp
from jax import lax
import numpy as np

D_MODEL = 1024
BATCH = 16
SEQ = 2048
DEPTH = 1

CHUNK = 64
EPS = 1e-6
A_HEADS = 8
A_HEAD_DIM = 64
A_WIDTH = A_HEADS * A_HEAD_DIM
A_LATENT = 128
IDX_HEADS = 8
IDX_DIM = 64
TOPK_MAX = 256
Q_BLOCK = 128
REL_BUCKETS = 32
REL_MAX_DIST = 128
B_HEADS = 4
B_HEAD_DIM = 128
B_WIDTH = B_HEADS * B_HEAD_DIM
CONV_WIDTH = 4
D_MIX = A_WIDTH + B_WIDTH
D_FF = 4 * D_MODEL
_IN_SPLITS = (A_WIDTH, A_LATENT, IDX_HEADS * IDX_DIM, IDX_DIM, IDX_HEADS,
              B_WIDTH, B_WIDTH, B_WIDTH, B_HEADS, B_HEADS, B_WIDTH)
IN_COLS = sum(_IN_SPLITS)

kernel_name = 'hybrid_dsa_gdn_sandwich_block'


def rms_norm(x, g):
    xf = x.astype(jnp.float32)
    y = xf * lax.rsqrt(jnp.mean(xf * xf, axis=-1, keepdims=True) + EPS)
    return (y * g.astype(jnp.float32)).astype(x.dtype)


def l2_norm(x):
    xf = x.astype(jnp.float32)
    return xf * lax.rsqrt(jnp.sum(xf * xf, axis=-1, keepdims=True) + EPS)


def t5_bucket(rel):
    nb = REL_BUCKETS // 2
    max_exact = nb // 2
    side = jnp.where(rel > 0, nb, 0)
    n = jnp.abs(rel)
    nf = jnp.maximum(n, 1).astype(jnp.float32)
    large = max_exact + (jnp.log(nf / max_exact) / math.log(REL_MAX_DIST / max_exact)
                         * (nb - max_exact)).astype(jnp.int32)
    large = jnp.minimum(large, nb - 1)
    return side + jnp.where(n < max_exact, n, large)


def dsa_mixer(q_a, c_kv, q_idx, k_idx, w_idx, c_norm, w_uk, w_uv, rel_bias):
    B, T, _ = q_a.shape
    topk = min(TOPK_MAX, T // 4)
    nb = T // Q_BLOCK
    q = q_a.reshape(B, T, A_HEADS, A_HEAD_DIM)
    c = rms_norm(c_kv, c_norm)
    q_abs = jnp.einsum('bthd,hcd->bthc', q, w_uk) * (A_HEAD_DIM ** -0.5)
    qi = q_idx.reshape(B, T, IDX_HEADS, IDX_DIM) * (IDX_DIM ** -0.5)
    wi = w_idx * (IDX_HEADS ** -0.5)
    key_pos = jnp.arange(T)

    def blocks(a):
        return jnp.moveaxis(a.reshape(B, nb, Q_BLOCK, *a.shape[2:]), 1, 0)

    def one_block(args):
        blk, qa_b, qi_b, wi_b = args
        t = blk * Q_BLOCK + jnp.arange(Q_BLOCK)
        limit = (t // CHUNK + 1) * CHUNK
        rel = jax.nn.relu(jnp.einsum('bqhd,bsd->bqhs', qi_b, k_idx))
        score = jnp.einsum('bqhs,bqh->bqs', rel, wi_b).astype(jnp.float32)
        score = jnp.where(key_pos[None, None, :] < limit[None, :, None], score, -jnp.inf)
        _, idx = lax.top_k(score, topk)
        valid = idx < limit[None, :, None]
        c_sel = jax.vmap(lambda cb, ib: cb[ib])(c, idx)
        logits = jnp.einsum('bqhc,bqkc->bqkh', qa_b, c_sel).astype(jnp.float32)
        logits = logits + rel_bias[t5_bucket(idx - t[None, :, None])].astype(jnp.float32)
        logits = jnp.where(valid[..., None], logits, -jnp.inf)
        p = jax.nn.softmax(logits, axis=2).astype(c_sel.dtype)
        return jnp.einsum('bqkh,bqkc->bqhc', p, c_sel)

    o_lat = lax.map(one_block, (jnp.arange(nb), blocks(q_abs), blocks(qi), blocks(wi)))
    o_lat = jnp.moveaxis(o_lat, 0, 1).reshape(B, T, A_HEADS, A_LATENT)
    o = jnp.einsum('bthc,hcd->bthd', o_lat, w_uv)
    return o.reshape(B, T, A_WIDTH)


def causal_depthwise_conv(x, w):
    C = x.shape[-1]
    return lax.conv_general_dilated(x, w[:, None, :].astype(x.dtype), window_strides=(1,),
                                    padding=[(CONV_WIDTH - 1, 0)],
                                    dimension_numbers=('NWC', 'WIO', 'NWC'),
                                    feature_group_count=C)


def chunked_gated_delta_rule(q, k, v, g, beta):
    B, T, H, DK = q.shape
    DV = v.shape[-1]
    N = T // CHUNK

    def chunks(a):
        return jnp.moveaxis(a.reshape(B, N, CHUNK, H, *a.shape[3:]), 3, 1)

    q = chunks(q) * (DK ** -0.5)
    k = chunks(k)
    v = chunks(v)
    g = jnp.cumsum(chunks(g), axis=-1)
    beta = chunks(beta)
    causal = jnp.tril(jnp.ones((CHUNK, CHUNK), dtype=bool))
    strict = jnp.tril(jnp.ones((CHUNK, CHUNK), dtype=bool), -1)
    decay = jnp.exp(jnp.where(causal, g[..., :, None] - g[..., None, :], -jnp.inf))
    k_beta = k * beta[..., None]
    lower = jnp.where(strict, jnp.einsum('bhncd,bhnsd->bhncs', k_beta, k) * decay, 0.0)
    eye = jnp.eye(CHUNK, dtype=q.dtype)
    t_inv = lax.linalg.triangular_solve(eye + lower, jnp.broadcast_to(eye, lower.shape),
                                        left_side=True, lower=True)
    u = t_inv @ (v * beta[..., None])
    w = t_inv @ (k_beta * jnp.exp(g)[..., None])
    intra = jnp.where(causal, jnp.einsum('bhncd,bhnsd->bhncs', q, k) * decay, 0.0)

    def step(state, xs):
        q_c, k_c, u_c, w_c, g_c, a_c = xs
        v_new = u_c - w_c @ state
        o_c = (q_c * jnp.exp(g_c)[..., None]) @ state + a_c @ v_new
        g_last = g_c[..., -1:]
        state = state * jnp.exp(g_last)[..., None] + jnp.einsum(
            'bhcd,bhce->bhde', k_c * jnp.exp(g_last - g_c)[..., None], v_new)
        return state, o_c

    xs = tuple(jnp.moveaxis(a, 2, 0) for a in (q, k, u, w, g, intra))
    state0 = jnp.zeros((B, H, DK, DV), q.dtype)
    _, o = lax.scan(step, state0, xs)
    o = jnp.moveaxis(jnp.moveaxis(o, 0, 2), 1, 3)
    return o.reshape(B, T, H, DV)


def gated_deltanet_mixer(q_b, k_b, v_b, a_b, b_b, z_b, conv_w, a_log, dt_bias, o_norm):
    B, T, _ = q_b.shape
    qkv = jax.nn.silu(causal_depthwise_conv(jnp.concatenate([q_b, k_b, v_b], axis=-1), conv_w))
    q, k, v = jnp.split(qkv, 3, axis=-1)
    q = l2_norm(q.reshape(B, T, B_HEADS, B_HEAD_DIM))
    k = l2_norm(k.reshape(B, T, B_HEADS, B_HEAD_DIM))
    v = v.reshape(B, T, B_HEADS, B_HEAD_DIM).astype(jnp.float32)
    g = -jnp.exp(a_log.astype(jnp.float32)) * jax.nn.softplus(
        a_b.astype(jnp.float32) + dt_bias.astype(jnp.float32))
    beta = jax.nn.sigmoid(b_b.astype(jnp.float32))
    o = chunked_gated_delta_rule(q, k, v, g, beta)
    z = z_b.reshape(B, T, B_HEADS, B_HEAD_DIM).astype(jnp.float32)
    o = rms_norm(o, o_norm) * jax.nn.silu(z)
    return o.reshape(B, T, B_WIDTH).astype(q_b.dtype)


def hybrid_mixer(h, w_in, c_norm, w_uk, w_uv, rel_bias, conv_w, a_log, dt_bias, o_norm, w_out):
    proj = h @ w_in
    split_points = [int(p) for p in np.cumsum(np.array(_IN_SPLITS))[:-1]]
    q_a, c_kv, q_idx, k_idx, w_idx, q_b, k_b, v_b, a_b, b_b, z_b = jnp.split(proj, split_points, axis=-1)
    o_a = dsa_mixer(q_a, c_kv, q_idx, k_idx, w_idx, c_norm, w_uk, w_uv, rel_bias)
    o_b = gated_deltanet_mixer(q_b, k_b, v_b, a_b, b_b, z_b, conv_w, a_log, dt_bias, o_norm)
    return jnp.concatenate([o_a, o_b], axis=-1) @ w_out


def setup_inputs(seed: int = 0) -> dict:
    key = jax.random.key(seed)
    ks = jax.random.split(key, 18)
    f32 = jnp.float32

    def nrm(k, shape, scale):
        return jax.random.normal(k, shape, f32) * scale

    def gain(k, shape):
        return 1.0 + 0.01 * jax.random.normal(k, shape, f32)

    dt = jnp.exp(jax.random.uniform(ks[8], (DEPTH, B_HEADS), f32, math.log(1e-3), math.log(1e-1)))
    return {
        'x': nrm(ks[0], (BATCH, SEQ, D_MODEL), 1.0),
        'w_in': nrm(ks[1], (DEPTH, D_MODEL, IN_COLS), D_MODEL ** -0.5),
        'c_norm': gain(ks[2], (DEPTH, A_LATENT)),
        'w_uk': nrm(ks[3], (DEPTH, A_HEADS, A_LATENT, A_HEAD_DIM), A_LATENT ** -0.5),
        'w_uv': nrm(ks[4], (DEPTH, A_HEADS, A_LATENT, A_HEAD_DIM), A_LATENT ** -0.5),
        'rel_bias': nrm(ks[5], (REL_BUCKETS, A_HEADS), 0.5),
        'conv_w': nrm(ks[6], (DEPTH, CONV_WIDTH, 3 * B_WIDTH), CONV_WIDTH ** -0.5),
        'a_log': jnp.log(jax.random.uniform(ks[7], (DEPTH, B_HEADS), f32, 1.0, 16.0)),
        'dt_bias': dt + jnp.log(-jnp.expm1(-dt)),
        'o_norm': gain(ks[9], (DEPTH, B_HEAD_DIM)),
        'w_out': nrm(ks[10], (DEPTH, D_MIX, D_MODEL), D_MIX ** -0.5),
        'pre_norm_mix': gain(ks[11], (DEPTH, D_MODEL)),
        'post_norm_mix': gain(ks[12], (DEPTH, D_MODEL)),
        'pre_norm_mlp': gain(ks[13], (DEPTH, D_MODEL)),
        'post_norm_mlp': gain(ks[14], (DEPTH, D_MODEL)),
        'w_mlp_in': nrm(ks[15], (DEPTH, D_MODEL, D_FF), D_MODEL ** -0.5),
        'w_mlp_out': nrm(ks[16], (DEPTH, D_FF, D_MODEL), D_FF ** -0.5),
    }


def reference(x, w_in, c_norm, w_uk, w_uv, rel_bias, conv_w, a_log, dt_bias, o_norm, w_out,
              pre_norm_mix, post_norm_mix, pre_norm_mlp, post_norm_mlp, w_mlp_in, w_mlp_out):
    for l in range(DEPTH):
        h = rms_norm(x, pre_norm_mix[l])
        mix = hybrid_mixer(h, w_in[l], c_norm[l], w_uk[l], w_uv[l], rel_bias, conv_w[l],
                           a_log[l], dt_bias[l], o_norm[l], w_out[l])
        x = x + rms_norm(mix, post_norm_mix[l])
        h = rms_norm(x, pre_norm_mlp[l])
        y = jnp.square(jax.nn.relu(h @ w_mlp_in[l])) @ w_mlp_out[l]
        x = x + rms_norm(y, post_norm_mlp[l])
    return x
```

```python
import functools
import math

import numpy as np
import jax
import jax.numpy as jnp
from jax import lax
from jax.experimental import pallas as pl
from jax.experimental.pallas import tpu as pltpu

F32 = jnp.float32
BF16 = jnp.bfloat16

EPS = 1e-6
CHUNK = 64
A_HEADS = 8
A_HEAD_DIM = 64
A_WIDTH = A_HEADS * A_HEAD_DIM
A_LATENT = 128
IDX_HEADS = 8
IDX_DIM = 64
TOPK_MAX = 256
REL_BUCKETS = 32
REL_MAX_DIST = 128
B_HEADS = 4
B_HEAD_DIM = 128
B_WIDTH = B_HEADS * B_HEAD_DIM
CONV_WIDTH = 4
IN_SPLITS = (A_WIDTH, A_LATENT, IDX_HEADS * IDX_DIM, IDX_DIM, IDX_HEADS,
             B_WIDTH, B_WIDTH, B_WIDTH, B_HEADS, B_HEADS, B_WIDTH)

LANES = 128
VMEM_LIMIT = 56 * 1024 * 1024
NEG_BIG = -1e30

NT_DIMS = (((1,), (1,)), ((), ()))


def _dot(a, b, precision=None):
    return jnp.dot(a, b, preferred_element_type=F32, precision=precision)


def _dot_nt(a, b, precision=None):
    return lax.dot_general(a, b, NT_DIMS, preferred_element_type=F32, precision=precision)


def _rms(x, g):
    return x * lax.rsqrt(jnp.mean(x * x, axis=-1, keepdims=True) + EPS) * g


def _sigmoid(x):
    return 1.0 / (1.0 + jnp.exp(-x))


def _softplus(x):
    return jnp.maximum(x, 0.0) + jnp.log1p(jnp.exp(-jnp.abs(x)))


_C_QA = 0
_C_QI = _C_QA + A_WIDTH
_C_QKV = _C_QI + IDX_HEADS * IDX_DIM
_C_Z = _C_QKV + 3 * B_WIDTH
_C_CKV = _C_Z + B_WIDTH
_C_SMALL = _C_CKV + A_LATENT
_C_END = _C_SMALL + LANES


def _in_proj_kernel(x_ref, g_ref, w_ref, wabt_ref, cn_ref,
                    qa_ref, qi_ref, qkv_ref, z_ref, c_ref, kidx_ref, wi_ref, ab_ref, abt_ref):
    h = _rms(x_ref[...], g_ref[...]).astype(BF16)

    def proj(lo, hi):
        return _dot(h, w_ref[:, lo:hi])

    qa_ref[...] = proj(_C_QA, _C_QI).astype(BF16)
    qi_ref[...] = (proj(_C_QI, _C_QKV) * (IDX_DIM ** -0.5)).astype(BF16)
    qkv_ref[...] = proj(_C_QKV, _C_Z).astype(BF16)
    z_ref[...] = proj(_C_Z, _C_CKV).astype(BF16)
    c_ref[...] = _rms(proj(_C_CKV, _C_SMALL), cn_ref[...]).astype(BF16)
    small = proj(_C_SMALL, _C_END)
    kidx_ref[...] = small[:, :IDX_DIM].astype(BF16)
    wi_ref[...] = small[:, IDX_DIM:IDX_DIM + IDX_HEADS] * (IDX_HEADS ** -0.5)
    ab_ref[...] = small[:, IDX_DIM + IDX_HEADS:IDX_DIM + IDX_HEADS + 2 * B_HEADS]
    abt_ref[...] = _dot_nt(wabt_ref[...], h)


def _in_proj(x2, g, w_in, c_norm, tm):
    m, d = x2.shape
    offs = np.cumsum((0,) + IN_SPLITS)
    seg = [w_in[:, offs[i]:offs[i + 1]] for i in range(len(IN_SPLITS))]
    q_a, c_kv, q_idx, k_idx, w_idx, q_b, k_b, v_b, a_b, b_b, z_b = seg
    pad = jnp.zeros((d, LANES - IDX_DIM - IDX_HEADS - 2 * B_HEADS), w_in.dtype)
    w_all = jnp.concatenate([q_a, q_idx, q_b, k_b, v_b, z_b, c_kv, k_idx, w_idx, a_b, b_b, pad],
                            axis=1).astype(BF16)
    w_abt = jnp.concatenate([a_b, b_b], axis=1).T.astype(BF16)
    const = lambda i: (0, 0)
    row = lambda i: (i, 0)
    out_shape = (
        jax.ShapeDtypeStruct((m, A_WIDTH), BF16),
        jax.ShapeDtypeStruct((m, IDX_HEADS * IDX_DIM), BF16),
        jax.ShapeDtypeStruct((m, 3 * B_WIDTH), BF16),
        jax.ShapeDtypeStruct((m, B_WIDTH), BF16),
        jax.ShapeDtypeStruct((m, A_LATENT), BF16),
        jax.ShapeDtypeStruct((m, IDX_DIM), BF16),
        jax.ShapeDtypeStruct((m, IDX_HEADS), F32),
        jax.ShapeDtypeStruct((m, 2 * B_HEADS), F32),
        jax.ShapeDtypeStruct((2 * B_HEADS, m), F32),
    )
    out_specs = (
        pl.BlockSpec((tm, A_WIDTH), row),
        pl.BlockSpec((tm, IDX_HEADS * IDX_DIM), row),
        pl.BlockSpec((tm, 3 * B_WIDTH), row),
        pl.BlockSpec((tm, B_WIDTH), row),
        pl.BlockSpec((tm, A_LATENT), row),
        pl.BlockSpec((tm, IDX_DIM), row),
        pl.BlockSpec((tm, IDX_HEADS), row),
        pl.BlockSpec((tm, 2 * B_HEADS), row),
        pl.BlockSpec((2 * B_HEADS, tm), lambda i: (0, i)),
    )
    return pl.pallas_call(
        _in_proj_kernel,
        grid=(m // tm,),
        in_specs=[
            pl.BlockSpec((tm, d), row),
            pl.BlockSpec((1, d), const),
            pl.BlockSpec((d, _C_END), const),
            pl.BlockSpec((2 * B_HEADS, d), const),
            pl.BlockSpec((1, A_LATENT), const),
        ],
        out_specs=out_specs,
        out_shape=out_shape,
        compiler_params=pltpu.CompilerParams(
            dimension_semantics=("arbitrary",), vmem_limit_bytes=VMEM_LIMIT),
        name="in_proj",
    )(x2, g.reshape(1, d), w_all, w_abt, c_norm.reshape(1, A_LATENT))


def _fold_lanes(x, op):
    acc = x[:, :LANES]
    for b in range(1, x.shape[1] // LANES):
        acc = op(acc, x[:, b * LANES:(b + 1) * LANES])
    return acc


def _rep(x):
    return jnp.broadcast_to(x, (x.shape[0], LANES))


def _tile_lanes(x, n):
    return jnp.concatenate([x] * (n // LANES), axis=1) if n > LANES else x


def _dsa_kernel(qa_ref, qi_ref, wi_ref, kidx_ref, c_ref, wuk_ref, wuv_ref, bias_ref, o_ref,
                sc_ref, qabs_ref, m_ref, l_ref, acc_ref, *, tq, ts, topk, nbisect, seq):
    i = pl.program_id(1)
    nkb = (i + 1) * (tq // ts)
    row_t = i * tq + lax.broadcasted_iota(jnp.int32, (tq, 1), 0)
    limit = (row_t // CHUNK + 1) * CHUNK
    kf = float(topk)
    ninf = -jnp.inf

    def key_ids(j):
        return j * ts + lax.broadcasted_iota(jnp.int32, (1, ts), 1)

    wi = wi_ref[...]

    def score_body(j, carry):
        rmin, rmax = carry
        k = kidx_ref[j]
        s = jnp.zeros((tq, ts), F32)
        for h in range(IDX_HEADS):
            r = _dot_nt(qi_ref[:, h * IDX_DIM:(h + 1) * IDX_DIM], k)
            s = s + jnp.maximum(r, 0.0) * wi[:, h:h + 1]
        adm = key_ids(j) < limit
        sc_ref[j] = jnp.where(adm, s, ninf)
        rmax = jnp.maximum(rmax, _fold_lanes(jnp.where(adm, s, ninf), jnp.maximum))
        rmin = jnp.minimum(rmin, _fold_lanes(jnp.where(adm, s, jnp.inf), jnp.minimum))
        return rmin, rmax

    rmin, rmax = lax.fori_loop(
        0, nkb, score_body,
        (jnp.full((tq, LANES), jnp.inf, F32), jnp.full((tq, LANES), ninf, F32)))
    rmin = _rep(jnp.min(rmin, axis=1, keepdims=True))
    rmax = _rep(jnp.max(rmax, axis=1, keepdims=True))

    def count(pred):
        def body(j, acc):
            hit = jnp.where(pred(sc_ref[j], j), 1.0, 0.0)
            return acc + _fold_lanes(hit, jnp.add)
        acc = lax.fori_loop(0, nkb, body, jnp.zeros((tq, LANES), F32))
        return _rep(jnp.sum(acc, axis=1, keepdims=True))

    def count_ge(v):
        vt = _tile_lanes(v, ts)
        return count(lambda s, j: s >= vt)

    need = _rep(limit > topk)
    top_is_thr = count_ge(rmax) >= kf
    done0 = jnp.logical_or(jnp.logical_not(need), top_is_thr)
    thr0 = jnp.where(need, rmax, ninf)

    def bisect_body(_, carry):
        lo, hi = carry
        mid = lo + 0.5 * (hi - lo)
        ge = count_ge(mid) >= kf
        return jnp.where(ge, mid, lo), jnp.where(ge, hi, mid)

    lo, hi = lax.fori_loop(0, nbisect, bisect_body, (rmin, rmax))

    def finish_cond(state):
        _, _, done = state
        return jnp.min(done) < 0.5

    def finish_body(state):
        hi, thr, done = state
        hit = _tile_lanes(hi, ts)

        def max_body(j, acc):
            s = sc_ref[j]
            return jnp.maximum(acc, _fold_lanes(jnp.where(s < hit, s, ninf), jnp.maximum))

        m = lax.fori_loop(0, nkb, max_body, jnp.full((tq, LANES), ninf, F32))
        m = _rep(jnp.max(m, axis=1, keepdims=True))
        ok = count_ge(m) >= kf
        open_ = done < 0.5
        thr = jnp.where(open_, m, thr)
        hi = jnp.where(open_, m, hi)
        done = jnp.where(jnp.logical_and(open_, ok), 1.0, done)
        return hi, thr, done

    _, thr, _ = lax.while_loop(finish_cond, finish_body,
                               (hi, thr0, jnp.where(done0, 1.0, 0.0)))

    thr_t = _tile_lanes(thr, ts)
    n_gt = count(lambda s, j: s > thr_t)
    n_eq = count(lambda s, j: s == thr_t)
    n_tie = kf - n_gt
    split = jnp.logical_and(need, n_eq > n_tie)

    def tie_search(_):
        def body(_, carry):
            jlo, jhi = carry
            jm = jnp.floor((jlo + jhi) * 0.5)
            jm_t = _tile_lanes(jm, ts)
            c = count(lambda s, j: jnp.logical_and(s == thr_t, key_ids(j).astype(F32) <= jm_t))
            ge = c >= n_tie
            return jnp.where(ge, jlo, jm), jnp.where(ge, jm, jhi)
        steps = int(math.ceil(math.log2(seq))) + 1
        _, jhi = lax.fori_loop(0, steps, body,
                               (jnp.full((tq, LANES), -1.0, F32), jnp.full((tq, LANES), seq - 1.0, F32)))
        return jnp.where(split, jhi, float(seq))

    last_tie = lax.cond(jnp.max(jnp.where(split, 1.0, 0.0)) > 0.5, tie_search,
                        lambda _: jnp.full((tq, LANES), float(seq), F32), 0)
    last_tie = jnp.where(need, last_tie, -1.0)
    last_tie_t = _tile_lanes(last_tie, ts)

    def mask_bias(j):
        s = sc_ref[j]
        sel = jnp.logical_or(s > thr_t,
                             jnp.logical_and(s == thr_t, key_ids(j).astype(F32) <= last_tie_t))
        return jnp.where(sel, 0.0, NEG_BIG)

    qa = qa_ref[...]
    for h in range(A_HEADS):
        qh = _dot(qa[:, h * A_HEAD_DIM:(h + 1) * A_HEAD_DIM], wuk_ref[h]) * (A_HEAD_DIM ** -0.5)
        qabs_ref[h] = qh.astype(BF16)
    m_ref[...] = jnp.full(m_ref.shape, NEG_BIG, F32)
    l_ref[...] = jnp.zeros(l_ref.shape, F32)
    acc_ref[...] = jnp.zeros(acc_ref.shape, F32)

    def attend(j, bias_of_head):
        cj = c_ref[j]
        mb = mask_bias(j)
        for h in range(A_HEADS):
            s = _dot_nt(qabs_ref[h], cj) + mb
            b = bias_of_head(h)
            if b is not None:
                s = s + b
            m_old = m_ref[h]
            m_new = jnp.maximum(m_old, _rep(jnp.max(s, axis=1, keepdims=True)))
            alpha = jnp.exp(m_old - m_new)
            p = jnp.exp(s - _tile_lanes(m_new, ts))
            l_ref[h] = alpha * l_ref[h] + _rep(jnp.sum(p, axis=1, keepdims=True))
            acc_ref[h] = alpha * acc_ref[h] + _dot(p.astype(BF16), cj)
            m_ref[h] = m_new

    nq = tq // ts
    assert nq == 1
    lax.fori_loop(0, jnp.maximum(i - 1, 0), lambda j, c: (attend(j, lambda h: None), c)[1], 0)

    @pl.when(i > 0)
    def _():
        attend(i - 1, lambda h: bias_ref[h, :, :ts])

    attend(i, lambda h: bias_ref[h, :, ts:])

    outs = []
    for h in range(A_HEADS):
        o_lat = acc_ref[h] / l_ref[h]
        outs.append(_dot(o_lat.astype(BF16), wuv_ref[h]))
    o_ref[...] = jnp.concatenate(outs, axis=1).astype(o_ref.dtype)


def _t5_bucket(rel):
    nb = REL_BUCKETS // 2
    max_exact = nb // 2
    side = jnp.where(rel > 0, nb, 0)
    n = jnp.abs(rel)
    nf = jnp.maximum(n, 1).astype(jnp.float32)
    large = max_exact + (jnp.log(nf / max_exact) / math.log(REL_MAX_DIST / max_exact)
                         * (nb - max_exact)).astype(jnp.int32)
    large = jnp.minimum(large, nb - 1)
    return side + jnp.where(n < max_exact, n, large)


def _near_bias(rel_bias, tq, ts):
    rel = (jnp.arange(2 * ts)[None, :] - ts) - jnp.arange(tq)[:, None]
    far = rel_bias[_t5_bucket(jnp.full((1, 1), -(tq + ts), jnp.int32))]
    tab = rel_bias[_t5_bucket(rel)] - far
    return jnp.moveaxis(tab, 2, 0).astype(F32)


def _dsa(qa, qi, wi, kidx, c, w_uk, w_uv, rel_bias, batch, seq, tq, nbisect):
    ts = tq
    m = batch * seq
    nq = seq // tq
    nks = seq // ts
    topk = min(TOPK_MAX, seq // 4)
    kidx3 = kidx.reshape(m // ts, ts, IDX_DIM)
    c3 = c.reshape(m // ts, ts, A_LATENT)
    wuk_t = jnp.swapaxes(w_uk, 1, 2).astype(BF16)
    wuv = w_uv.astype(BF16)
    bias = _near_bias(rel_bias, tq, ts)
    qrow = lambda b, i: (b * nq + i, 0)
    kv = lambda b, i: (b, 0, 0)
    const3 = lambda b, i: (0, 0, 0)
    kern = functools.partial(_dsa_kernel, tq=tq, ts=ts, topk=topk, nbisect=nbisect, seq=seq)
    return pl.pallas_call(
        kern,
        grid=(batch, nq),
        in_specs=[
            pl.BlockSpec((tq, A_WIDTH), qrow),
            pl.BlockSpec((tq, IDX_HEADS * IDX_DIM), qrow),
            pl.BlockSpec((tq, IDX_HEADS), qrow),
            pl.BlockSpec((nks, ts, IDX_DIM), kv),
            pl.BlockSpec((nks, ts, A_LATENT), kv),
            pl.BlockSpec((A_HEADS, A_HEAD_DIM, A_LATENT), const3),
            pl.BlockSpec((A_HEADS, A_LATENT, A_HEAD_DIM), const3),
            pl.BlockSpec((A_HEADS, tq, 2 * ts), const3),
        ],
        out_specs=pl.BlockSpec((tq, A_WIDTH), qrow),
        out_shape=jax.ShapeDtypeStruct((m, A_WIDTH), BF16),
        scratch_shapes=[
            pltpu.VMEM((nks, tq, ts), F32),
            pltpu.VMEM((A_HEADS, tq, A_LATENT), BF16),
            pltpu.VMEM((A_HEADS, tq, LANES), F32),
            pltpu.VMEM((A_HEADS, tq, LANES), F32),
            pltpu.VMEM((A_HEADS, tq, A_LATENT), F32),
        ],
        compiler_params=pltpu.CompilerParams(
            dimension_semantics=("arbitrary", "arbitrary"), vmem_limit_bytes=VMEM_LIMIT),
        name="dsa",
    )(qa, qi, wi, kidx3, c3, wuk_t, wuv, bias)


PAIR = 2 * CHUNK
HI = lax.Precision.HIGHEST


def _gdn_kernel(qkv_ref, z_ref, ab_ref, abt_ref, convw_ref, pcol_ref, prow_ref, onorm_ref, o_ref,
                u_s, w_s, qg_s, kdt_s, a_s, gl_s, st_s, *, seq):
    npair = seq // PAIR
    nchunk = seq // CHUNK
    ri = lax.broadcasted_iota(jnp.int32, (PAIR, PAIR), 0)
    ci = lax.broadcasted_iota(jnp.int32, (PAIR, PAIR), 1)
    same = (ri // CHUNK) == (ci // CHUNK)
    lower_bd = jnp.where(jnp.logical_and(same, ci <= ri), 1.0, 0.0)
    upper_bd = jnp.where(jnp.logical_and(same, ri <= ci), 1.0, 0.0)
    r64 = lax.broadcasted_iota(jnp.int32, (CHUNK, CHUNK), 0)
    c64 = lax.broadcasted_iota(jnp.int32, (CHUNK, CHUNK), 1)
    causal = c64 <= r64
    strict = c64 < r64
    eye = jnp.where(c64 == r64, 1.0, 0.0)
    neg_exp_a_col = pcol_ref[:, 0:1]
    dt_col = pcol_ref[:, 1:2]
    neg_exp_a_row = prow_ref[0:1, :]
    dt_row = prow_ref[1:2, :]
    convw = convw_ref[...]
    row_in_pair = lax.broadcasted_iota(jnp.int32, (PAIR, 1), 0)

    def conv_silu(p, col):
        r0 = pl.multiple_of(p * PAIR, PAIR)
        cur = qkv_ref[pl.ds(r0, PAIR), col:col + B_HEAD_DIM].astype(F32)
        prev0 = pl.multiple_of(jnp.maximum(r0 - 16, 0), 16)
        prev = qkv_ref[pl.ds(prev0, 16), col:col + B_HEAD_DIM].astype(F32)
        prev = jnp.where(p > 0, prev, 0.0)
        ext = jnp.concatenate([prev, cur], axis=0)
        y = cur * convw[CONV_WIDTH - 1:CONV_WIDTH, col:col + B_HEAD_DIM]
        for back in range(1, CONV_WIDTH):
            sh = ext[16 - back:16 - back + PAIR]
            y = y + sh * convw[CONV_WIDTH - 1 - back:CONV_WIDTH - back, col:col + B_HEAD_DIM]
        return y * _sigmoid(y)

    def l2n(x):
        return x * lax.rsqrt(jnp.sum(x * x, axis=-1, keepdims=True) + EPS)

    def tri_inverse(low):
        x = eye - low
        pw = _dot(low, low, HI)
        steps = int(math.log2(CHUNK)) - 1
        for k in range(steps):
            x = x + _dot(x, pw, HI)
            if k + 1 < steps:
                pw = _dot(pw, pw, HI)
        return x

    def phase1(p, carry):
        r0 = pl.multiple_of(p * PAIR, PAIR)
        ab = ab_ref[pl.ds(r0, PAIR), :]
        g_col = neg_exp_a_row * _softplus(ab + dt_row)
        gc_col = _dot(lower_bd, g_col, HI)
        beta_col = _sigmoid(ab)
        abt = abt_ref[p]
        g_row = neg_exp_a_col * _softplus(abt + dt_col)
        gc_row = _dot(g_row, upper_bd, HI)
        for h in range(B_HEADS):
            q = l2n(conv_silu(p, h * B_HEAD_DIM)) * (B_HEAD_DIM ** -0.5)
            k = l2n(conv_silu(p, B_WIDTH + h * B_HEAD_DIM))
            v = conv_silu(p, 2 * B_WIDTH + h * B_HEAD_DIM)
            gcc = gc_col[:, h:h + 1]
            beta = beta_col[:, B_HEADS + h:B_HEADS + h + 1]
            kb = k * beta
            vb = v * beta
            for cc in range(2):
                rs = slice(cc * CHUNK, (cc + 1) * CHUNK)
                n = 2 * p + cc
                gcol = gcc[rs]
                grow = gc_row[h:h + 1, rs]
                decay = jnp.exp(jnp.where(causal, gcol - grow, -jnp.inf))
                qk_kk = _dot_nt(jnp.concatenate([q[rs], kb[rs]], axis=0).astype(BF16),
                                k[rs].astype(BF16))
                intra = qk_kk[:CHUNK] * decay
                low = jnp.where(strict, qk_kk[CHUNK:] * decay, 0.0)
                tinv = tri_inverse(low)
                rhs = jnp.concatenate([vb[rs], kb[rs] * jnp.exp(gcol)], axis=1)
                uw = _dot(tinv.astype(BF16), rhs.astype(BF16))
                g_last = grow[:, CHUNK - 1:CHUNK]
                rows = pl.ds(pl.multiple_of(n * CHUNK, CHUNK), CHUNK)
                u_s[h, rows, :] = uw[:, :B_HEAD_DIM]
                w_s[h, rows, :] = uw[:, B_HEAD_DIM:].astype(BF16)
                qg_s[h, rows, :] = (q[rs] * jnp.exp(gcol)).astype(BF16)
                kdt_s[h, n] = (k[rs] * jnp.exp(g_last - gcol)).T.astype(BF16)
                a_s[h, n] = intra.astype(BF16)
                gl_s[h, n] = jnp.broadcast_to(jnp.exp(g_last), (8, LANES))
        return carry

    lax.fori_loop(0, npair, phase1, 0)

    st_s[...] = jnp.zeros(st_s.shape, F32)
    onorm = onorm_ref[...]

    def phase2(n, carry):
        rows = pl.ds(pl.multiple_of(n * CHUNK, CHUNK), CHUNK)
        for h in range(B_HEADS):
            state = st_s[h]
            sb = state.astype(BF16)
            v_new = u_s[h, rows, :] - _dot(w_s[h, rows, :], sb)
            vb = v_new.astype(BF16)
            o = _dot(qg_s[h, rows, :], sb) + _dot(a_s[h, n], vb)
            st_s[h] = state * gl_s[h, n][0:1, :] + _dot(kdt_s[h, n], vb)
            z = z_ref[rows, h * B_HEAD_DIM:(h + 1) * B_HEAD_DIM].astype(F32)
            o_ref[rows, h * B_HEAD_DIM:(h + 1) * B_HEAD_DIM] = (
                _rms(o, onorm) * (z * _sigmoid(z))).astype(o_ref.dtype)
        return carry

    lax.fori_loop(0, nchunk, phase2, 0)


def _gdn(qkv, z, ab, abt, conv_w, a_log, dt_bias, o_norm, batch, seq):
    m = batch * seq
    npair = seq // PAIR
    nchunk = seq // CHUNK
    abt3 = jnp.swapaxes(abt.reshape(2 * B_HEADS, m // PAIR, PAIR), 0, 1)
    zeros = jnp.zeros((B_HEADS,), F32)
    nea = jnp.concatenate([-jnp.exp(a_log.astype(F32)), zeros])
    dtb = jnp.concatenate([dt_bias.astype(F32), zeros])
    pcol = jnp.stack([nea, dtb], axis=1)
    prow = jnp.stack([nea, dtb], axis=0)
    const = lambda b: (0, 0)
    rows = lambda b: (b, 0)
    return pl.pallas_call(
        functools.partial(_gdn_kernel, seq=seq),
        grid=(batch,),
        in_specs=[
            pl.BlockSpec((seq, 3 * B_WIDTH), rows),
            pl.BlockSpec((seq, B_WIDTH), rows),
            pl.BlockSpec((seq, 2 * B_HEADS), rows),
            pl.BlockSpec((npair, 2 * B_HEADS, PAIR), lambda b: (b, 0, 0)),
            pl.BlockSpec((CONV_WIDTH, 3 * B_WIDTH), const),
            pl.BlockSpec((2 * B_HEADS, 2), const),
            pl.BlockSpec((2, 2 * B_HEADS), const),
            pl.BlockSpec((1, B_HEAD_DIM), const),
        ],
        out_specs=pl.BlockSpec((seq, B_WIDTH), rows),
        out_shape=jax.ShapeDtypeStruct((m, B_WIDTH), BF16),
        scratch_shapes=[
            pltpu.VMEM((B_HEADS, seq, B_HEAD_DIM), F32),
            pltpu.VMEM((B_HEADS, seq, B_HEAD_DIM), BF16),
            pltpu.VMEM((B_HEADS, seq, B_HEAD_DIM), BF16),
            pltpu.VMEM((B_HEADS, nchunk, B_HEAD_DIM, CHUNK), BF16),
            pltpu.VMEM((B_HEADS, nchunk, CHUNK, CHUNK), BF16),
            pltpu.VMEM((B_HEADS, nchunk, 8, LANES), F32),
            pltpu.VMEM((B_HEADS, B_HEAD_DIM, B_HEAD_DIM), F32),
        ],
        compiler_params=pltpu.CompilerParams(
            dimension_semantics=("arbitrary",), vmem_limit_bytes=VMEM_LIMIT),
        name="gdn",
    )(qkv, z, ab, abt3, conv_w.astype(F32), pcol, prow, o_norm.reshape(1, B_HEAD_DIM).astype(F32))


def _out_mlp_kernel(x_ref, oa_ref, ob_ref, wo_ref, g_post_mix_ref, g_pre_mlp_ref, w1_ref, w2_ref,
                    g_post_mlp_ref, o_ref, *, ff_chunk):
    mix = _dot(oa_ref[...], wo_ref[:A_WIDTH, :]) + _dot(ob_ref[...], wo_ref[A_WIDTH:, :])
    x1 = x_ref[...] + _rms(mix, g_post_mix_ref[...])
    h = _rms(x1, g_pre_mlp_ref[...]).astype(BF16)
    d_ff = w1_ref.shape[1]
    y = jnp.zeros(x1.shape, F32)
    for lo in range(0, d_ff, ff_chunk):
        a = jnp.maximum(_dot(h, w1_ref[:, lo:lo + ff_chunk]), 0.0)
        y = y + _dot((a * a).astype(BF16), w2_ref[lo:lo + ff_chunk, :])
    o_ref[...] = x1 + _rms(y, g_post_mlp_ref[...])


def _out_mlp(x2, oa, ob, w_out, g_post_mix, g_pre_mlp, w1, w2, g_post_mlp, tm, ff_chunk):
    m, d = x2.shape
    d_ff = w1.shape[1]
    const = lambda i: (0, 0)
    row = lambda i: (i, 0)
    once = dict(pipeline_mode=pl.Buffered(1))
    vec = lambda g: g.reshape(1, d).astype(F32)
    return pl.pallas_call(
        functools.partial(_out_mlp_kernel, ff_chunk=ff_chunk),
        grid=(m // tm,),
        in_specs=[
            pl.BlockSpec((tm, d), row),
            pl.BlockSpec((tm, A_WIDTH), row),
            pl.BlockSpec((tm, B_WIDTH), row),
            pl.BlockSpec((A_WIDTH + B_WIDTH, d), const, **once),
            pl.BlockSpec((1, d), const),
            pl.BlockSpec((1, d), const),
            pl.BlockSpec((d, d_ff), const, **once),
            pl.BlockSpec((d_ff, d), const, **once),
            pl.BlockSpec((1, d), const),
        ],
        out_specs=pl.BlockSpec((tm, d), row),
        out_shape=jax.ShapeDtypeStruct((m, d), F32),
        compiler_params=pltpu.CompilerParams(
            dimension_semantics=("arbitrary",), vmem_limit_bytes=VMEM_LIMIT),
        name="out_mlp",
    )(x2, oa, ob, w_out.astype(BF16), vec(g_post_mix), vec(g_pre_mlp),
      w1.astype(BF16), w2.astype(BF16), vec(g_post_mlp))


def kernel(x, w_in, c_norm, w_uk, w_uv, rel_bias, conv_w, a_log, dt_bias, o_norm, w_out,
           pre_norm_mix, post_norm_mix, pre_norm_mlp, post_norm_mlp, w_mlp_in, w_mlp_out):
    batch, seq, d = x.shape
    m = batch * seq
    tm = min(512, m)
    tq = min(256, seq)
    xs = x.reshape(m, d)
    for l in range(w_in.shape[0]):
        qa, qi, qkv, z, c, kidx, wi, ab, abt = _in_proj(xs, pre_norm_mix[l], w_in[l], c_norm[l], tm)
        oa = _dsa(qa, qi, wi, kidx, c, w_uk[l], w_uv[l], rel_bias, batch, seq, tq, nbisect=16)
        ob = _gdn(qkv, z, ab, abt, conv_w[l], a_log[l], dt_bias[l], o_norm[l], batch, seq)
        xs = _out_mlp(xs, oa, ob, w_out[l], post_norm_mix[l], pre_norm_mlp[l],
                      w_mlp_in[l], w_mlp_out[l], post_norm_mlp[l], tm, ff_chunk=1024)
    return xs.reshape(batch, seq, d)
```

```python
import functools
import math

import numpy as np
import jax
import jax.numpy as jnp
from jax import lax
from jax.experimental import pallas as pl
from jax.experimental.pallas import tpu as pltpu

F32 = jnp.float32
BF16 = jnp.bfloat16

EPS = 1e-6
CHUNK = 64
A_HEADS = 8
A_HEAD_DIM = 64
A_WIDTH = A_HEADS * A_HEAD_DIM
A_LATENT = 128
IDX_HEADS = 8
IDX_DIM = 64
TOPK_MAX = 256
REL_BUCKETS = 32
REL_MAX_DIST = 128
B_HEADS = 4
B_HEAD_DIM = 128
B_WIDTH = B_HEADS * B_HEAD_DIM
CONV_WIDTH = 4
IN_SPLITS = (A_WIDTH, A_LATENT, IDX_HEADS * IDX_DIM, IDX_DIM, IDX_HEADS,
             B_WIDTH, B_WIDTH, B_WIDTH, B_HEADS, B_HEADS, B_WIDTH)

LANES = 128
VMEM_LIMIT = 56 * 1024 * 1024
NEG_BIG = -1e30

NT_DIMS = (((1,), (1,)), ((), ()))


def _dot(a, b, precision=None):
    return jnp.dot(a, b, preferred_element_type=F32, precision=precision)


def _dot_nt(a, b, precision=None):
    return lax.dot_general(a, b, NT_DIMS, preferred_element_type=F32, precision=precision)


def _rms(x, g):
    return x * lax.rsqrt(jnp.mean(x * x, axis=-1, keepdims=True) + EPS) * g


def _sigmoid(x):
    return 1.0 / (1.0 + jnp.exp(-x))


def _softplus(x):
    return jnp.maximum(x, 0.0) + jnp.log1p(jnp.exp(-jnp.abs(x)))


_C_QA = 0
_C_QI = _C_QA + A_WIDTH
_C_QKV = _C_QI + IDX_HEADS * IDX_DIM
_C_Z = _C_QKV + 3 * B_WIDTH
_C_CKV = _C_Z + B_WIDTH
_C_SMALL = _C_CKV + A_LATENT
_C_END = _C_SMALL + LANES


def _in_proj_kernel(x_ref, g_ref, w_ref, wabt_ref, cn_ref,
                    qa_ref, qi_ref, qkv_ref, z_ref, c_ref, kidx_ref, wi_ref, ab_ref, abt_ref):
    h = _rms(x_ref[...], g_ref[...]).astype(BF16)

    def proj(lo, hi):
        return _dot(h, w_ref[:, lo:hi])

    qa_ref[...] = proj(_C_QA, _C_QI).astype(BF16)
    qi_ref[...] = (proj(_C_QI, _C_QKV) * (IDX_DIM ** -0.5)).astype(BF16)
    qkv_ref[...] = proj(_C_QKV, _C_Z).astype(BF16)
    z_ref[...] = proj(_C_Z, _C_CKV).astype(BF16)
    c_ref[...] = _rms(proj(_C_CKV, _C_SMALL), cn_ref[...]).astype(BF16)
    small = proj(_C_SMALL, _C_END)
    kidx_ref[...] = small[:, :IDX_DIM].astype(BF16)
    wi_ref[...] = small[:, IDX_DIM:IDX_DIM + IDX_HEADS] * (IDX_HEADS ** -0.5)
    ab_ref[...] = small[:, IDX_DIM + IDX_HEADS:IDX_DIM + IDX_HEADS + 2 * B_HEADS]
    abt_ref[...] = _dot_nt(wabt_ref[...], h)


def _in_proj(x2, g, w_in, c_norm, tm):
    m, d = x2.shape
    offs = np.cumsum((0,) + IN_SPLITS)
    seg = [w_in[:, offs[i]:offs[i + 1]] for i in range(len(IN_SPLITS))]
    q_a, c_kv, q_idx, k_idx, w_idx, q_b, k_b, v_b, a_b, b_b, z_b = seg
    pad = jnp.zeros((d, LANES - IDX_DIM - IDX_HEADS - 2 * B_HEADS), w_in.dtype)
    w_all = jnp.concatenate([q_a, q_idx, q_b, k_b, v_b, z_b, c_kv, k_idx, w_idx, a_b, b_b, pad],
                            axis=1).astype(BF16)
    w_abt = jnp.concatenate([a_b, b_b], axis=1).T.astype(BF16)
    const = lambda i: (0, 0)
    row = lambda i: (i, 0)
    out_shape = (
        jax.ShapeDtypeStruct((m, A_WIDTH), BF16),
        jax.ShapeDtypeStruct((m, IDX_HEADS * IDX_DIM), BF16),
        jax.ShapeDtypeStruct((m, 3 * B_WIDTH), BF16),
        jax.ShapeDtypeStruct((m, B_WIDTH), BF16),
        jax.ShapeDtypeStruct((m, A_LATENT), BF16),
        jax.ShapeDtypeStruct((m, IDX_DIM), BF16),
        jax.ShapeDtypeStruct((m, IDX_HEADS), F32),
        jax.ShapeDtypeStruct((m, 2 * B_HEADS), F32),
        jax.ShapeDtypeStruct((2 * B_HEADS, m), F32),
    )
    out_specs = (
        pl.BlockSpec((tm, A_WIDTH), row),
        pl.BlockSpec((tm, IDX_HEADS * IDX_DIM), row),
        pl.BlockSpec((tm, 3 * B_WIDTH), row),
        pl.BlockSpec((tm, B_WIDTH), row),
        pl.BlockSpec((tm, A_LATENT), row),
        pl.BlockSpec((tm, IDX_DIM), row),
        pl.BlockSpec((tm, IDX_HEADS), row),
        pl.BlockSpec((tm, 2 * B_HEADS), row),
        pl.BlockSpec((2 * B_HEADS, tm), lambda i: (0, i)),
    )
    return pl.pallas_call(
        _in_proj_kernel,
        grid=(m // tm,),
        in_specs=[
            pl.BlockSpec((tm, d), row),
            pl.BlockSpec((1, d), const),
            pl.BlockSpec((d, _C_END), const),
            pl.BlockSpec((2 * B_HEADS, d), const),
            pl.BlockSpec((1, A_LATENT), const),
        ],
        out_specs=out_specs,
        out_shape=out_shape,
        compiler_params=pltpu.CompilerParams(
            dimension_semantics=("arbitrary",), vmem_limit_bytes=VMEM_LIMIT),
        name="in_proj",
    )(x2, g.reshape(1, d), w_all, w_abt, c_norm.reshape(1, A_LATENT))


def _fold_lanes(x, op):
    acc = x[:, :LANES]
    for b in range(1, x.shape[1] // LANES):
        acc = op(acc, x[:, b * LANES:(b + 1) * LANES])
    return acc


def _rep(x):
    return jnp.broadcast_to(x, (x.shape[0], LANES))


def _tile_lanes(x, n):
    return jnp.concatenate([x] * (n // LANES), axis=1) if n > LANES else x


def _dsa_kernel(qa_ref, qi_ref, wi_ref, kidx_ref, c_ref, wuk_ref, wuv_ref, bias_ref, o_ref,
                sc_ref, qabs_ref, m_ref, l_ref, acc_ref, *, tq, ts, topk, nbisect, seq):
    i = pl.program_id(1)
    nkb = (i + 1) * (tq // ts)
    row_t = i * tq + lax.broadcasted_iota(jnp.int32, (tq, 1), 0)
    limit = (row_t // CHUNK + 1) * CHUNK
    kf = float(topk)
    ninf = -jnp.inf

    def key_ids(j):
        return j * ts + lax.broadcasted_iota(jnp.int32, (1, ts), 1)

    wi = wi_ref[...]

    def score_body(j, carry):
        rmin, rmax = carry
        k = kidx_ref[j]
        s = jnp.zeros((tq, ts), F32)
        for h in range(IDX_HEADS):
            r = _dot_nt(qi_ref[:, h * IDX_DIM:(h + 1) * IDX_DIM], k)
            s = s + jnp.maximum(r, 0.0) * wi[:, h:h + 1]
        adm = key_ids(j) < limit
        sc_ref[j] = jnp.where(adm, s, ninf)
        rmax = jnp.maximum(rmax, _fold_lanes(jnp.where(adm, s, ninf), jnp.maximum))
        rmin = jnp.minimum(rmin, _fold_lanes(jnp.where(adm, s, jnp.inf), jnp.minimum))
        return rmin, rmax

    rmin, rmax = lax.fori_loop(
        0, nkb, score_body,
        (jnp.full((tq, LANES), jnp.inf, F32), jnp.full((tq, LANES), ninf, F32)))
    rmin = _rep(jnp.min(rmin, axis=1, keepdims=True))
    rmax = _rep(jnp.max(rmax, axis=1, keepdims=True))

    def count(pred):
        def body(j, acc):
            hit = jnp.where(pred(sc_ref[j], j), 1.0, 0.0)
            return acc + _fold_lanes(hit, jnp.add)
        acc = lax.fori_loop(0, nkb, body, jnp.zeros((tq, LANES), F32))
        return _rep(jnp.sum(acc, axis=1, keepdims=True))

    def count_ge(v):
        vt = _tile_lanes(v, ts)
        return count(lambda s, j: s >= vt)

    need = _rep(limit > topk)
    top_is_thr = count_ge(rmax) >= kf
    done0 = jnp.logical_or(jnp.logical_not(need), top_is_thr)
    thr0 = jnp.where(need, rmax, ninf)

    def bisect_body(_, carry):
        lo, hi = carry
        mid = lo + 0.5 * (hi - lo)
        ge = count_ge(mid) >= kf
        return jnp.where(ge, mid, lo), jnp.where(ge, hi, mid)

    lo, hi = lax.fori_loop(0, nbisect, bisect_body, (rmin, rmax))

    def finish_cond(state):
        _, _, done = state
        return jnp.min(done) < 0.5

    def finish_body(state):
        hi, thr, done = state
        hit = _tile_lanes(hi, ts)

        def max_body(j, acc):
            s = sc_ref[j]
            return jnp.maximum(acc, _fold_lanes(jnp.where(s < hit, s, ninf), jnp.maximum))

        m = lax.fori_loop(0, nkb, max_body, jnp.full((tq, LANES), ninf, F32))
        m = _rep(jnp.max(m, axis=1, keepdims=True))
        ok = count_ge(m) >= kf
        open_ = done < 0.5
        thr = jnp.where(open_, m, thr)
        hi = jnp.where(open_, m, hi)
        done = jnp.where(jnp.logical_and(open_, ok), 1.0, done)
        return hi, thr, done

    _, thr, _ = lax.while_loop(finish_cond, finish_body,
                               (hi, thr0, jnp.where(done0, 1.0, 0.0)))

    thr_t = _tile_lanes(thr, ts)
    n_gt = count(lambda s, j: s > thr_t)
    n_eq = count(lambda s, j: s == thr_t)
    n_tie = kf - n_gt
    split = jnp.logical_and(need, n_eq > n_tie)

    def tie_search(_):
        def body(_, carry):
            jlo, jhi = carry
            jm = jnp.floor((jlo + jhi) * 0.5)
            jm_t = _tile_lanes(jm, ts)
            c = count(lambda s, j: jnp.logical_and(s == thr_t, key_ids(j).astype(F32) <= jm_t))
            ge = c >= n_tie
            return jnp.where(ge, jlo, jm), jnp.where(ge, jm, jhi)
        steps = int(math.ceil(math.log2(seq))) + 1
        _, jhi = lax.fori_loop(0, steps, body,
                               (jnp.full((tq, LANES), -1.0, F32), jnp.full((tq, LANES), seq - 1.0, F32)))
        return jnp.where(split, jhi, float(seq))

    last_tie = lax.cond(jnp.max(jnp.where(split, 1.0, 0.0)) > 0.5, tie_search,
                        lambda _: jnp.full((tq, LANES), float(seq), F32), 0)
    last_tie = jnp.where(need, last_tie, -1.0)
    last_tie_t = _tile_lanes(last_tie, ts)

    def mask_bias(j):
        s = sc_ref[j]
        sel = jnp.logical_or(s > thr_t,
                             jnp.logical_and(s == thr_t, key_ids(j).astype(F32) <= last_tie_t))
        return jnp.where(sel, 0.0, NEG_BIG)

    qa = qa_ref[...]
    for h in range(A_HEADS):
        qh = _dot(qa[:, h * A_HEAD_DIM:(h + 1) * A_HEAD_DIM], wuk_ref[h]) * (A_HEAD_DIM ** -0.5)
        qabs_ref[h] = qh.astype(BF16)
    m_ref[...] = jnp.full(m_ref.shape, NEG_BIG, F32)
    l_ref[...] = jnp.zeros(l_ref.shape, F32)
    acc_ref[...] = jnp.zeros(acc_ref.shape, F32)

    def attend(j, bias_of_head):
        cj = c_ref[j]
        mb = mask_bias(j)
        for h in range(A_HEADS):
            s = _dot_nt(qabs_ref[h], cj) + mb
            b = bias_of_head(h)
            if b is not None:
                s = s + b
            m_old = m_ref[h]
            m_new = jnp.maximum(m_old, _rep(jnp.max(s, axis=1, keepdims=True)))
            alpha = jnp.exp(m_old - m_new)
            p = jnp.exp(s - _tile_lanes(m_new, ts))
            l_ref[h] = alpha * l_ref[h] + _rep(jnp.sum(p, axis=1, keepdims=True))
            acc_ref[h] = alpha * acc_ref[h] + _dot(p.astype(BF16), cj)
            m_ref[h] = m_new

    nq = tq // ts
    assert nq == 1
    lax.fori_loop(0, jnp.maximum(i - 1, 0), lambda j, c: (attend(j, lambda h: None), c)[1], 0)

    @pl.when(i > 0)
    def _():
        attend(i - 1, lambda h: bias_ref[h, :, :ts])

    attend(i, lambda h: bias_ref[h, :, ts:])

    outs = []
    for h in range(A_HEADS):
        o_lat = acc_ref[h] / l_ref[h]
        outs.append(_dot(o_lat.astype(BF16), wuv_ref[h]))
    o_ref[...] = jnp.concatenate(outs, axis=1).astype(o_ref.dtype)


def _t5_bucket(rel):
    nb = REL_BUCKETS // 2
    max_exact = nb // 2
    side = jnp.where(rel > 0, nb, 0)
    n = jnp.abs(rel)
    nf = jnp.maximum(n, 1).astype(jnp.float32)
    large = max_exact + (jnp.log(nf / max_exact) / math.log(REL_MAX_DIST / max_exact)
                         * (nb - max_exact)).astype(jnp.int32)
    large = jnp.minimum(large, nb - 1)
    return side + jnp.where(n < max_exact, n, large)


def _near_bias_kernel(far_ref, rb_ref, bucket_ref, o_ref):
    bucket = bucket_ref[...]
    for h in range(A_HEADS):
        acc = jnp.zeros(bucket.shape, F32)
        for b in range(REL_BUCKETS):
            acc = jnp.where(bucket == b, rb_ref[b, h], acc)
        o_ref[h] = acc - rb_ref[far_ref[0], h]


def _near_bias(rel_bias, tq, ts):
    assert ts >= REL_MAX_DIST
    rel = (jnp.arange(2 * ts)[None, :] - ts) - jnp.arange(tq)[:, None]
    far = _t5_bucket(jnp.full((1,), -(ts + 1), jnp.int32))
    smem = pl.BlockSpec(memory_space=pltpu.SMEM)
    return pl.pallas_call(
        _near_bias_kernel,
        in_specs=[smem, smem, pl.BlockSpec(memory_space=pltpu.VMEM)],
        out_specs=pl.BlockSpec(memory_space=pltpu.VMEM),
        out_shape=jax.ShapeDtypeStruct((A_HEADS, tq, 2 * ts), F32),
        name="near_bias",
    )(far, rel_bias.astype(F32), _t5_bucket(rel).astype(jnp.int32))


def _dsa(qa, qi, wi, kidx, c, w_uk, w_uv, rel_bias, batch, seq, tq, nbisect):
    ts = tq
    m = batch * seq
    nq = seq // tq
    nks = seq // ts
    topk = min(TOPK_MAX, seq // 4)
    kidx3 = kidx.reshape(m // ts, ts, IDX_DIM)
    c3 = c.reshape(m // ts, ts, A_LATENT)
    wuk_t = jnp.swapaxes(w_uk, 1, 2).astype(BF16)
    wuv = w_uv.astype(BF16)
    bias = _near_bias(rel_bias, tq, ts)
    qrow = lambda b, i: (b * nq + i, 0)
    kv = lambda b, i: (b, 0, 0)
    const3 = lambda b, i: (0, 0, 0)
    kern = functools.partial(_dsa_kernel, tq=tq, ts=ts, topk=topk, nbisect=nbisect, seq=seq)
    return pl.pallas_call(
        kern,
        grid=(batch, nq),
        in_specs=[
            pl.BlockSpec((tq, A_WIDTH), qrow),
            pl.BlockSpec((tq, IDX_HEADS * IDX_DIM), qrow),
            pl.BlockSpec((tq, IDX_HEADS), qrow),
            pl.BlockSpec((nks, ts, IDX_DIM), kv),
            pl.BlockSpec((nks, ts, A_LATENT), kv),
            pl.BlockSpec((A_HEADS, A_HEAD_DIM, A_LATENT), const3),
            pl.BlockSpec((A_HEADS, A_LATENT, A_HEAD_DIM), const3),
            pl.BlockSpec((A_HEADS, tq, 2 * ts), const3),
        ],
        out_specs=pl.BlockSpec((tq, A_WIDTH), qrow),
        out_shape=jax.ShapeDtypeStruct((m, A_WIDTH), BF16),
        scratch_shapes=[
            pltpu.VMEM((nks, tq, ts), F32),
            pltpu.VMEM((A_HEADS, tq, A_LATENT), BF16),
            pltpu.VMEM((A_HEADS, tq, LANES), F32),
            pltpu.VMEM((A_HEADS, tq, LANES), F32),
            pltpu.VMEM((A_HEADS, tq, A_LATENT), F32),
        ],
        compiler_params=pltpu.CompilerParams(
            dimension_semantics=("arbitrary", "arbitrary"), vmem_limit_bytes=VMEM_LIMIT),
        name="dsa",
    )(qa, qi, wi, kidx3, c3, wuk_t, wuv, bias)


PAIR = 2 * CHUNK
HI = lax.Precision.HIGHEST


def _gdn_kernel(qkv_ref, z_ref, ab_ref, abt_ref, convw_ref, pcol_ref, prow_ref, onorm_ref, o_ref,
                u_s, w_s, qg_s, kdt_s, a_s, gl_s, st_s, *, seq):
    npair = seq // PAIR
    nchunk = seq // CHUNK
    ri = lax.broadcasted_iota(jnp.int32, (PAIR, PAIR), 0)
    ci = lax.broadcasted_iota(jnp.int32, (PAIR, PAIR), 1)
    same = (ri // CHUNK) == (ci // CHUNK)
    lower_bd = jnp.where(jnp.logical_and(same, ci <= ri), 1.0, 0.0)
    upper_bd = jnp.where(jnp.logical_and(same, ri <= ci), 1.0, 0.0)
    causal_bd = jnp.logical_and(same, ci <= ri)
    diag = ci == ri
    eye = jnp.where(diag, 1.0, 0.0)
    neg_exp_a_col = pcol_ref[:, 0:1]
    dt_col = pcol_ref[:, 1:2]
    neg_exp_a_row = prow_ref[0:1, :]
    dt_row = prow_ref[1:2, :]
    convw = convw_ref[...]
    first_chunk = lax.broadcasted_iota(jnp.int32, (PAIR, 1), 0) < CHUNK

    def conv_silu(p, col):
        r0 = pl.multiple_of(p * PAIR, PAIR)
        cur = qkv_ref[pl.ds(r0, PAIR), col:col + B_HEAD_DIM].astype(F32)
        prev0 = pl.multiple_of(jnp.maximum(r0 - 16, 0), 16)
        prev = qkv_ref[pl.ds(prev0, 16), col:col + B_HEAD_DIM].astype(F32)
        prev = jnp.where(p > 0, prev, 0.0)
        ext = jnp.concatenate([prev, cur], axis=0)
        y = cur * convw[CONV_WIDTH - 1:CONV_WIDTH, col:col + B_HEAD_DIM]
        for back in range(1, CONV_WIDTH):
            sh = ext[16 - back:16 - back + PAIR]
            y = y + sh * convw[CONV_WIDTH - 1 - back:CONV_WIDTH - back, col:col + B_HEAD_DIM]
        return y * _sigmoid(y)

    def l2n(x):
        return x * lax.rsqrt(jnp.sum(x * x, axis=-1, keepdims=True) + EPS)

    def split(a):
        hi = a.astype(BF16)
        return hi, (a - hi.astype(F32)).astype(BF16)

    def dot3(a_parts, b_parts):
        (ah, al), (bh, bl) = a_parts, b_parts
        return (_dot(jnp.concatenate([ah, al], axis=1), jnp.concatenate([bh, bh], axis=0))
                + _dot(ah, bl))

    heads = range(B_HEADS)

    def phase1(p, carry):
        r0 = pl.multiple_of(p * PAIR, PAIR)
        rows = pl.ds(r0, PAIR)
        ab = ab_ref[rows, :]
        g_col = neg_exp_a_row * _softplus(ab + dt_row)
        gc_col = _dot(lower_bd, g_col, HI)
        beta_col = _sigmoid(ab)
        abt = abt_ref[p]
        g_row = neg_exp_a_col * _softplus(abt + dt_col)
        gc_row = _dot(g_row, upper_bd, HI)
        q = [l2n(conv_silu(p, h * B_HEAD_DIM)) * (B_HEAD_DIM ** -0.5) for h in heads]
        k = [l2n(conv_silu(p, B_WIDTH + h * B_HEAD_DIM)) for h in heads]
        v = [conv_silu(p, 2 * B_WIDTH + h * B_HEAD_DIM) for h in heads]
        gcol = [gc_col[:, h:h + 1] for h in heads]
        beta = [beta_col[:, B_HEADS + h:B_HEADS + h + 1] for h in heads]
        kb = [k[h] * beta[h] for h in heads]
        qk_kk = [_dot_nt(jnp.concatenate([q[h], kb[h]], axis=0).astype(BF16), k[h].astype(BF16))
                 for h in heads]
        decay = [jnp.exp(jnp.where(causal_bd, gcol[h] - gc_row[h:h + 1, :], -jnp.inf)) for h in heads]
        intra = [(qk_kk[h][:PAIR] * decay[h]).astype(BF16) for h in heads]
        low = [jnp.where(diag, 0.0, qk_kk[h][PAIR:] * decay[h]) for h in heads]

        x = [eye - low[h] for h in heads]
        lp = [split(low[h]) for h in heads]
        pw = [dot3(lp[h], lp[h]) for h in heads]
        for _ in range(int(math.log2(CHUNK)) - 2):
            xp = [split(x[h]) for h in heads]
            pp = [split(pw[h]) for h in heads]
            r = [dot3((jnp.concatenate([xp[h][0], pp[h][0]], axis=0),
                       jnp.concatenate([xp[h][1], pp[h][1]], axis=0)), pp[h]) for h in heads]
            x = [x[h] + r[h][:PAIR] for h in heads]
            pw = [r[h][PAIR:] for h in heads]
        tinv = [x[h] + dot3(split(x[h]), split(pw[h])) for h in heads]

        eg = [jnp.exp(gcol[h]) for h in heads]
        uw = [_dot(tinv[h].astype(BF16),
                   jnp.concatenate([v[h] * beta[h], kb[h] * eg[h]], axis=1).astype(BF16)) for h in heads]
        for h in heads:
            g_last = jnp.where(first_chunk, gcol[h][CHUNK - 1:CHUNK], gcol[h][PAIR - 1:PAIR])
            kd = k[h] * jnp.exp(g_last - gcol[h])
            u_s[h, rows, :] = uw[h][:, :B_HEAD_DIM]
            w_s[h, rows, :] = uw[h][:, B_HEAD_DIM:].astype(BF16)
            qg_s[h, rows, :] = (q[h] * eg[h]).astype(BF16)
            for cc in range(2):
                rs = slice(cc * CHUNK, (cc + 1) * CHUNK)
                n = 2 * p + cc
                kdt_s[h, n] = kd[rs].T.astype(BF16)
                a_s[h, n] = intra[h][rs, rs]
                gl_s[h, n] = jnp.broadcast_to(
                    jnp.exp(gcol[h][(cc + 1) * CHUNK - 1:(cc + 1) * CHUNK]), (8, LANES))
        return carry

    lax.fori_loop(0, npair, phase1, 0)

    st_s[...] = jnp.zeros(st_s.shape, F32)
    onorm = onorm_ref[...]

    def phase2(n, carry):
        rows = pl.ds(pl.multiple_of(n * CHUNK, CHUNK), CHUNK)
        state = [st_s[h] for h in heads]
        sb = [state[h].astype(BF16) for h in heads]
        wq = [_dot(jnp.concatenate([w_s[h, rows, :], qg_s[h, rows, :]], axis=0), sb[h]) for h in heads]
        vb = [(u_s[h, rows, :] - wq[h][:CHUNK]).astype(BF16) for h in heads]
        upd = [_dot(kdt_s[h, n], vb[h]) for h in heads]
        av = [_dot(a_s[h, n], vb[h]) for h in heads]
        for h in heads:
            st_s[h] = state[h] * gl_s[h, n][0:1, :] + upd[h]
        for h in heads:
            o = wq[h][CHUNK:] + av[h]
            z = z_ref[rows, h * B_HEAD_DIM:(h + 1) * B_HEAD_DIM].astype(F32)
            o_ref[rows, h * B_HEAD_DIM:(h + 1) * B_HEAD_DIM] = (
                _rms(o, onorm) * (z * _sigmoid(z))).astype(o_ref.dtype)
        return carry

    lax.fori_loop(0, nchunk, phase2, 0)


def _gdn(qkv, z, ab, abt, conv_w, a_log, dt_bias, o_norm, batch, seq):
    m = batch * seq
    npair = seq // PAIR
    nchunk = seq // CHUNK
    abt3 = jnp.swapaxes(abt.reshape(2 * B_HEADS, m // PAIR, PAIR), 0, 1)
    zeros = jnp.zeros((B_HEADS,), F32)
    nea = jnp.concatenate([-jnp.exp(a_log.astype(F32)), zeros])
    dtb = jnp.concatenate([dt_bias.astype(F32), zeros])
    pcol = jnp.stack([nea, dtb], axis=1)
    prow = jnp.stack([nea, dtb], axis=0)
    const = lambda b: (0, 0)
    rows = lambda b: (b, 0)
    return pl.pallas_call(
        functools.partial(_gdn_kernel, seq=seq),
        grid=(batch,),
        in_specs=[
            pl.BlockSpec((seq, 3 * B_WIDTH), rows),
            pl.BlockSpec((seq, B_WIDTH), rows),
            pl.BlockSpec((seq, 2 * B_HEADS), rows),
            pl.BlockSpec((npair, 2 * B_HEADS, PAIR), lambda b: (b, 0, 0)),
            pl.BlockSpec((CONV_WIDTH, 3 * B_WIDTH), const),
            pl.BlockSpec((2 * B_HEADS, 2), const),
            pl.BlockSpec((2, 2 * B_HEADS), const),
            pl.BlockSpec((1, B_HEAD_DIM), const),
        ],
        out_specs=pl.BlockSpec((seq, B_WIDTH), rows),
        out_shape=jax.ShapeDtypeStruct((m, B_WIDTH), BF16),
        scratch_shapes=[
            pltpu.VMEM((B_HEADS, seq, B_HEAD_DIM), F32),
            pltpu.VMEM((B_HEADS, seq, B_HEAD_DIM), BF16),
            pltpu.VMEM((B_HEADS, seq, B_HEAD_DIM), BF16),
            pltpu.VMEM((B_HEADS, nchunk, B_HEAD_DIM, CHUNK), BF16),
            pltpu.VMEM((B_HEADS, nchunk, CHUNK, CHUNK), BF16),
            pltpu.VMEM((B_HEADS, nchunk, 8, LANES), F32),
            pltpu.VMEM((B_HEADS, B_HEAD_DIM, B_HEAD_DIM), F32),
        ],
        compiler_params=pltpu.CompilerParams(
            dimension_semantics=("arbitrary",), vmem_limit_bytes=VMEM_LIMIT),
        name="gdn",
    )(qkv, z, ab, abt3, conv_w.astype(F32), pcol, prow, o_norm.reshape(1, B_HEAD_DIM).astype(F32))


def _out_mlp_kernel(x_ref, oa_ref, ob_ref, wo_ref, g_post_mix_ref, g_pre_mlp_ref, w1_ref, w2_ref,
                    g_post_mlp_ref, o_ref, *, ff_chunk):
    mix = _dot(oa_ref[...], wo_ref[:A_WIDTH, :]) + _dot(ob_ref[...], wo_ref[A_WIDTH:, :])
    x1 = x_ref[...] + _rms(mix, g_post_mix_ref[...])
    h = _rms(x1, g_pre_mlp_ref[...]).astype(BF16)
    d_ff = w1_ref.shape[1]
    y = jnp.zeros(x1.shape, F32)
    for lo in range(0, d_ff, ff_chunk):
        a = jnp.maximum(_dot(h, w1_ref[:, lo:lo + ff_chunk]), 0.0)
        y = y + _dot((a * a).astype(BF16), w2_ref[lo:lo + ff_chunk, :])
    o_ref[...] = x1 + _rms(y, g_post_mlp_ref[...])


def _out_mlp(x2, oa, ob, w_out, g_post_mix, g_pre_mlp, w1, w2, g_post_mlp, tm, ff_chunk):
    m, d = x2.shape
    d_ff = w1.shape[1]
    const = lambda i: (0, 0)
    row = lambda i: (i, 0)
    once = dict(pipeline_mode=pl.Buffered(1))
    vec = lambda g: g.reshape(1, d).astype(F32)
    return pl.pallas_call(
        functools.partial(_out_mlp_kernel, ff_chunk=ff_chunk),
        grid=(m // tm,),
        in_specs=[
            pl.BlockSpec((tm, d), row),
            pl.BlockSpec((tm, A_WIDTH), row),
            pl.BlockSpec((tm, B_WIDTH), row),
            pl.BlockSpec((A_WIDTH + B_WIDTH, d), const, **once),
            pl.BlockSpec((1, d), const),
            pl.BlockSpec((1, d), const),
            pl.BlockSpec((d, d_ff), const, **once),
            pl.BlockSpec((d_ff, d), const, **once),
            pl.BlockSpec((1, d), const),
        ],
        out_specs=pl.BlockSpec((tm, d), row),
        out_shape=jax.ShapeDtypeStruct((m, d), F32),
        compiler_params=pltpu.CompilerParams(
            dimension_semantics=("arbitrary",), vmem_limit_bytes=VMEM_LIMIT),
        name="out_mlp",
    )(x2, oa, ob, w_out.astype(BF16), vec(g_post_mix), vec(g_pre_mlp),
      w1.astype(BF16), w2.astype(BF16), vec(g_post_mlp))


def kernel(x, w_in, c_norm, w_uk, w_uv, rel_bias, conv_w, a_log, dt_bias, o_norm, w_out,
           pre_norm_mix, post_norm_mix, pre_norm_mlp, post_norm_mlp, w_mlp_in, w_mlp_out):
    batch, seq, d = x.shape
    m = batch * seq
    tm = min(512, m)
    tq = min(256, seq)
    xs = x.reshape(m, d)
    for l in range(w_in.shape[0]):
        qa, qi, qkv, z, c, kidx, wi, ab, abt = _in_proj(xs, pre_norm_mix[l], w_in[l], c_norm[l], tm)
        oa = _dsa(qa, qi, wi, kidx, c, w_uk[l], w_uv[l], rel_bias, batch, seq, tq, nbisect=16)
        ob = _gdn(qkv, z, ab, abt, conv_w[l], a_log[l], dt_bias[l], o_norm[l], batch, seq)
        xs = _out_mlp(xs, oa, ob, w_out[l], post_norm_mix[l], pre_norm_mlp[l],
                      w_mlp_in[l], w_mlp_out[l], post_norm_mlp[l], tm, ff_chunk=1024)
    return xs.reshape(batch, seq, d)
```

```python
import functools
import math

import numpy as np
import jax
import jax.numpy as jnp
from jax import lax
from jax.experimental import pallas as pl
from jax.experimental.pallas import tpu as pltpu

F32 = jnp.float32
BF16 = jnp.bfloat16

EPS = 1e-6
CHUNK = 64
A_HEADS = 8
A_HEAD_DIM = 64
A_WIDTH = A_HEADS * A_HEAD_DIM
A_LATENT = 128
IDX_HEADS = 8
IDX_DIM = 64
TOPK_MAX = 256
REL_BUCKETS = 32
REL_MAX_DIST = 128
B_HEADS = 4
B_HEAD_DIM = 128
B_WIDTH = B_HEADS * B_HEAD_DIM
CONV_WIDTH = 4
IN_SPLITS = (A_WIDTH, A_LATENT, IDX_HEADS * IDX_DIM, IDX_DIM, IDX_HEADS,
             B_WIDTH, B_WIDTH, B_WIDTH, B_HEADS, B_HEADS, B_WIDTH)

LANES = 128
SUBLANES = 8
VMEM_LIMIT = 56 * 1024 * 1024
NEG_BIG = -1e30

NT_DIMS = (((1,), (1,)), ((), ()))


def _dot(a, b, precision=None):
    return jnp.dot(a, b, preferred_element_type=F32, precision=precision)


def _dot_nt(a, b, precision=None):
    return lax.dot_general(a, b, NT_DIMS, preferred_element_type=F32, precision=precision)


def _rms(x, g):
    return x * lax.rsqrt(jnp.mean(x * x, axis=-1, keepdims=True) + EPS) * g


def _sigmoid(x):
    return 1.0 / (1.0 + jnp.exp(-x))


def _softplus(x):
    return jnp.maximum(x, 0.0) + jnp.log1p(jnp.exp(-jnp.abs(x)))


_C_QKV = 0
_C_Z = _C_QKV + 3 * B_WIDTH
_C_CKV = _C_Z + B_WIDTH
_C_SMALL = _C_CKV + A_LATENT
_C_END = _C_SMALL + LANES
_R_QA = 0
_R_QI = _R_QA + A_WIDTH
_R_CKV = _R_QI + IDX_HEADS * IDX_DIM
_R_SMALL = _R_CKV + A_LATENT
_R_END = _R_SMALL + IDX_HEADS + 2 * B_HEADS


def _in_proj_kernel(x_ref, g_ref, w_ref, wt_ref, cn_ref, cnt_ref,
                    qat_ref, qit_ref, qkv_ref, z_ref, c_ref, ct_ref, kidx_ref, wit_ref, ab_ref, abt_ref,
                    *, ts):
    h = _rms(x_ref[...], g_ref[...]).astype(BF16)

    def proj(lo, hi):
        return _dot(h, w_ref[:, lo:hi])

    def proj_t(lo, hi):
        return _dot_nt(wt_ref[lo:hi, :], h)

    qkv_ref[...] = proj(_C_QKV, _C_Z).astype(BF16)
    z_ref[...] = proj(_C_Z, _C_CKV).astype(BF16)
    c_ref[...] = _rms(proj(_C_CKV, _C_SMALL), cn_ref[...]).astype(BF16)
    small = proj(_C_SMALL, _C_END)
    kidx_ref[...] = small[:, :IDX_DIM].astype(BF16)
    ab_ref[...] = small[:, IDX_DIM:IDX_DIM + 2 * B_HEADS]

    qat_ref[...] = proj_t(_R_QA, _R_QI).astype(BF16)
    qit_ref[...] = (proj_t(_R_QI, _R_CKV) * (IDX_DIM ** -0.5)).astype(BF16)
    ckv_t = proj_t(_R_CKV, _R_SMALL)
    c_t = (ckv_t * lax.rsqrt(jnp.mean(ckv_t * ckv_t, axis=0, keepdims=True) + EPS)
           * cnt_ref[...]).astype(BF16)
    for r in range(ct_ref.shape[0]):
        ct_ref[r] = c_t[:, r * ts:(r + 1) * ts]
    small_t = proj_t(_R_SMALL, _R_END)
    wit_ref[...] = small_t[:IDX_HEADS] * (IDX_HEADS ** -0.5)
    abt_ref[...] = small_t[IDX_HEADS:]


def _in_proj(x2, g, w_in, c_norm, tm, ts):
    m, d = x2.shape
    offs = np.cumsum((0,) + IN_SPLITS)
    seg = [w_in[:, offs[i]:offs[i + 1]] for i in range(len(IN_SPLITS))]
    q_a, c_kv, q_idx, k_idx, w_idx, q_b, k_b, v_b, a_b, b_b, z_b = seg
    pad = jnp.zeros((d, LANES - IDX_DIM - 2 * B_HEADS), w_in.dtype)
    w_all = jnp.concatenate([q_b, k_b, v_b, z_b, c_kv, k_idx, a_b, b_b, pad], axis=1).astype(BF16)
    w_t = jnp.concatenate([q_a, q_idx, c_kv, w_idx, a_b, b_b], axis=1).T.astype(BF16)
    const = lambda i: (0, 0)
    row = lambda i: (i, 0)
    col = lambda i: (0, i)
    out_shape = (
        jax.ShapeDtypeStruct((A_WIDTH, m), BF16),
        jax.ShapeDtypeStruct((IDX_HEADS * IDX_DIM, m), BF16),
        jax.ShapeDtypeStruct((m, 3 * B_WIDTH), BF16),
        jax.ShapeDtypeStruct((m, B_WIDTH), BF16),
        jax.ShapeDtypeStruct((m, A_LATENT), BF16),
        jax.ShapeDtypeStruct((m // ts, A_LATENT, ts), BF16),
        jax.ShapeDtypeStruct((m, IDX_DIM), BF16),
        jax.ShapeDtypeStruct((IDX_HEADS, m), F32),
        jax.ShapeDtypeStruct((m, 2 * B_HEADS), F32),
        jax.ShapeDtypeStruct((2 * B_HEADS, m), F32),
    )
    out_specs = (
        pl.BlockSpec((A_WIDTH, tm), col),
        pl.BlockSpec((IDX_HEADS * IDX_DIM, tm), col),
        pl.BlockSpec((tm, 3 * B_WIDTH), row),
        pl.BlockSpec((tm, B_WIDTH), row),
        pl.BlockSpec((tm, A_LATENT), row),
        pl.BlockSpec((tm // ts, A_LATENT, ts), lambda i: (i, 0, 0)),
        pl.BlockSpec((tm, IDX_DIM), row),
        pl.BlockSpec((IDX_HEADS, tm), col),
        pl.BlockSpec((tm, 2 * B_HEADS), row),
        pl.BlockSpec((2 * B_HEADS, tm), col),
    )
    return pl.pallas_call(
        functools.partial(_in_proj_kernel, ts=ts),
        grid=(m // tm,),
        in_specs=[
            pl.BlockSpec((tm, d), row),
            pl.BlockSpec((1, d), const),
            pl.BlockSpec((d, _C_END), const),
            pl.BlockSpec((_R_END, d), const),
            pl.BlockSpec((1, A_LATENT), const),
            pl.BlockSpec((A_LATENT, 1), const),
        ],
        out_specs=out_specs,
        out_shape=out_shape,
        compiler_params=pltpu.CompilerParams(
            dimension_semantics=("arbitrary",), vmem_limit_bytes=VMEM_LIMIT),
        name="in_proj",
    )(x2, g.reshape(1, d), w_all, w_t, c_norm.reshape(1, A_LATENT), c_norm.reshape(A_LATENT, 1))


def _fold_rows(x, op):
    r = x.shape[0]
    while r > SUBLANES:
        r //= 2
        x = op(x[:r], x[r:])
    return x


def _dsa_kernel(qat_ref, qit_ref, wit_ref, kidx_ref, c_ref, ct_ref, wuk_ref, wuvt_ref, bias_ref, o_ref,
                sc_ref, qabs_ref, m_ref, l_ref, acc_ref, *, tq, ts, topk, nbisect, seq):
    i = pl.program_id(1)
    nkb = i + 1
    q_pos = i * tq + lax.broadcasted_iota(jnp.int32, (1, tq), 1)
    limit = (q_pos // CHUNK + 1) * CHUNK
    kf = float(topk)
    ninf = -jnp.inf

    def key_ids(j):
        return j * ts + lax.broadcasted_iota(jnp.int32, (ts, 1), 0)

    def col_reduce(x8, op):
        return op(x8, axis=0, keepdims=True)

    wit = wit_ref[...]

    def score_body(j, carry):
        rmin, rmax = carry
        k = kidx_ref[j]
        s = jnp.zeros((ts, tq), F32)
        for h in range(IDX_HEADS):
            r = _dot(k, qit_ref[h * IDX_DIM:(h + 1) * IDX_DIM, :])
            s = s + jnp.maximum(r, 0.0) * wit[h:h + 1, :]
        adm = key_ids(j) < limit
        sc_ref[j] = jnp.where(adm, s, ninf)
        rmax = jnp.maximum(rmax, _fold_rows(jnp.where(adm, s, ninf), jnp.maximum))
        rmin = jnp.minimum(rmin, _fold_rows(jnp.where(adm, s, jnp.inf), jnp.minimum))
        return rmin, rmax

    rmin, rmax = lax.fori_loop(
        0, nkb, score_body,
        (jnp.full((SUBLANES, tq), jnp.inf, F32), jnp.full((SUBLANES, tq), ninf, F32)))
    rmin = col_reduce(rmin, jnp.min)
    rmax = col_reduce(rmax, jnp.max)

    def count(pred):
        def body(j, acc):
            return acc + _fold_rows(jnp.where(pred(sc_ref[j], j), 1.0, 0.0), jnp.add)
        acc = lax.fori_loop(0, nkb, body, jnp.zeros((SUBLANES, tq), F32))
        return col_reduce(acc, jnp.sum)

    def count_ge(v):
        return count(lambda s, j: s >= v)

    need = limit > topk
    top_is_thr = count_ge(rmax) >= kf
    done0 = jnp.logical_or(jnp.logical_not(need), top_is_thr)
    thr0 = jnp.where(need, rmax, ninf)

    def bisect_body(_, carry):
        lo, hi = carry
        mid = lo + 0.5 * (hi - lo)
        ge = count_ge(mid) >= kf
        return jnp.where(ge, mid, lo), jnp.where(ge, hi, mid)

    lo, hi = lax.fori_loop(0, nbisect, bisect_body, (rmin, rmax))

    def finish_cond(state):
        _, _, done = state
        return jnp.min(done) < 0.5

    def finish_body(state):
        hi, thr, done = state

        def max_body(j, acc):
            s = sc_ref[j]
            return jnp.maximum(acc, _fold_rows(jnp.where(s < hi, s, ninf), jnp.maximum))

        m = col_reduce(lax.fori_loop(0, nkb, max_body, jnp.full((SUBLANES, tq), ninf, F32)), jnp.max)
        ok = count_ge(m) >= kf
        open_ = done < 0.5
        thr = jnp.where(open_, m, thr)
        hi = jnp.where(open_, m, hi)
        done = jnp.where(jnp.logical_and(open_, ok), 1.0, done)
        return hi, thr, done

    _, thr, _ = lax.while_loop(finish_cond, finish_body,
                               (hi, thr0, jnp.where(done0, 1.0, 0.0)))

    n_gt = count(lambda s, j: s > thr)
    n_eq = count(lambda s, j: s == thr)
    n_tie = kf - n_gt
    split = jnp.logical_and(need, n_eq > n_tie)

    def tie_search(_):
        def body(_, carry):
            jlo, jhi = carry
            jm = jnp.floor((jlo + jhi) * 0.5)
            c = count(lambda s, j: jnp.logical_and(s == thr, key_ids(j).astype(F32) <= jm))
            ge = c >= n_tie
            return jnp.where(ge, jlo, jm), jnp.where(ge, jm, jhi)
        steps = int(math.ceil(math.log2(seq))) + 1
        _, jhi = lax.fori_loop(0, steps, body,
                               (jnp.full((1, tq), -1.0, F32), jnp.full((1, tq), seq - 1.0, F32)))
        return jnp.where(split, jhi, float(seq))

    last_tie = lax.cond(jnp.max(jnp.where(split, 1.0, 0.0)) > 0.5, tie_search,
                        lambda _: jnp.full((1, tq), float(seq), F32), 0)
    last_tie = jnp.where(need, last_tie, -1.0)

    def mask_bias(j):
        s = sc_ref[j]
        sel = jnp.logical_or(
            s > thr, jnp.logical_and(s == thr, key_ids(j).astype(F32) <= last_tie))
        return jnp.where(sel, 0.0, NEG_BIG)

    for h in range(A_HEADS):
        qh = _dot(wuk_ref[h], qat_ref[h * A_HEAD_DIM:(h + 1) * A_HEAD_DIM, :]) * (A_HEAD_DIM ** -0.5)
        qabs_ref[h] = qh.astype(BF16)
    m_ref[...] = jnp.full(m_ref.shape, NEG_BIG, F32)
    l_ref[...] = jnp.zeros(l_ref.shape, F32)
    acc_ref[...] = jnp.zeros(acc_ref.shape, F32)

    heads = range(A_HEADS)

    def attend(j, bias_of_head):
        cj = c_ref[j]
        ctj = ct_ref[j]
        mb = mask_bias(j)
        s = [_dot(cj, qabs_ref[h]) for h in heads]
        for h in heads:
            b = bias_of_head(h)
            s[h] = s[h] + (mb if b is None else mb + b)
        m_old = [m_ref[h] for h in heads]
        m_new = [jnp.maximum(m_old[h], col_reduce(_fold_rows(s[h], jnp.maximum), jnp.max))
                 for h in heads]
        alpha = [jnp.exp(m_old[h] - m_new[h]) for h in heads]
        p = [jnp.exp(s[h] - m_new[h]) for h in heads]
        psum = [col_reduce(_fold_rows(p[h], jnp.add), jnp.sum) for h in heads]
        pv = [_dot(ctj, p[h].astype(BF16)) for h in heads]
        for h in heads:
            l_ref[h] = alpha[h] * l_ref[h] + psum[h]
            acc_ref[h] = alpha[h] * acc_ref[h] + pv[h]
            m_ref[h] = m_new[h]

    def far_body(j, carry):
        attend(j, lambda h: None)
        return carry

    lax.fori_loop(0, jnp.maximum(i - 1, 0), far_body, 0)

    @pl.when(i > 0)
    def _():
        attend(i - 1, lambda h: bias_ref[h, :ts, :])

    attend(i, lambda h: bias_ref[h, ts:, :])

    outs = [_dot(wuvt_ref[h], (acc_ref[h] / l_ref[h]).astype(BF16)) for h in range(A_HEADS)]
    o_ref[...] = jnp.concatenate(outs, axis=0).T.astype(o_ref.dtype)


def _t5_bucket(rel):
    nb = REL_BUCKETS // 2
    max_exact = nb // 2
    side = jnp.where(rel > 0, nb, 0)
    n = jnp.abs(rel)
    nf = jnp.maximum(n, 1).astype(jnp.float32)
    large = max_exact + (jnp.log(nf / max_exact) / math.log(REL_MAX_DIST / max_exact)
                         * (nb - max_exact)).astype(jnp.int32)
    large = jnp.minimum(large, nb - 1)
    return side + jnp.where(n < max_exact, n, large)


def _near_bias_kernel(far_ref, rb_ref, bucket_ref, o_ref):
    bucket = bucket_ref[...]
    for h in range(A_HEADS):
        acc = jnp.zeros(bucket.shape, F32)
        for b in range(REL_BUCKETS):
            acc = jnp.where(bucket == b, rb_ref[b, h], acc)
        o_ref[h] = acc - rb_ref[far_ref[0], h]


def _near_bias(rel_bias, tq, ts):
    assert ts >= REL_MAX_DIST
    rel = (jnp.arange(2 * ts)[:, None] - ts) - jnp.arange(tq)[None, :]
    far = _t5_bucket(jnp.full((1,), -(ts + 1), jnp.int32))
    smem = pl.BlockSpec(memory_space=pltpu.SMEM)
    return pl.pallas_call(
        _near_bias_kernel,
        in_specs=[smem, smem, pl.BlockSpec(memory_space=pltpu.VMEM)],
        out_specs=pl.BlockSpec(memory_space=pltpu.VMEM),
        out_shape=jax.ShapeDtypeStruct((A_HEADS, 2 * ts, tq), F32),
        name="near_bias",
    )(far, rel_bias.astype(F32), _t5_bucket(rel).astype(jnp.int32))


def _dsa(qat, qit, wit, kidx, c, ct3, w_uk, w_uv, rel_bias, batch, seq, tq, nbisect):
    ts = tq
    m = batch * seq
    nq = seq // tq
    nks = seq // ts
    topk = min(TOPK_MAX, seq // 4)
    kidx3 = kidx.reshape(m // ts, ts, IDX_DIM)
    c3 = c.reshape(m // ts, ts, A_LATENT)
    wuk = w_uk.astype(BF16)
    wuv_t = jnp.swapaxes(w_uv, 1, 2).astype(BF16)
    bias = _near_bias(rel_bias, tq, ts)
    qcol = lambda b, i: (0, b * nq + i)
    kv = lambda b, i: (b, 0, 0)
    const3 = lambda b, i: (0, 0, 0)
    kern = functools.partial(_dsa_kernel, tq=tq, ts=ts, topk=topk, nbisect=nbisect, seq=seq)
    return pl.pallas_call(
        kern,
        grid=(batch, nq),
        in_specs=[
            pl.BlockSpec((A_WIDTH, tq), qcol),
            pl.BlockSpec((IDX_HEADS * IDX_DIM, tq), qcol),
            pl.BlockSpec((IDX_HEADS, tq), qcol),
            pl.BlockSpec((nks, ts, IDX_DIM), kv),
            pl.BlockSpec((nks, ts, A_LATENT), kv),
            pl.BlockSpec((nks, A_LATENT, ts), kv),
            pl.BlockSpec((A_HEADS, A_LATENT, A_HEAD_DIM), const3),
            pl.BlockSpec((A_HEADS, A_HEAD_DIM, A_LATENT), const3),
            pl.BlockSpec((A_HEADS, 2 * ts, tq), const3),
        ],
        out_specs=pl.BlockSpec((tq, A_WIDTH), lambda b, i: (b * nq + i, 0)),
        out_shape=jax.ShapeDtypeStruct((m, A_WIDTH), BF16),
        scratch_shapes=[
            pltpu.VMEM((nks, ts, tq), F32),
            pltpu.VMEM((A_HEADS, A_LATENT, tq), BF16),
            pltpu.VMEM((A_HEADS, 1, tq), F32),
            pltpu.VMEM((A_HEADS, 1, tq), F32),
            pltpu.VMEM((A_HEADS, A_LATENT, tq), F32),
        ],
        compiler_params=pltpu.CompilerParams(
            dimension_semantics=("arbitrary", "arbitrary"), vmem_limit_bytes=VMEM_LIMIT),
        name="dsa",
    )(qat, qit, wit, kidx3, c3, ct3, wuk, wuv_t, bias)


PAIR = 2 * CHUNK
HI = lax.Precision.HIGHEST


def _gdn_kernel(qkv_ref, z_ref, ab_ref, abt_ref, convw_ref, pcol_ref, prow_ref, onorm_ref, o_ref,
                u_s, w_s, qg_s, kdt_s, a_s, gl_s, st_s, *, seq):
    npair = seq // PAIR
    nchunk = seq // CHUNK
    ri = lax.broadcasted_iota(jnp.int32, (PAIR, PAIR), 0)
    ci = lax.broadcasted_iota(jnp.int32, (PAIR, PAIR), 1)
    same = (ri // CHUNK) == (ci // CHUNK)
    lower_bd = jnp.where(jnp.logical_and(same, ci <= ri), 1.0, 0.0)
    upper_bd = jnp.where(jnp.logical_and(same, ri <= ci), 1.0, 0.0)
    causal_bd = jnp.logical_and(same, ci <= ri)
    diag = ci == ri
    eye = jnp.where(diag, 1.0, 0.0)
    neg_exp_a_col = pcol_ref[:, 0:1]
    dt_col = pcol_ref[:, 1:2]
    neg_exp_a_row = prow_ref[0:1, :]
    dt_row = prow_ref[1:2, :]
    convw = convw_ref[...]
    first_chunk = lax.broadcasted_iota(jnp.int32, (PAIR, 1), 0) < CHUNK

    def conv_silu(p, col):
        r0 = pl.multiple_of(p * PAIR, PAIR)
        cur = qkv_ref[pl.ds(r0, PAIR), col:col + B_HEAD_DIM].astype(F32)
        prev0 = pl.multiple_of(jnp.maximum(r0 - 16, 0), 16)
        prev = qkv_ref[pl.ds(prev0, 16), col:col + B_HEAD_DIM].astype(F32)
        prev = jnp.where(p > 0, prev, 0.0)
        ext = jnp.concatenate([prev, cur], axis=0)
        y = cur * convw[CONV_WIDTH - 1:CONV_WIDTH, col:col + B_HEAD_DIM]
        for back in range(1, CONV_WIDTH):
            sh = ext[16 - back:16 - back + PAIR]
            y = y + sh * convw[CONV_WIDTH - 1 - back:CONV_WIDTH - back, col:col + B_HEAD_DIM]
        return y * _sigmoid(y)

    def l2n(x):
        return x * lax.rsqrt(jnp.sum(x * x, axis=-1, keepdims=True) + EPS)

    def split(a):
        hi = a.astype(BF16)
        return hi, (a - hi.astype(F32)).astype(BF16)

    def dot3(a_parts, b_parts):
        (ah, al), (bh, bl) = a_parts, b_parts
        return (_dot(jnp.concatenate([ah, al], axis=1), jnp.concatenate([bh, bh], axis=0))
                + _dot(ah, bl))

    heads = range(B_HEADS)

    def phase1(p, carry):
        r0 = pl.multiple_of(p * PAIR, PAIR)
        rows = pl.ds(r0, PAIR)
        ab = ab_ref[rows, :]
        g_col = neg_exp_a_row * _softplus(ab + dt_row)
        gc_col = _dot(lower_bd, g_col, HI)
        beta_col = _sigmoid(ab)
        abt = abt_ref[p]
        g_row = neg_exp_a_col * _softplus(abt + dt_col)
        gc_row = _dot(g_row, upper_bd, HI)
        q = [l2n(conv_silu(p, h * B_HEAD_DIM)) * (B_HEAD_DIM ** -0.5) for h in heads]
        k = [l2n(conv_silu(p, B_WIDTH + h * B_HEAD_DIM)) for h in heads]
        v = [conv_silu(p, 2 * B_WIDTH + h * B_HEAD_DIM) for h in heads]
        gcol = [gc_col[:, h:h + 1] for h in heads]
        beta = [beta_col[:, B_HEADS + h:B_HEADS + h + 1] for h in heads]
        kb = [k[h] * beta[h] for h in heads]
        qk_kk = [_dot_nt(jnp.concatenate([q[h], kb[h]], axis=0).astype(BF16), k[h].astype(BF16))
                 for h in heads]
        decay = [jnp.exp(jnp.where(causal_bd, gcol[h] - gc_row[h:h + 1, :], -jnp.inf)) for h in heads]
        intra = [(qk_kk[h][:PAIR] * decay[h]).astype(BF16) for h in heads]
        low = [jnp.where(diag, 0.0, qk_kk[h][PAIR:] * decay[h]) for h in heads]

        x = [eye - low[h] for h in heads]
        lp = [split(low[h]) for h in heads]
        pw = [dot3(lp[h], lp[h]) for h in heads]
        for _ in range(int(math.log2(CHUNK)) - 2):
            xp = [split(x[h]) for h in heads]
            pp = [split(pw[h]) for h in heads]
            r = [dot3((jnp.concatenate([xp[h][0], pp[h][0]], axis=0),
                       jnp.concatenate([xp[h][1], pp[h][1]], axis=0)), pp[h]) for h in heads]
            x = [x[h] + r[h][:PAIR] for h in heads]
            pw = [r[h][PAIR:] for h in heads]
        tinv = [x[h] + dot3(split(x[h]), split(pw[h])) for h in heads]

        eg = [jnp.exp(gcol[h]) for h in heads]
        uw = [_dot(tinv[h].astype(BF16),
                   jnp.concatenate([v[h] * beta[h], kb[h] * eg[h]], axis=1).astype(BF16)) for h in heads]
        for h in heads:
            g_last = jnp.where(first_chunk, gcol[h][CHUNK - 1:CHUNK], gcol[h][PAIR - 1:PAIR])
            kd = k[h] * jnp.exp(g_last - gcol[h])
            u_s[h, rows, :] = uw[h][:, :B_HEAD_DIM]
            w_s[h, rows, :] = uw[h][:, B_HEAD_DIM:].astype(BF16)
            qg_s[h, rows, :] = (q[h] * eg[h]).astype(BF16)
            for cc in range(2):
                rs = slice(cc * CHUNK, (cc + 1) * CHUNK)
                n = 2 * p + cc
                kdt_s[h, n] = kd[rs].T.astype(BF16)
                a_s[h, n] = intra[h][rs, rs]
                gl_s[h, n] = jnp.broadcast_to(
                    jnp.exp(gcol[h][(cc + 1) * CHUNK - 1:(cc + 1) * CHUNK]), (8, LANES))
        return carry

    lax.fori_loop(0, npair, phase1, 0)

    st_s[...] = jnp.zeros(st_s.shape, F32)
    onorm = onorm_ref[...]

    def phase2(n, carry):
        rows = pl.ds(pl.multiple_of(n * CHUNK, CHUNK), CHUNK)
        state = [st_s[h] for h in heads]
        sb = [state[h].astype(BF16) for h in heads]
        wq = [_dot(jnp.concatenate([w_s[h, rows, :], qg_s[h, rows, :]], axis=0), sb[h]) for h in heads]
        vb = [(u_s[h, rows, :] - wq[h][:CHUNK]).astype(BF16) for h in heads]
        upd = [_dot(kdt_s[h, n], vb[h]) for h in heads]
        av = [_dot(a_s[h, n], vb[h]) for h in heads]
        for h in heads:
            st_s[h] = state[h] * gl_s[h, n][0:1, :] + upd[h]
        for h in heads:
            o = wq[h][CHUNK:] + av[h]
            z = z_ref[rows, h * B_HEAD_DIM:(h + 1) * B_HEAD_DIM].astype(F32)
            o_ref[rows, h * B_HEAD_DIM:(h + 1) * B_HEAD_DIM] = (
                _rms(o, onorm) * (z * _sigmoid(z))).astype(o_ref.dtype)
        return carry

    lax.fori_loop(0, nchunk, phase2, 0)


def _gdn(qkv, z, ab, abt, conv_w, a_log, dt_bias, o_norm, batch, seq):
    m = batch * seq
    npair = seq // PAIR
    nchunk = seq // CHUNK
    abt3 = jnp.swapaxes(abt.reshape(2 * B_HEADS, m // PAIR, PAIR), 0, 1)
    zeros = jnp.zeros((B_HEADS,), F32)
    nea = jnp.concatenate([-jnp.exp(a_log.astype(F32)), zeros])
    dtb = jnp.concatenate([dt_bias.astype(F32), zeros])
    pcol = jnp.stack([nea, dtb], axis=1)
    prow = jnp.stack([nea, dtb], axis=0)
    const = lambda b: (0, 0)
    rows = lambda b: (b, 0)
    return pl.pallas_call(
        functools.partial(_gdn_kernel, seq=seq),
        grid=(batch,),
        in_specs=[
            pl.BlockSpec((seq, 3 * B_WIDTH), rows),
            pl.BlockSpec((seq, B_WIDTH), rows),
            pl.BlockSpec((seq, 2 * B_HEADS), rows),
            pl.BlockSpec((npair, 2 * B_HEADS, PAIR), lambda b: (b, 0, 0)),
            pl.BlockSpec((CONV_WIDTH, 3 * B_WIDTH), const),
            pl.BlockSpec((2 * B_HEADS, 2), const),
            pl.BlockSpec((2, 2 * B_HEADS), const),
            pl.BlockSpec((1, B_HEAD_DIM), const),
        ],
        out_specs=pl.BlockSpec((seq, B_WIDTH), rows),
        out_shape=jax.ShapeDtypeStruct((m, B_WIDTH), BF16),
        scratch_shapes=[
            pltpu.VMEM((B_HEADS, seq, B_HEAD_DIM), F32),
            pltpu.VMEM((B_HEADS, seq, B_HEAD_DIM), BF16),
            pltpu.VMEM((B_HEADS, seq, B_HEAD_DIM), BF16),
            pltpu.VMEM((B_HEADS, nchunk, B_HEAD_DIM, CHUNK), BF16),
            pltpu.VMEM((B_HEADS, nchunk, CHUNK, CHUNK), BF16),
            pltpu.VMEM((B_HEADS, nchunk, 8, LANES), F32),
            pltpu.VMEM((B_HEADS, B_HEAD_DIM, B_HEAD_DIM), F32),
        ],
        compiler_params=pltpu.CompilerParams(
            dimension_semantics=("arbitrary",), vmem_limit_bytes=VMEM_LIMIT),
        name="gdn",
    )(qkv, z, ab, abt3, conv_w.astype(F32), pcol, prow, o_norm.reshape(1, B_HEAD_DIM).astype(F32))


def _out_mlp_kernel(x_ref, oa_ref, ob_ref, wo_ref, g_post_mix_ref, g_pre_mlp_ref, w1_ref, w2_ref,
                    g_post_mlp_ref, o_ref, *, ff_chunk):
    mix = _dot(oa_ref[...], wo_ref[:A_WIDTH, :]) + _dot(ob_ref[...], wo_ref[A_WIDTH:, :])
    x1 = x_ref[...] + _rms(mix, g_post_mix_ref[...])
    h = _rms(x1, g_pre_mlp_ref[...]).astype(BF16)
    d_ff = w1_ref.shape[1]
    y = jnp.zeros(x1.shape, F32)
    for lo in range(0, d_ff, ff_chunk):
        a = jnp.maximum(_dot(h, w1_ref[:, lo:lo + ff_chunk]), 0.0)
        y = y + _dot((a * a).astype(BF16), w2_ref[lo:lo + ff_chunk, :])
    o_ref[...] = x1 + _rms(y, g_post_mlp_ref[...])


def _out_mlp(x2, oa, ob, w_out, g_post_mix, g_pre_mlp, w1, w2, g_post_mlp, tm, ff_chunk):
    m, d = x2.shape
    d_ff = w1.shape[1]
    const = lambda i: (0, 0)
    row = lambda i: (i, 0)
    once = dict(pipeline_mode=pl.Buffered(1))
    vec = lambda g: g.reshape(1, d).astype(F32)
    return pl.pallas_call(
        functools.partial(_out_mlp_kernel, ff_chunk=ff_chunk),
        grid=(m // tm,),
        in_specs=[
            pl.BlockSpec((tm, d), row),
            pl.BlockSpec((tm, A_WIDTH), row),
            pl.BlockSpec((tm, B_WIDTH), row),
            pl.BlockSpec((A_WIDTH + B_WIDTH, d), const, **once),
            pl.BlockSpec((1, d), const),
            pl.BlockSpec((1, d), const),
            pl.BlockSpec((d, d_ff), const, **once),
            pl.BlockSpec((d_ff, d), const, **once),
            pl.BlockSpec((1, d), const),
        ],
        out_specs=pl.BlockSpec((tm, d), row),
        out_shape=jax.ShapeDtypeStruct((m, d), F32),
        compiler_params=pltpu.CompilerParams(
            dimension_semantics=("arbitrary",), vmem_limit_bytes=VMEM_LIMIT),
        name="out_mlp",
    )(x2, oa, ob, w_out.astype(BF16), vec(g_post_mix), vec(g_pre_mlp),
      w1.astype(BF16), w2.astype(BF16), vec(g_post_mlp))


def kernel(x, w_in, c_norm, w_uk, w_uv, rel_bias, conv_w, a_log, dt_bias, o_norm, w_out,
           pre_norm_mix, post_norm_mix, pre_norm_mlp, post_norm_mlp, w_mlp_in, w_mlp_out):
    batch, seq, d = x.shape
    m = batch * seq
    tm = min(512, m)
    tq = min(256, seq)
    xs = x.reshape(m, d)
    for l in range(w_in.shape[0]):
        qat, qit, qkv, z, c, ct3, kidx, wit, ab, abt = _in_proj(
            xs, pre_norm_mix[l], w_in[l], c_norm[l], tm, tq)
        oa = _dsa(qat, qit, wit, kidx, c, ct3, w_uk[l], w_uv[l], rel_bias, batch, seq, tq, nbisect=16)
        ob = _gdn(qkv, z, ab, abt, conv_w[l], a_log[l], dt_bias[l], o_norm[l], batch, seq)
        xs = _out_mlp(xs, oa, ob, w_out[l], post_norm_mix[l], pre_norm_mlp[l],
                      w_mlp_in[l], w_mlp_out[l], post_norm_mlp[l], tm, ff_chunk=1024)
    return xs.reshape(batch, seq, d)
```

```python
import functools
import math

import numpy as np
import jax
import jax.numpy as jnp
from jax import lax
from jax.experimental import pallas as pl
from jax.experimental.pallas import tpu as pltpu

F32 = jnp.float32
BF16 = jnp.bfloat16

EPS = 1e-6
CHUNK = 64
A_HEADS = 8
A_HEAD_DIM = 64
A_WIDTH = A_HEADS * A_HEAD_DIM
A_LATENT = 128
IDX_HEADS = 8
IDX_DIM = 64
TOPK_MAX = 256
REL_BUCKETS = 32
REL_MAX_DIST = 128
B_HEADS = 4
B_HEAD_DIM = 128
B_WIDTH = B_HEADS * B_HEAD_DIM
CONV_WIDTH = 4
IN_SPLITS = (A_WIDTH, A_LATENT, IDX_HEADS * IDX_DIM, IDX_DIM, IDX_HEADS,
             B_WIDTH, B_WIDTH, B_WIDTH, B_HEADS, B_HEADS, B_WIDTH)

LANES = 128
SUBLANES = 8
VMEM_LIMIT = 56 * 1024 * 1024
NEG_BIG = -1e30
LOG2E = 1.0 / math.log(2.0)
DENOM_ROWS = 16

NT_DIMS = (((1,), (1,)), ((), ()))


def _dot(a, b, precision=None):
    return jnp.dot(a, b, preferred_element_type=F32, precision=precision)


def _dot_nt(a, b, precision=None):
    return lax.dot_general(a, b, NT_DIMS, preferred_element_type=F32, precision=precision)


def _rms(x, g):
    return x * lax.rsqrt(jnp.mean(x * x, axis=-1, keepdims=True) + EPS) * g


def _sigmoid(x):
    return 1.0 / (1.0 + jnp.exp(-x))


def _softplus(x):
    return jnp.maximum(x, 0.0) + jnp.log1p(jnp.exp(-jnp.abs(x)))


_C_QKV = 0
_C_Z = _C_QKV + 3 * B_WIDTH
_C_CKV = _C_Z + B_WIDTH
_C_SMALL = _C_CKV + A_LATENT
_C_END = _C_SMALL + LANES
_R_QA = 0
_R_QI = _R_QA + A_WIDTH
_R_CKV = _R_QI + IDX_HEADS * IDX_DIM
_R_SMALL = _R_CKV + A_LATENT
_R_END = _R_SMALL + IDX_HEADS + 2 * B_HEADS


def _in_proj_kernel(x_ref, g_ref, w_ref, wt_ref, cn_ref, cnt_ref,
                    qat_ref, qit_ref, qkv_ref, z_ref, c_ref, ct_ref, kidx_ref, wit_ref, ab_ref, abt_ref,
                    *, ts):
    h = _rms(x_ref[...], g_ref[...]).astype(BF16)

    def proj(lo, hi):
        return _dot(h, w_ref[:, lo:hi])

    def proj_t(lo, hi):
        return _dot_nt(wt_ref[lo:hi, :], h)

    qkv_ref[...] = proj(_C_QKV, _C_Z).astype(BF16)
    z_ref[...] = proj(_C_Z, _C_CKV).astype(BF16)
    c_ref[...] = _rms(proj(_C_CKV, _C_SMALL), cn_ref[...]).astype(BF16)
    small = proj(_C_SMALL, _C_END)
    kidx_ref[...] = small[:, :IDX_DIM].astype(BF16)
    ab_ref[...] = small[:, IDX_DIM:IDX_DIM + 2 * B_HEADS]

    qat_ref[...] = proj_t(_R_QA, _R_QI).astype(BF16)
    qit_ref[...] = (proj_t(_R_QI, _R_CKV) * (IDX_DIM ** -0.5)).astype(BF16)
    ckv_t = proj_t(_R_CKV, _R_SMALL)
    c_t = (ckv_t * lax.rsqrt(jnp.mean(ckv_t * ckv_t, axis=0, keepdims=True) + EPS)
           * cnt_ref[...]).astype(BF16)
    for r in range(ct_ref.shape[0]):
        ct_ref[r] = c_t[:, r * ts:(r + 1) * ts]
    small_t = proj_t(_R_SMALL, _R_END)
    wit_ref[...] = small_t[:IDX_HEADS] * (IDX_HEADS ** -0.5)
    abt_ref[...] = small_t[IDX_HEADS:]


def _in_proj(x2, g, w_in, c_norm, tm, ts):
    m, d = x2.shape
    offs = np.cumsum((0,) + IN_SPLITS)
    seg = [w_in[:, offs[i]:offs[i + 1]] for i in range(len(IN_SPLITS))]
    q_a, c_kv, q_idx, k_idx, w_idx, q_b, k_b, v_b, a_b, b_b, z_b = seg
    pad = jnp.zeros((d, LANES - IDX_DIM - 2 * B_HEADS), w_in.dtype)
    w_all = jnp.concatenate([q_b, k_b, v_b, z_b, c_kv, k_idx, a_b, b_b, pad], axis=1).astype(BF16)
    w_t = jnp.concatenate([q_a, q_idx, c_kv, w_idx, a_b, b_b], axis=1).T.astype(BF16)
    const = lambda i: (0, 0)
    row = lambda i: (i, 0)
    col = lambda i: (0, i)
    out_shape = (
        jax.ShapeDtypeStruct((A_WIDTH, m), BF16),
        jax.ShapeDtypeStruct((IDX_HEADS * IDX_DIM, m), BF16),
        jax.ShapeDtypeStruct((m, 3 * B_WIDTH), BF16),
        jax.ShapeDtypeStruct((m, B_WIDTH), BF16),
        jax.ShapeDtypeStruct((m, A_LATENT), BF16),
        jax.ShapeDtypeStruct((m // ts, A_LATENT, ts), BF16),
        jax.ShapeDtypeStruct((m, IDX_DIM), BF16),
        jax.ShapeDtypeStruct((IDX_HEADS, m), F32),
        jax.ShapeDtypeStruct((m, 2 * B_HEADS), F32),
        jax.ShapeDtypeStruct((2 * B_HEADS, m), F32),
    )
    out_specs = (
        pl.BlockSpec((A_WIDTH, tm), col),
        pl.BlockSpec((IDX_HEADS * IDX_DIM, tm), col),
        pl.BlockSpec((tm, 3 * B_WIDTH), row),
        pl.BlockSpec((tm, B_WIDTH), row),
        pl.BlockSpec((tm, A_LATENT), row),
        pl.BlockSpec((tm // ts, A_LATENT, ts), lambda i: (i, 0, 0)),
        pl.BlockSpec((tm, IDX_DIM), row),
        pl.BlockSpec((IDX_HEADS, tm), col),
        pl.BlockSpec((tm, 2 * B_HEADS), row),
        pl.BlockSpec((2 * B_HEADS, tm), col),
    )
    return pl.pallas_call(
        functools.partial(_in_proj_kernel, ts=ts),
        grid=(m // tm,),
        in_specs=[
            pl.BlockSpec((tm, d), row),
            pl.BlockSpec((1, d), const),
            pl.BlockSpec((d, _C_END), const),
            pl.BlockSpec((_R_END, d), const),
            pl.BlockSpec((1, A_LATENT), const),
            pl.BlockSpec((A_LATENT, 1), const),
        ],
        out_specs=out_specs,
        out_shape=out_shape,
        compiler_params=pltpu.CompilerParams(
            dimension_semantics=("arbitrary",), vmem_limit_bytes=VMEM_LIMIT),
        name="in_proj",
    )(x2, g.reshape(1, d), w_all, w_t, c_norm.reshape(1, A_LATENT), c_norm.reshape(A_LATENT, 1))


def _fold_rows(x, op):
    r = x.shape[0]
    while r > SUBLANES:
        r //= 2
        x = op(x[:r], x[r:])
    return x


def _dsa_kernel(qat_ref, qit_ref, wit_ref, kidx_ref, c_ref, ct_ref, wuk_ref, wuvt_ref, bias_ref, o_ref,
                sc_ref, qabs_ref, m_ref, acc_ref, tie_base_ref, *, tq, ts, topk, nbisect,
                head_group):
    i = pl.program_id(1)
    nkb = i + 1
    q_pos = i * tq + lax.broadcasted_iota(jnp.int32, (1, tq), 1)
    limit = (q_pos // CHUNK + 1) * CHUNK
    kf = float(topk)
    ninf = -jnp.inf

    def key_ids(j):
        return j * ts + lax.broadcasted_iota(jnp.int32, (ts, 1), 0)

    def col_reduce(x8, op):
        return op(x8, axis=0, keepdims=True)

    wit = wit_ref[...]

    def score_body(j, carry):
        rmin, rmax = carry
        k = kidx_ref[j]
        r = [_dot(k, qit_ref[h * IDX_DIM:(h + 1) * IDX_DIM, :]) for h in range(IDX_HEADS)]
        s = jnp.maximum(r[0], 0.0) * wit[0:1, :]
        for h in range(1, IDX_HEADS):
            s = s + jnp.maximum(r[h], 0.0) * wit[h:h + 1, :]
        adm = key_ids(j) < limit
        sc_ref[j] = jnp.where(adm, s, ninf)
        rmax = jnp.maximum(rmax, _fold_rows(jnp.where(adm, s, ninf), jnp.maximum))
        rmin = jnp.minimum(rmin, _fold_rows(jnp.where(adm, s, jnp.inf), jnp.minimum))
        return rmin, rmax

    rmin, rmax = lax.fori_loop(
        0, nkb, score_body,
        (jnp.full((SUBLANES, tq), jnp.inf, F32), jnp.full((SUBLANES, tq), ninf, F32)))
    rmin = col_reduce(rmin, jnp.min)
    rmax = col_reduce(rmax, jnp.max)

    def count(pred):
        def body(j, acc):
            return acc + _fold_rows(jnp.where(pred(sc_ref[j], j), 1.0, 0.0), jnp.add)
        acc = lax.fori_loop(0, nkb, body, jnp.zeros((SUBLANES, tq), F32))
        return col_reduce(acc, jnp.sum)

    def count_ge(v):
        return count(lambda s, j: s >= v)

    need = limit > topk

    def bisect_body(_, carry):
        lo, hi, n_hi = carry
        mid = lo + 0.5 * (hi - lo)
        c = count_ge(mid)
        ge = c >= kf
        return jnp.where(ge, mid, lo), jnp.where(ge, hi, mid), jnp.where(ge, n_hi, c)

    hi0 = rmax + jnp.abs(rmax) * 1e-6 + 1e-30
    _, hi, n_hi = lax.fori_loop(0, nbisect, bisect_body, (rmin, hi0, jnp.zeros((1, tq), F32)))

    def max_below(v):
        def body(j, acc):
            s = sc_ref[j]
            return jnp.maximum(acc, _fold_rows(jnp.where(s < v, s, ninf), jnp.maximum))
        return col_reduce(lax.fori_loop(0, nkb, body, jnp.full((SUBLANES, tq), ninf, F32)), jnp.max)

    def finish_cond(state):
        return jnp.min(state[-1]) < 0.5

    def finish_body(state):
        m, n_hi, thr, n_thr, done = state

        def body(j, carry):
            cnt, nxt = carry
            s = sc_ref[j]
            below = s < m
            return (cnt + _fold_rows(jnp.where(below, 0.0, 1.0), jnp.add),
                    jnp.maximum(nxt, _fold_rows(jnp.where(below, s, ninf), jnp.maximum)))

        cnt, nxt = lax.fori_loop(0, nkb, body, (jnp.zeros((SUBLANES, tq), F32),
                                               jnp.full((SUBLANES, tq), ninf, F32)))
        c = col_reduce(cnt, jnp.sum)
        open_ = done < 0.5
        hit = jnp.logical_and(open_, c >= kf)
        miss = jnp.logical_and(open_, c < kf)
        thr = jnp.where(hit, m, thr)
        n_thr = jnp.where(hit, c, n_thr)
        n_hi = jnp.where(miss, c, n_hi)
        m = jnp.where(miss, col_reduce(nxt, jnp.max), m)
        return m, n_hi, thr, n_thr, jnp.where(hit, 1.0, done)

    lowest = float(jnp.finfo(F32).min)
    zero = jnp.zeros((1, tq), F32)
    _, n_hi, thr, n_thr, _ = lax.while_loop(
        finish_cond, finish_body,
        (max_below(hi), n_hi, jnp.full((1, tq), lowest, F32), zero, jnp.where(need, 0.0, 1.0)))

    n_tie = jnp.where(need, kf - n_hi, 0.0)
    surplus = jnp.max(jnp.where(jnp.logical_and(need, n_thr - n_hi > n_tie), 1.0, 0.0)) > 0.5
    tie_base_ref[...] = zero
    kr = lax.broadcasted_iota(jnp.int32, (ts, ts), 0)
    kc = lax.broadcasted_iota(jnp.int32, (ts, ts), 1)

    def mask_bias(j):
        def plain(_):
            return jnp.where(sc_ref[j] >= thr, 0.0, NEG_BIG)

        def with_ties(_):
            s = sc_ref[j]
            tie = s == thr
            prefix_ones = jnp.where(kc <= kr, 1.0, 0.0).astype(BF16)
            rank = _dot(prefix_ones, jnp.where(tie, 1.0, 0.0).astype(BF16)) + tie_base_ref[...]
            tie_base_ref[...] = rank[ts - 1:ts, :]
            keep = jnp.where(tie, jnp.where(rank <= n_tie, 0.0, NEG_BIG), NEG_BIG)
            return jnp.where(s > thr, 0.0, keep)

        return lax.cond(surplus, with_ties, plain, 0)

    for h in range(A_HEADS):
        qh = _dot(wuk_ref[h], qat_ref[h * A_HEAD_DIM:(h + 1) * A_HEAD_DIM, :])
        qabs_ref[h] = (qh * (A_HEAD_DIM ** -0.5 * LOG2E)).astype(BF16)
    m_ref[...] = jnp.full(m_ref.shape, NEG_BIG, F32)
    acc_ref[...] = jnp.zeros(acc_ref.shape, F32)
    ones_rows = jnp.ones((DENOM_ROWS, ts), BF16)

    def attend_heads(heads, cj, ctj, mb, bias_of_head):
        s = {h: _dot(cj, qabs_ref[h]) for h in heads}
        for h in heads:
            b = bias_of_head(h)
            s[h] = s[h] + (mb if b is None else mb + b)
        m_old = {h: m_ref[h] for h in heads}
        m_new = {h: jnp.maximum(m_old[h], col_reduce(_fold_rows(s[h], jnp.maximum), jnp.max))
                 for h in heads}
        alpha = {h: jnp.exp2(m_old[h] - m_new[h]) for h in heads}
        p = {h: jnp.exp2(s[h] - m_new[h]).astype(BF16) for h in heads}
        pv = {h: _dot(ctj, p[h]) for h in heads}
        for h in heads:
            acc_ref[h] = alpha[h] * acc_ref[h] + pv[h]
            m_ref[h] = m_new[h]

    def attend(j, bias_of_head):
        cj = c_ref[j]
        ctj = jnp.concatenate([ct_ref[j], ones_rows], axis=0)
        mb = mask_bias(j)
        for g in range(0, A_HEADS, head_group):
            attend_heads(range(g, g + head_group), cj, ctj, mb, bias_of_head)

    def far_body(j, carry):
        attend(j, lambda h: None)
        return carry

    lax.fori_loop(0, jnp.maximum(i - 1, 0), far_body, 0)

    @pl.when(i > 0)
    def _():
        attend(i - 1, lambda h: bias_ref[h, :ts, :])

    attend(i, lambda h: bias_ref[h, ts:, :])

    outs = [_dot(wuvt_ref[h],
                 (acc_ref[h, :A_LATENT, :] / acc_ref[h, A_LATENT:A_LATENT + 1, :]).astype(BF16))
            for h in range(A_HEADS)]
    o_ref[...] = jnp.concatenate(outs, axis=0).T.astype(o_ref.dtype)


def _t5_bucket(rel):
    nb = REL_BUCKETS // 2
    max_exact = nb // 2
    side = jnp.where(rel > 0, nb, 0)
    n = jnp.abs(rel)
    nf = jnp.maximum(n, 1).astype(jnp.float32)
    large = max_exact + (jnp.log(nf / max_exact) / math.log(REL_MAX_DIST / max_exact)
                         * (nb - max_exact)).astype(jnp.int32)
    large = jnp.minimum(large, nb - 1)
    return side + jnp.where(n < max_exact, n, large)


def _near_bias_kernel(far_ref, rb_ref, bucket_ref, o_ref):
    bucket = bucket_ref[...]
    for h in range(A_HEADS):
        acc = jnp.zeros(bucket.shape, F32)
        for b in range(REL_BUCKETS):
            acc = jnp.where(bucket == b, rb_ref[b, h], acc)
        o_ref[h] = (acc - rb_ref[far_ref[0], h]) * LOG2E


def _near_bias(rel_bias, tq, ts):
    assert ts >= REL_MAX_DIST
    rel = (jnp.arange(2 * ts)[:, None] - ts) - jnp.arange(tq)[None, :]
    far = _t5_bucket(jnp.full((1,), -(ts + 1), jnp.int32))
    smem = pl.BlockSpec(memory_space=pltpu.SMEM)
    return pl.pallas_call(
        _near_bias_kernel,
        in_specs=[smem, smem, pl.BlockSpec(memory_space=pltpu.VMEM)],
        out_specs=pl.BlockSpec(memory_space=pltpu.VMEM),
        out_shape=jax.ShapeDtypeStruct((A_HEADS, 2 * ts, tq), F32),
        name="near_bias",
    )(far, rel_bias.astype(F32), _t5_bucket(rel).astype(jnp.int32))


def _dsa(qat, qit, wit, kidx, c, ct3, w_uk, w_uv, rel_bias, batch, seq, tq, nbisect):
    ts = tq
    m = batch * seq
    nq = seq // tq
    nks = seq // ts
    topk = min(TOPK_MAX, seq // 4)
    kidx3 = kidx.reshape(m // ts, ts, IDX_DIM)
    c3 = c.reshape(m // ts, ts, A_LATENT)
    wuk = w_uk.astype(BF16)
    wuv_t = jnp.swapaxes(w_uv, 1, 2).astype(BF16)
    bias = _near_bias(rel_bias, tq, ts)
    qcol = lambda b, i: (0, b * nq + i)
    kv = lambda b, i: (b, 0, 0)
    const3 = lambda b, i: (0, 0, 0)
    kern = functools.partial(_dsa_kernel, tq=tq, ts=ts, topk=topk, nbisect=nbisect, head_group=A_HEADS)
    return pl.pallas_call(
        kern,
        grid=(batch, nq),
        in_specs=[
            pl.BlockSpec((A_WIDTH, tq), qcol),
            pl.BlockSpec((IDX_HEADS * IDX_DIM, tq), qcol),
            pl.BlockSpec((IDX_HEADS, tq), qcol),
            pl.BlockSpec((nks, ts, IDX_DIM), kv),
            pl.BlockSpec((nks, ts, A_LATENT), kv),
            pl.BlockSpec((nks, A_LATENT, ts), kv),
            pl.BlockSpec((A_HEADS, A_LATENT, A_HEAD_DIM), const3),
            pl.BlockSpec((A_HEADS, A_HEAD_DIM, A_LATENT), const3),
            pl.BlockSpec((A_HEADS, 2 * ts, tq), const3),
        ],
        out_specs=pl.BlockSpec((tq, A_WIDTH), lambda b, i: (b * nq + i, 0)),
        out_shape=jax.ShapeDtypeStruct((m, A_WIDTH), BF16),
        scratch_shapes=[
            pltpu.VMEM((nks, ts, tq), F32),
            pltpu.VMEM((A_HEADS, A_LATENT, tq), BF16),
            pltpu.VMEM((A_HEADS, 1, tq), F32),
            pltpu.VMEM((A_HEADS, A_LATENT + DENOM_ROWS, tq), F32),
            pltpu.VMEM((1, tq), F32),
        ],
        compiler_params=pltpu.CompilerParams(
            dimension_semantics=("arbitrary", "arbitrary"), vmem_limit_bytes=VMEM_LIMIT),
        name="dsa",
    )(qat, qit, wit, kidx3, c3, ct3, wuk, wuv_t, bias)


PAIR = 2 * CHUNK
HI = lax.Precision.HIGHEST


def _gdn_kernel(qkv_ref, z_ref, ab_ref, abt_ref, convw_ref, pcol_ref, prow_ref, onorm_ref, o_ref,
                u_s, w_s, qg_s, kdt_s, a_s, gl_s, st_s, *, seq):
    npair = seq // PAIR
    nchunk = seq // CHUNK
    ri = lax.broadcasted_iota(jnp.int32, (PAIR, PAIR), 0)
    ci = lax.broadcasted_iota(jnp.int32, (PAIR, PAIR), 1)
    same = (ri // CHUNK) == (ci // CHUNK)
    lower_bd = jnp.where(jnp.logical_and(same, ci <= ri), 1.0, 0.0)
    upper_bd = jnp.where(jnp.logical_and(same, ri <= ci), 1.0, 0.0)
    causal_bd = jnp.logical_and(same, ci <= ri)
    diag = ci == ri
    eye = jnp.where(diag, 1.0, 0.0)
    neg_exp_a_col = pcol_ref[:, 0:1]
    dt_col = pcol_ref[:, 1:2]
    neg_exp_a_row = prow_ref[0:1, :]
    dt_row = prow_ref[1:2, :]
    convw = convw_ref[...]
    first_chunk = lax.broadcasted_iota(jnp.int32, (PAIR, 1), 0) < CHUNK

    def conv_silu(p, col):
        r0 = pl.multiple_of(p * PAIR, PAIR)
        cur = qkv_ref[pl.ds(r0, PAIR), col:col + B_HEAD_DIM].astype(F32)
        prev0 = pl.multiple_of(jnp.maximum(r0 - 16, 0), 16)
        prev = qkv_ref[pl.ds(prev0, 16), col:col + B_HEAD_DIM].astype(F32)
        prev = jnp.where(p > 0, prev, 0.0)
        ext = jnp.concatenate([prev, cur], axis=0)
        y = cur * convw[CONV_WIDTH - 1:CONV_WIDTH, col:col + B_HEAD_DIM]
        for back in range(1, CONV_WIDTH):
            sh = ext[16 - back:16 - back + PAIR]
            y = y + sh * convw[CONV_WIDTH - 1 - back:CONV_WIDTH - back, col:col + B_HEAD_DIM]
        return y * _sigmoid(y)

    def l2n(x):
        return x * lax.rsqrt(jnp.sum(x * x, axis=-1, keepdims=True) + EPS)

    def split(a):
        hi = a.astype(BF16)
        return hi, (a - hi.astype(F32)).astype(BF16)

    def dot3(a_parts, b_parts):
        (ah, al), (bh, bl) = a_parts, b_parts
        return (_dot(jnp.concatenate([ah, al], axis=1), jnp.concatenate([bh, bh], axis=0))
                + _dot(ah, bl))

    heads = range(B_HEADS)

    def phase1(p, carry):
        r0 = pl.multiple_of(p * PAIR, PAIR)
        rows = pl.ds(r0, PAIR)
        ab = ab_ref[rows, :]
        g_col = neg_exp_a_row * _softplus(ab + dt_row)
        gc_col = _dot(lower_bd, g_col, HI)
        beta_col = _sigmoid(ab)
        abt = abt_ref[p]
        g_row = neg_exp_a_col * _softplus(abt + dt_col)
        gc_row = _dot(g_row, upper_bd, HI)
        q = [l2n(conv_silu(p, h * B_HEAD_DIM)) * (B_HEAD_DIM ** -0.5) for h in heads]
        k = [l2n(conv_silu(p, B_WIDTH + h * B_HEAD_DIM)) for h in heads]
        v = [conv_silu(p, 2 * B_WIDTH + h * B_HEAD_DIM) for h in heads]
        gcol = [gc_col[:, h:h + 1] for h in heads]
        beta = [beta_col[:, B_HEADS + h:B_HEADS + h + 1] for h in heads]
        kb = [k[h] * beta[h] for h in heads]
        qk_kk = [_dot_nt(jnp.concatenate([q[h], kb[h]], axis=0).astype(BF16), k[h].astype(BF16))
                 for h in heads]
        decay = [jnp.exp(jnp.where(causal_bd, gcol[h] - gc_row[h:h + 1, :], -jnp.inf)) for h in heads]
        intra = [(qk_kk[h][:PAIR] * decay[h]).astype(BF16) for h in heads]
        low = [jnp.where(diag, 0.0, qk_kk[h][PAIR:] * decay[h]) for h in heads]

        x = [eye - low[h] for h in heads]
        lp = [split(low[h]) for h in heads]
        pw = [dot3(lp[h], lp[h]) for h in heads]
        for _ in range(int(math.log2(CHUNK)) - 2):
            xp = [split(x[h]) for h in heads]
            pp = [split(pw[h]) for h in heads]
            r = [dot3((jnp.concatenate([xp[h][0], pp[h][0]], axis=0),
                       jnp.concatenate([xp[h][1], pp[h][1]], axis=0)), pp[h]) for h in heads]
            x = [x[h] + r[h][:PAIR] for h in heads]
            pw = [r[h][PAIR:] for h in heads]
        tinv = [x[h] + dot3(split(x[h]), split(pw[h])) for h in heads]

        eg = [jnp.exp(gcol[h]) for h in heads]
        uw = [_dot(tinv[h].astype(BF16),
                   jnp.concatenate([v[h] * beta[h], kb[h] * eg[h]], axis=1).astype(BF16)) for h in heads]
        for h in heads:
            g_last = jnp.where(first_chunk, gcol[h][CHUNK - 1:CHUNK], gcol[h][PAIR - 1:PAIR])
            kd = k[h] * jnp.exp(g_last - gcol[h])
            u_s[h, rows, :] = uw[h][:, :B_HEAD_DIM]
            w_s[h, rows, :] = uw[h][:, B_HEAD_DIM:].astype(BF16)
            qg_s[h, rows, :] = (q[h] * eg[h]).astype(BF16)
            for cc in range(2):
                rs = slice(cc * CHUNK, (cc + 1) * CHUNK)
                n = 2 * p + cc
                kdt_s[h, n] = kd[rs].T.astype(BF16)
                a_s[h, n] = intra[h][rs, rs]
                gl_s[h, n] = jnp.broadcast_to(
                    jnp.exp(gcol[h][(cc + 1) * CHUNK - 1:(cc + 1) * CHUNK]), (8, LANES))
        return carry

    lax.fori_loop(0, npair, phase1, 0)

    st_s[...] = jnp.zeros(st_s.shape, F32)
    onorm = onorm_ref[...]

    def phase2(n, carry):
        rows = pl.ds(pl.multiple_of(n * CHUNK, CHUNK), CHUNK)
        state = [st_s[h] for h in heads]
        sb = [state[h].astype(BF16) for h in heads]
        wq = [_dot(jnp.concatenate([w_s[h, rows, :], qg_s[h, rows, :]], axis=0), sb[h]) for h in heads]
        vb = [(u_s[h, rows, :] - wq[h][:CHUNK]).astype(BF16) for h in heads]
        upd = [_dot(kdt_s[h, n], vb[h]) for h in heads]
        av = [_dot(a_s[h, n], vb[h]) for h in heads]
        for h in heads:
            st_s[h] = state[h] * gl_s[h, n][0:1, :] + upd[h]
        for h in heads:
            o = wq[h][CHUNK:] + av[h]
            z = z_ref[rows, h * B_HEAD_DIM:(h + 1) * B_HEAD_DIM].astype(F32)
            o_ref[rows, h * B_HEAD_DIM:(h + 1) * B_HEAD_DIM] = (
                _rms(o, onorm) * (z * _sigmoid(z))).astype(o_ref.dtype)
        return carry

    lax.fori_loop(0, nchunk, phase2, 0)


def _gdn(qkv, z, ab, abt, conv_w, a_log, dt_bias, o_norm, batch, seq):
    m = batch * seq
    npair = seq // PAIR
    nchunk = seq // CHUNK
    abt3 = jnp.swapaxes(abt.reshape(2 * B_HEADS, m // PAIR, PAIR), 0, 1)
    zeros = jnp.zeros((B_HEADS,), F32)
    nea = jnp.concatenate([-jnp.exp(a_log.astype(F32)), zeros])
    dtb = jnp.concatenate([dt_bias.astype(F32), zeros])
    pcol = jnp.stack([nea, dtb], axis=1)
    prow = jnp.stack([nea, dtb], axis=0)
    const = lambda b: (0, 0)
    rows = lambda b: (b, 0)
    return pl.pallas_call(
        functools.partial(_gdn_kernel, seq=seq),
        grid=(batch,),
        in_specs=[
            pl.BlockSpec((seq, 3 * B_WIDTH), rows),
            pl.BlockSpec((seq, B_WIDTH), rows),
            pl.BlockSpec((seq, 2 * B_HEADS), rows),
            pl.BlockSpec((npair, 2 * B_HEADS, PAIR), lambda b: (b, 0, 0)),
            pl.BlockSpec((CONV_WIDTH, 3 * B_WIDTH), const),
            pl.BlockSpec((2 * B_HEADS, 2), const),
            pl.BlockSpec((2, 2 * B_HEADS), const),
            pl.BlockSpec((1, B_HEAD_DIM), const),
        ],
        out_specs=pl.BlockSpec((seq, B_WIDTH), rows),
        out_shape=jax.ShapeDtypeStruct((m, B_WIDTH), BF16),
        scratch_shapes=[
            pltpu.VMEM((B_HEADS, seq, B_HEAD_DIM), F32),
            pltpu.VMEM((B_HEADS, seq, B_HEAD_DIM), BF16),
            pltpu.VMEM((B_HEADS, seq, B_HEAD_DIM), BF16),
            pltpu.VMEM((B_HEADS, nchunk, B_HEAD_DIM, CHUNK), BF16),
            pltpu.VMEM((B_HEADS, nchunk, CHUNK, CHUNK), BF16),
            pltpu.VMEM((B_HEADS, nchunk, 8, LANES), F32),
            pltpu.VMEM((B_HEADS, B_HEAD_DIM, B_HEAD_DIM), F32),
        ],
        compiler_params=pltpu.CompilerParams(
            dimension_semantics=("arbitrary",), vmem_limit_bytes=VMEM_LIMIT),
        name="gdn",
    )(qkv, z, ab, abt3, conv_w.astype(F32), pcol, prow, o_norm.reshape(1, B_HEAD_DIM).astype(F32))


def _out_mlp_kernel(x_ref, oa_ref, ob_ref, wo_ref, g_post_mix_ref, g_pre_mlp_ref, w1_ref, w2_ref,
                    g_post_mlp_ref, o_ref, *, ff_chunk):
    mix = _dot(oa_ref[...], wo_ref[:A_WIDTH, :]) + _dot(ob_ref[...], wo_ref[A_WIDTH:, :])
    x1 = x_ref[...] + _rms(mix, g_post_mix_ref[...])
    h = _rms(x1, g_pre_mlp_ref[...]).astype(BF16)
    d_ff = w1_ref.shape[1]
    y = jnp.zeros(x1.shape, F32)
    for lo in range(0, d_ff, ff_chunk):
        a = jnp.maximum(_dot(h, w1_ref[:, lo:lo + ff_chunk]), 0.0)
        y = y + _dot((a * a).astype(BF16), w2_ref[lo:lo + ff_chunk, :])
    o_ref[...] = x1 + _rms(y, g_post_mlp_ref[...])


def _out_mlp(x2, oa, ob, w_out, g_post_mix, g_pre_mlp, w1, w2, g_post_mlp, tm, ff_chunk):
    m, d = x2.shape
    d_ff = w1.shape[1]
    const = lambda i: (0, 0)
    row = lambda i: (i, 0)
    once = dict(pipeline_mode=pl.Buffered(1))
    vec = lambda g: g.reshape(1, d).astype(F32)
    return pl.pallas_call(
        functools.partial(_out_mlp_kernel, ff_chunk=ff_chunk),
        grid=(m // tm,),
        in_specs=[
            pl.BlockSpec((tm, d), row),
            pl.BlockSpec((tm, A_WIDTH), row),
            pl.BlockSpec((tm, B_WIDTH), row),
            pl.BlockSpec((A_WIDTH + B_WIDTH, d), const, **once),
            pl.BlockSpec((1, d), const),
            pl.BlockSpec((1, d), const),
            pl.BlockSpec((d, d_ff), const, **once),
            pl.BlockSpec((d_ff, d), const, **once),
            pl.BlockSpec((1, d), const),
        ],
        out_specs=pl.BlockSpec((tm, d), row),
        out_shape=jax.ShapeDtypeStruct((m, d), F32),
        compiler_params=pltpu.CompilerParams(
            dimension_semantics=("arbitrary",), vmem_limit_bytes=VMEM_LIMIT),
        name="out_mlp",
    )(x2, oa, ob, w_out.astype(BF16), vec(g_post_mix), vec(g_pre_mlp),
      w1.astype(BF16), w2.astype(BF16), vec(g_post_mlp))


def kernel(x, w_in, c_norm, w_uk, w_uv, rel_bias, conv_w, a_log, dt_bias, o_norm, w_out,
           pre_norm_mix, post_norm_mix, pre_norm_mlp, post_norm_mlp, w_mlp_in, w_mlp_out):
    batch, seq, d = x.shape
    m = batch * seq
    tm = min(512, m)
    tq = min(256, seq)
    xs = x.reshape(m, d)
    for l in range(w_in.shape[0]):
        qat, qit, qkv, z, c, ct3, kidx, wit, ab, abt = _in_proj(
            xs, pre_norm_mix[l], w_in[l], c_norm[l], tm, tq)
        oa = _dsa(qat, qit, wit, kidx, c, ct3, w_uk[l], w_uv[l], rel_bias, batch, seq, tq, nbisect=16)
        ob = _gdn(qkv, z, ab, abt, conv_w[l], a_log[l], dt_bias[l], o_norm[l], batch, seq)
        xs = _out_mlp(xs, oa, ob, w_out[l], post_norm_mix[l], pre_norm_mlp[l],
                      w_mlp_in[l], w_mlp_out[l], post_norm_mlp[l], tm, ff_chunk=1024)
    return xs.reshape(batch, seq, d)
```

```python
import functools
import math

import numpy as np
import jax
import jax.numpy as jnp
from jax import lax
from jax.experimental import pallas as pl
from jax.experimental.pallas import tpu as pltpu

F32 = jnp.float32
BF16 = jnp.bfloat16

EPS = 1e-6
CHUNK = 64
A_HEADS = 8
A_HEAD_DIM = 64
A_WIDTH = A_HEADS * A_HEAD_DIM
A_LATENT = 128
IDX_HEADS = 8
IDX_DIM = 64
TOPK_MAX = 256
REL_BUCKETS = 32
REL_MAX_DIST = 128
B_HEADS = 4
B_HEAD_DIM = 128
B_WIDTH = B_HEADS * B_HEAD_DIM
CONV_WIDTH = 4
IN_SPLITS = (A_WIDTH, A_LATENT, IDX_HEADS * IDX_DIM, IDX_DIM, IDX_HEADS,
             B_WIDTH, B_WIDTH, B_WIDTH, B_HEADS, B_HEADS, B_WIDTH)

LANES = 128
SUBLANES = 8
VMEM_LIMIT = 56 * 1024 * 1024
NEG_BIG = -1e30
LOG2E = 1.0 / math.log(2.0)
DENOM_ROWS = 16

NT_DIMS = (((1,), (1,)), ((), ()))


def _dot(a, b, precision=None):
    return jnp.dot(a, b, preferred_element_type=F32, precision=precision)


def _dot_nt(a, b, precision=None):
    return lax.dot_general(a, b, NT_DIMS, preferred_element_type=F32, precision=precision)


def _rms(x, g):
    return x * lax.rsqrt(jnp.mean(x * x, axis=-1, keepdims=True) + EPS) * g


def _sigmoid(x):
    return 1.0 / (1.0 + jnp.exp(-x))


def _softplus(x):
    return jnp.maximum(x, 0.0) + jnp.log1p(jnp.exp(-jnp.abs(x)))


_C_QKV = 0
_C_Z = _C_QKV + 3 * B_WIDTH
_C_CKV = _C_Z + B_WIDTH
_C_SMALL = _C_CKV + A_LATENT
_C_END = _C_SMALL + LANES
_R_QA = 0
_R_QI = _R_QA + A_WIDTH
_R_CKV = _R_QI + IDX_HEADS * IDX_DIM
_R_SMALL = _R_CKV + A_LATENT
_R_END = _R_SMALL + IDX_HEADS + 2 * B_HEADS


def _in_proj_kernel(x_ref, g_ref, w_ref, wt_ref, cn_ref, cnt_ref,
                    qat_ref, qit_ref, qkv_ref, z_ref, c_ref, ct_ref, kidx_ref, wit_ref, ab_ref, abt_ref,
                    *, ts):
    h = _rms(x_ref[...], g_ref[...]).astype(BF16)

    def proj(lo, hi):
        return _dot(h, w_ref[:, lo:hi])

    def proj_t(lo, hi):
        return _dot_nt(wt_ref[lo:hi, :], h)

    qkv_ref[...] = proj(_C_QKV, _C_Z).astype(BF16)
    z_ref[...] = proj(_C_Z, _C_CKV).astype(BF16)
    c_ref[...] = _rms(proj(_C_CKV, _C_SMALL), cn_ref[...]).astype(BF16)
    small = proj(_C_SMALL, _C_END)
    kidx_ref[...] = small[:, :IDX_DIM].astype(BF16)
    ab_ref[...] = small[:, IDX_DIM:IDX_DIM + 2 * B_HEADS]

    qat_ref[...] = proj_t(_R_QA, _R_QI).astype(BF16)
    qit_ref[...] = (proj_t(_R_QI, _R_CKV) * (IDX_DIM ** -0.5)).astype(BF16)
    ckv_t = proj_t(_R_CKV, _R_SMALL)
    c_t = (ckv_t * lax.rsqrt(jnp.mean(ckv_t * ckv_t, axis=0, keepdims=True) + EPS)
           * cnt_ref[...]).astype(BF16)
    for r in range(ct_ref.shape[0]):
        ct_ref[r] = c_t[:, r * ts:(r + 1) * ts]
    small_t = proj_t(_R_SMALL, _R_END)
    wit_ref[...] = small_t[:IDX_HEADS] * (IDX_HEADS ** -0.5)
    abt_ref[...] = small_t[IDX_HEADS:]


def _in_proj(x2, g, w_in, c_norm, tm, ts):
    m, d = x2.shape
    offs = np.cumsum((0,) + IN_SPLITS)
    seg = [w_in[:, offs[i]:offs[i + 1]] for i in range(len(IN_SPLITS))]
    q_a, c_kv, q_idx, k_idx, w_idx, q_b, k_b, v_b, a_b, b_b, z_b = seg
    pad = jnp.zeros((d, LANES - IDX_DIM - 2 * B_HEADS), w_in.dtype)
    w_all = jnp.concatenate([q_b, k_b, v_b, z_b, c_kv, k_idx, a_b, b_b, pad], axis=1).astype(BF16)
    w_t = jnp.concatenate([q_a, q_idx, c_kv, w_idx, a_b, b_b], axis=1).T.astype(BF16)
    const = lambda i: (0, 0)
    row = lambda i: (i, 0)
    col = lambda i: (0, i)
    out_shape = (
        jax.ShapeDtypeStruct((A_WIDTH, m), BF16),
        jax.ShapeDtypeStruct((IDX_HEADS * IDX_DIM, m), BF16),
        jax.ShapeDtypeStruct((m, 3 * B_WIDTH), BF16),
        jax.ShapeDtypeStruct((m, B_WIDTH), BF16),
        jax.ShapeDtypeStruct((m, A_LATENT), BF16),
        jax.ShapeDtypeStruct((m // ts, A_LATENT, ts), BF16),
        jax.ShapeDtypeStruct((m, IDX_DIM), BF16),
        jax.ShapeDtypeStruct((IDX_HEADS, m), F32),
        jax.ShapeDtypeStruct((m, 2 * B_HEADS), F32),
        jax.ShapeDtypeStruct((2 * B_HEADS, m), F32),
    )
    out_specs = (
        pl.BlockSpec((A_WIDTH, tm), col),
        pl.BlockSpec((IDX_HEADS * IDX_DIM, tm), col),
        pl.BlockSpec((tm, 3 * B_WIDTH), row),
        pl.BlockSpec((tm, B_WIDTH), row),
        pl.BlockSpec((tm, A_LATENT), row),
        pl.BlockSpec((tm // ts, A_LATENT, ts), lambda i: (i, 0, 0)),
        pl.BlockSpec((tm, IDX_DIM), row),
        pl.BlockSpec((IDX_HEADS, tm), col),
        pl.BlockSpec((tm, 2 * B_HEADS), row),
        pl.BlockSpec((2 * B_HEADS, tm), col),
    )
    return pl.pallas_call(
        functools.partial(_in_proj_kernel, ts=ts),
        grid=(m // tm,),
        in_specs=[
            pl.BlockSpec((tm, d), row),
            pl.BlockSpec((1, d), const),
            pl.BlockSpec((d, _C_END), const),
            pl.BlockSpec((_R_END, d), const),
            pl.BlockSpec((1, A_LATENT), const),
            pl.BlockSpec((A_LATENT, 1), const),
        ],
        out_specs=out_specs,
        out_shape=out_shape,
        compiler_params=pltpu.CompilerParams(
            dimension_semantics=("arbitrary",), vmem_limit_bytes=VMEM_LIMIT),
        name="in_proj",
    )(x2, g.reshape(1, d), w_all, w_t, c_norm.reshape(1, A_LATENT), c_norm.reshape(A_LATENT, 1))


def _fold_rows(x, op):
    r = x.shape[0]
    while r > SUBLANES:
        r //= 2
        x = op(x[:r], x[r:])
    return x


def _dsa_kernel(qat_ref, qit_ref, wit_ref, kidx_ref, c_ref, ct_ref, wuk_ref, wuvt_ref, bias_ref, o_ref,
                sc_ref, qabs_ref, m_ref, acc_ref, tie_base_ref, range_ref, *, tq, ts, topk, nbisect):
    i = pl.program_id(1)
    nkb = i + 1
    q_pos = i * tq + lax.broadcasted_iota(jnp.int32, (1, tq), 1)
    limit = (q_pos // CHUNK + 1) * CHUNK
    kf = float(topk)
    ninf = -jnp.inf

    def key_ids(j):
        return j * ts + lax.broadcasted_iota(jnp.int32, (ts, 1), 0)

    def col_reduce(x8, op):
        return op(x8, axis=0, keepdims=True)

    wit = wit_ref[...]

    range_ref[0] = jnp.full((SUBLANES, tq), jnp.inf, F32)
    range_ref[1] = jnp.full((SUBLANES, tq), ninf, F32)

    def score_blocks(js, diagonal):
        k = jnp.concatenate([kidx_ref[j] for j in js], axis=0)
        r = [_dot(k, qit_ref[h * IDX_DIM:(h + 1) * IDX_DIM, :]) for h in range(IDX_HEADS)]
        s = jnp.maximum(r[0], 0.0) * wit[0:1, :]
        for h in range(1, IDX_HEADS):
            s = s + jnp.maximum(r[h], 0.0) * wit[h:h + 1, :]
        lo, hi = s, s
        if diagonal:
            adm = key_ids(js[0]) < limit
            s = jnp.where(adm, s, ninf)
            lo, hi = jnp.where(adm, s, jnp.inf), s
        for n, j in enumerate(js):
            sc_ref[j] = s[n * ts:(n + 1) * ts]
        range_ref[0] = jnp.minimum(range_ref[0], _fold_rows(lo, jnp.minimum))
        range_ref[1] = jnp.maximum(range_ref[1], _fold_rows(hi, jnp.maximum))

    def score_pair(jj, carry):
        score_blocks([2 * jj, 2 * jj + 1], False)
        return carry

    lax.fori_loop(0, i // 2, score_pair, 0)

    @pl.when(i % 2 == 1)
    def _():
        score_blocks([i - 1], False)

    score_blocks([i], True)
    rmin = col_reduce(range_ref[0], jnp.min)
    rmax = col_reduce(range_ref[1], jnp.max)

    def count(pred):
        def body(j, acc):
            return acc + _fold_rows(jnp.where(pred(sc_ref[j], j), 1.0, 0.0), jnp.add)
        acc = lax.fori_loop(0, nkb, body, jnp.zeros((SUBLANES, tq), F32))
        return col_reduce(acc, jnp.sum)

    def count_ge(v):
        return count(lambda s, j: s >= v)

    need = limit > topk

    def bisect_body(_, carry):
        lo, hi, n_hi = carry
        mid = lo + 0.5 * (hi - lo)
        c = count_ge(mid)
        ge = c >= kf
        return jnp.where(ge, mid, lo), jnp.where(ge, hi, mid), jnp.where(ge, n_hi, c)

    hi0 = rmax + jnp.abs(rmax) * 1e-6 + 1e-30
    _, hi, n_hi = lax.fori_loop(0, nbisect, bisect_body, (rmin, hi0, jnp.zeros((1, tq), F32)))

    def max_below(v):
        def body(j, acc):
            s = sc_ref[j]
            return jnp.maximum(acc, _fold_rows(jnp.where(s < v, s, ninf), jnp.maximum))
        return col_reduce(lax.fori_loop(0, nkb, body, jnp.full((SUBLANES, tq), ninf, F32)), jnp.max)

    def finish_cond(state):
        return jnp.min(state[-1]) < 0.5

    def finish_body(state):
        m, n_hi, thr, n_thr, done = state

        def body(j, carry):
            cnt, nxt = carry
            s = sc_ref[j]
            below = s < m
            return (cnt + _fold_rows(jnp.where(below, 0.0, 1.0), jnp.add),
                    jnp.maximum(nxt, _fold_rows(jnp.where(below, s, ninf), jnp.maximum)))

        cnt, nxt = lax.fori_loop(0, nkb, body, (jnp.zeros((SUBLANES, tq), F32),
                                               jnp.full((SUBLANES, tq), ninf, F32)))
        c = col_reduce(cnt, jnp.sum)
        open_ = done < 0.5
        hit = jnp.logical_and(open_, c >= kf)
        miss = jnp.logical_and(open_, c < kf)
        thr = jnp.where(hit, m, thr)
        n_thr = jnp.where(hit, c, n_thr)
        n_hi = jnp.where(miss, c, n_hi)
        m = jnp.where(miss, col_reduce(nxt, jnp.max), m)
        return m, n_hi, thr, n_thr, jnp.where(hit, 1.0, done)

    lowest = float(jnp.finfo(F32).min)
    zero = jnp.zeros((1, tq), F32)
    _, n_hi, thr, n_thr, _ = lax.while_loop(
        finish_cond, finish_body,
        (max_below(hi), n_hi, jnp.full((1, tq), lowest, F32), zero, jnp.where(need, 0.0, 1.0)))

    n_tie = jnp.where(need, kf - n_hi, 0.0)
    surplus = jnp.max(jnp.where(jnp.logical_and(need, n_thr - n_hi > n_tie), 1.0, 0.0)) > 0.5
    tie_base_ref[...] = zero
    kr = lax.broadcasted_iota(jnp.int32, (ts, ts), 0)
    kc = lax.broadcasted_iota(jnp.int32, (ts, ts), 1)

    def mask_bias(j):
        def plain(_):
            return jnp.where(sc_ref[j] >= thr, 0.0, NEG_BIG)

        def with_ties(_):
            s = sc_ref[j]
            tie = s == thr
            prefix_ones = jnp.where(kc <= kr, 1.0, 0.0).astype(BF16)
            rank = _dot(prefix_ones, jnp.where(tie, 1.0, 0.0).astype(BF16)) + tie_base_ref[...]
            tie_base_ref[...] = rank[ts - 1:ts, :]
            keep = jnp.where(tie, jnp.where(rank <= n_tie, 0.0, NEG_BIG), NEG_BIG)
            return jnp.where(s > thr, 0.0, keep)

        return lax.cond(surplus, with_ties, plain, 0)

    for h in range(A_HEADS):
        qh = _dot(wuk_ref[h], qat_ref[h * A_HEAD_DIM:(h + 1) * A_HEAD_DIM, :])
        qabs_ref[h] = (qh * (A_HEAD_DIM ** -0.5 * LOG2E)).astype(BF16)
    m_ref[...] = jnp.full(m_ref.shape, NEG_BIG, F32)
    acc_ref[...] = jnp.zeros(acc_ref.shape, F32)
    heads = range(A_HEADS)

    def attend(js, bias_of_head):
        n = len(js) * ts
        cj = jnp.concatenate([c_ref[j] for j in js], axis=0)
        ctj = jnp.concatenate([jnp.concatenate([ct_ref[j] for j in js], axis=1),
                               jnp.ones((DENOM_ROWS, n), BF16)], axis=0)
        mb = jnp.concatenate([mask_bias(j) for j in js], axis=0)
        s = [_dot(cj, qabs_ref[h]) for h in heads]
        for h in heads:
            b = bias_of_head(h)
            s[h] = s[h] + (mb if b is None else mb + b)
        m_old = [m_ref[h] for h in heads]
        m_new = [jnp.maximum(m_old[h], col_reduce(_fold_rows(s[h], jnp.maximum), jnp.max))
                 for h in heads]
        alpha = [jnp.exp2(m_old[h] - m_new[h]) for h in heads]
        p = [jnp.exp2(s[h] - m_new[h]).astype(BF16) for h in heads]
        pv = [_dot(ctj, p[h]) for h in heads]
        for h in heads:
            acc_ref[h] = alpha[h] * acc_ref[h] + pv[h]
            m_ref[h] = m_new[h]

    n_far = jnp.maximum(i - 1, 0)

    def far_body(jj, carry):
        attend([2 * jj, 2 * jj + 1], lambda h: None)
        return carry

    lax.fori_loop(0, n_far // 2, far_body, 0)

    @pl.when(n_far % 2 == 1)
    def _():
        attend([n_far - 1], lambda h: None)

    @pl.when(i > 0)
    def _():
        attend([i - 1, i], lambda h: bias_ref[h])

    @pl.when(i == 0)
    def _():
        attend([i], lambda h: bias_ref[h, ts:, :])

    outs = [_dot(wuvt_ref[h],
                 (acc_ref[h, :A_LATENT, :] / acc_ref[h, A_LATENT:A_LATENT + 1, :]).astype(BF16))
            for h in range(A_HEADS)]
    o_ref[...] = jnp.concatenate(outs, axis=0).T.astype(o_ref.dtype)


def _t5_bucket(rel):
    nb = REL_BUCKETS // 2
    max_exact = nb // 2
    side = jnp.where(rel > 0, nb, 0)
    n = jnp.abs(rel)
    nf = jnp.maximum(n, 1).astype(jnp.float32)
    large = max_exact + (jnp.log(nf / max_exact) / math.log(REL_MAX_DIST / max_exact)
                         * (nb - max_exact)).astype(jnp.int32)
    large = jnp.minimum(large, nb - 1)
    return side + jnp.where(n < max_exact, n, large)


def _near_bias_kernel(far_ref, rb_ref, bucket_ref, o_ref):
    bucket = bucket_ref[...]
    for h in range(A_HEADS):
        acc = jnp.zeros(bucket.shape, F32)
        for b in range(REL_BUCKETS):
            acc = jnp.where(bucket == b, rb_ref[b, h], acc)
        o_ref[h] = (acc - rb_ref[far_ref[0], h]) * LOG2E


def _near_bias(rel_bias, tq, ts):
    assert ts >= REL_MAX_DIST
    rel = (jnp.arange(2 * ts)[:, None] - ts) - jnp.arange(tq)[None, :]
    far = _t5_bucket(jnp.full((1,), -(ts + 1), jnp.int32))
    smem = pl.BlockSpec(memory_space=pltpu.SMEM)
    return pl.pallas_call(
        _near_bias_kernel,
        in_specs=[smem, smem, pl.BlockSpec(memory_space=pltpu.VMEM)],
        out_specs=pl.BlockSpec(memory_space=pltpu.VMEM),
        out_shape=jax.ShapeDtypeStruct((A_HEADS, 2 * ts, tq), F32),
        name="near_bias",
    )(far, rel_bias.astype(F32), _t5_bucket(rel).astype(jnp.int32))


def _dsa(qat, qit, wit, kidx, c, ct3, w_uk, w_uv, rel_bias, batch, seq, tq, nbisect):
    ts = tq
    m = batch * seq
    nq = seq // tq
    nks = seq // ts
    topk = min(TOPK_MAX, seq // 4)
    kidx3 = kidx.reshape(m // ts, ts, IDX_DIM)
    c3 = c.reshape(m // ts, ts, A_LATENT)
    wuk = w_uk.astype(BF16)
    wuv_t = jnp.swapaxes(w_uv, 1, 2).astype(BF16)
    bias = _near_bias(rel_bias, tq, ts)
    qcol = lambda b, i: (0, b * nq + i)
    kv = lambda b, i: (b, 0, 0)
    const3 = lambda b, i: (0, 0, 0)
    kern = functools.partial(_dsa_kernel, tq=tq, ts=ts, topk=topk, nbisect=nbisect)
    return pl.pallas_call(
        kern,
        grid=(batch, nq),
        in_specs=[
            pl.BlockSpec((A_WIDTH, tq), qcol),
            pl.BlockSpec((IDX_HEADS * IDX_DIM, tq), qcol),
            pl.BlockSpec((IDX_HEADS, tq), qcol),
            pl.BlockSpec((nks, ts, IDX_DIM), kv),
            pl.BlockSpec((nks, ts, A_LATENT), kv),
            pl.BlockSpec((nks, A_LATENT, ts), kv),
            pl.BlockSpec((A_HEADS, A_LATENT, A_HEAD_DIM), const3),
            pl.BlockSpec((A_HEADS, A_HEAD_DIM, A_LATENT), const3),
            pl.BlockSpec((A_HEADS, 2 * ts, tq), const3),
        ],
        out_specs=pl.BlockSpec((tq, A_WIDTH), lambda b, i: (b * nq + i, 0)),
        out_shape=jax.ShapeDtypeStruct((m, A_WIDTH), BF16),
        scratch_shapes=[
            pltpu.VMEM((nks, ts, tq), F32),
            pltpu.VMEM((A_HEADS, A_LATENT, tq), BF16),
            pltpu.VMEM((A_HEADS, 1, tq), F32),
            pltpu.VMEM((A_HEADS, A_LATENT + DENOM_ROWS, tq), F32),
            pltpu.VMEM((1, tq), F32),
            pltpu.VMEM((2, SUBLANES, tq), F32),
        ],
        compiler_params=pltpu.CompilerParams(
            dimension_semantics=("arbitrary", "arbitrary"), vmem_limit_bytes=VMEM_LIMIT),
        name="dsa",
    )(qat, qit, wit, kidx3, c3, ct3, wuk, wuv_t, bias)


PAIR = 2 * CHUNK
HI = lax.Precision.HIGHEST


def _gdn_kernel(qkv_ref, z_ref, ab_ref, abt_ref, convw_ref, pcol_ref, prow_ref, onorm_ref, o_ref,
                u_s, w_s, qg_s, kdt_s, a_s, gl_s, st_s, *, seq):
    npair = seq // PAIR
    ri = lax.broadcasted_iota(jnp.int32, (PAIR, PAIR), 0)
    ci = lax.broadcasted_iota(jnp.int32, (PAIR, PAIR), 1)
    same = (ri // CHUNK) == (ci // CHUNK)
    lower_bd = jnp.where(jnp.logical_and(same, ci <= ri), 1.0, 0.0)
    upper_bd = jnp.where(jnp.logical_and(same, ri <= ci), 1.0, 0.0)
    causal_bd = jnp.logical_and(same, ci <= ri)
    diag = ci == ri
    eye = jnp.where(diag, 1.0, 0.0)
    neg_exp_a_col = pcol_ref[:, 0:1]
    dt_col = pcol_ref[:, 1:2]
    neg_exp_a_row = prow_ref[0:1, :]
    dt_row = prow_ref[1:2, :]
    convw = convw_ref[...]
    first_chunk = lax.broadcasted_iota(jnp.int32, (PAIR, 1), 0) < CHUNK

    def aligned(x, m):
        return x if isinstance(x, int) else pl.multiple_of(x, m)

    def conv_silu(p, col):
        r0 = aligned(p * PAIR, PAIR)
        cur = qkv_ref[pl.ds(r0, PAIR), col:col + B_HEAD_DIM].astype(F32)
        prev0 = max(r0 - 16, 0) if isinstance(r0, int) else pl.multiple_of(jnp.maximum(r0 - 16, 0), 16)
        prev = qkv_ref[pl.ds(prev0, 16), col:col + B_HEAD_DIM].astype(F32)
        prev = jnp.where(p > 0, prev, 0.0)
        ext = jnp.concatenate([prev, cur], axis=0)
        y = cur * convw[CONV_WIDTH - 1:CONV_WIDTH, col:col + B_HEAD_DIM]
        for back in range(1, CONV_WIDTH):
            sh = ext[16 - back:16 - back + PAIR]
            y = y + sh * convw[CONV_WIDTH - 1 - back:CONV_WIDTH - back, col:col + B_HEAD_DIM]
        return y * _sigmoid(y)

    def l2n(x):
        return x * lax.rsqrt(jnp.sum(x * x, axis=-1, keepdims=True) + EPS)

    def split(a):
        hi = a.astype(BF16)
        return hi, (a - hi.astype(F32)).astype(BF16)

    def dot3(a_parts, b_parts):
        (ah, al), (bh, bl) = a_parts, b_parts
        return (_dot(jnp.concatenate([ah, al], axis=1), jnp.concatenate([bh, bh], axis=0))
                + _dot(ah, bl))

    heads = range(B_HEADS)

    def phase1(p):
        r0 = aligned(p * PAIR, PAIR)
        rows = pl.ds(r0, PAIR)
        ab = ab_ref[rows, :]
        g_col = neg_exp_a_row * _softplus(ab + dt_row)
        gc_col = _dot(lower_bd, g_col, HI)
        beta_col = _sigmoid(ab)
        abt = abt_ref[p]
        g_row = neg_exp_a_col * _softplus(abt + dt_col)
        gc_row = _dot(g_row, upper_bd, HI)
        q = [l2n(conv_silu(p, h * B_HEAD_DIM)) * (B_HEAD_DIM ** -0.5) for h in heads]
        k = [l2n(conv_silu(p, B_WIDTH + h * B_HEAD_DIM)) for h in heads]
        v = [conv_silu(p, 2 * B_WIDTH + h * B_HEAD_DIM) for h in heads]
        gcol = [gc_col[:, h:h + 1] for h in heads]
        beta = [beta_col[:, B_HEADS + h:B_HEADS + h + 1] for h in heads]
        kb = [k[h] * beta[h] for h in heads]
        qk_kk = [_dot_nt(jnp.concatenate([q[h], kb[h]], axis=0).astype(BF16), k[h].astype(BF16))
                 for h in heads]
        decay = [jnp.exp(jnp.where(causal_bd, gcol[h] - gc_row[h:h + 1, :], -jnp.inf)) for h in heads]
        intra = [(qk_kk[h][:PAIR] * decay[h]).astype(BF16) for h in heads]
        low = [jnp.where(diag, 0.0, qk_kk[h][PAIR:] * decay[h]) for h in heads]

        x = [eye - low[h] for h in heads]
        lp = [split(low[h]) for h in heads]
        pw = [dot3(lp[h], lp[h]) for h in heads]
        for _ in range(int(math.log2(CHUNK)) - 2):
            xp = [split(x[h]) for h in heads]
            pp = [split(pw[h]) for h in heads]
            r = [dot3((jnp.concatenate([xp[h][0], pp[h][0]], axis=0),
                       jnp.concatenate([xp[h][1], pp[h][1]], axis=0)), pp[h]) for h in heads]
            x = [x[h] + r[h][:PAIR] for h in heads]
            pw = [r[h][PAIR:] for h in heads]
        tinv = [x[h] + dot3(split(x[h]), split(pw[h])) for h in heads]

        eg = [jnp.exp(gcol[h]) for h in heads]
        uw = [_dot(tinv[h].astype(BF16),
                   jnp.concatenate([v[h] * beta[h], kb[h] * eg[h]], axis=1).astype(BF16)) for h in heads]
        slot = p % 2
        for h in heads:
            g_last = jnp.where(first_chunk, gcol[h][CHUNK - 1:CHUNK], gcol[h][PAIR - 1:PAIR])
            kd = k[h] * jnp.exp(g_last - gcol[h])
            u_s[slot, h] = uw[h][:, :B_HEAD_DIM]
            w_s[slot, h] = uw[h][:, B_HEAD_DIM:].astype(BF16)
            qg_s[slot, h] = (q[h] * eg[h]).astype(BF16)
            for cc in range(2):
                rs = slice(cc * CHUNK, (cc + 1) * CHUNK)
                kdt_s[slot, h, cc] = kd[rs].T.astype(BF16)
                a_s[slot, h, cc] = intra[h][rs, rs]
                gl_s[slot, h, cc] = jnp.broadcast_to(
                    jnp.exp(gcol[h][(cc + 1) * CHUNK - 1:(cc + 1) * CHUNK]), (8, LANES))

    st_s[...] = jnp.zeros(st_s.shape, F32)
    onorm = onorm_ref[...]

    def phase2(p):
        slot = p % 2
        for cc in range(2):
            rs = slice(cc * CHUNK, (cc + 1) * CHUNK)
            rows = pl.ds(aligned(p * PAIR + cc * CHUNK, CHUNK), CHUNK)
            state = [st_s[h] for h in heads]
            sb = [state[h].astype(BF16) for h in heads]
            wq = [_dot(jnp.concatenate([w_s[slot, h, rs, :], qg_s[slot, h, rs, :]], axis=0), sb[h])
                  for h in heads]
            vb = [(u_s[slot, h, rs, :] - wq[h][:CHUNK]).astype(BF16) for h in heads]
            upd = [_dot(kdt_s[slot, h, cc], vb[h]) for h in heads]
            av = [_dot(a_s[slot, h, cc], vb[h]) for h in heads]
            for h in heads:
                st_s[h] = state[h] * gl_s[slot, h, cc][0:1, :] + upd[h]
            for h in heads:
                o = wq[h][CHUNK:] + av[h]
                z = z_ref[rows, h * B_HEAD_DIM:(h + 1) * B_HEAD_DIM].astype(F32)
                o_ref[rows, h * B_HEAD_DIM:(h + 1) * B_HEAD_DIM] = (
                    _rms(o, onorm) * (z * _sigmoid(z))).astype(o_ref.dtype)

    def body(p, carry):
        phase2(p - 1)
        phase1(p)
        return carry

    phase1(0)
    lax.fori_loop(1, npair, body, 0)
    phase2(npair - 1)


def _gdn(qkv, z, ab, abt, conv_w, a_log, dt_bias, o_norm, batch, seq):
    m = batch * seq
    npair = seq // PAIR
    abt3 = jnp.swapaxes(abt.reshape(2 * B_HEADS, m // PAIR, PAIR), 0, 1)
    zeros = jnp.zeros((B_HEADS,), F32)
    nea = jnp.concatenate([-jnp.exp(a_log.astype(F32)), zeros])
    dtb = jnp.concatenate([dt_bias.astype(F32), zeros])
    pcol = jnp.stack([nea, dtb], axis=1)
    prow = jnp.stack([nea, dtb], axis=0)
    const = lambda b: (0, 0)
    rows = lambda b: (b, 0)
    return pl.pallas_call(
        functools.partial(_gdn_kernel, seq=seq),
        grid=(batch,),
        in_specs=[
            pl.BlockSpec((seq, 3 * B_WIDTH), rows),
            pl.BlockSpec((seq, B_WIDTH), rows),
            pl.BlockSpec((seq, 2 * B_HEADS), rows),
            pl.BlockSpec((npair, 2 * B_HEADS, PAIR), lambda b: (b, 0, 0)),
            pl.BlockSpec((CONV_WIDTH, 3 * B_WIDTH), const),
            pl.BlockSpec((2 * B_HEADS, 2), const),
            pl.BlockSpec((2, 2 * B_HEADS), const),
            pl.BlockSpec((1, B_HEAD_DIM), const),
        ],
        out_specs=pl.BlockSpec((seq, B_WIDTH), rows),
        out_shape=jax.ShapeDtypeStruct((m, B_WIDTH), BF16),
        scratch_shapes=[
            pltpu.VMEM((2, B_HEADS, PAIR, B_HEAD_DIM), F32),
            pltpu.VMEM((2, B_HEADS, PAIR, B_HEAD_DIM), BF16),
            pltpu.VMEM((2, B_HEADS, PAIR, B_HEAD_DIM), BF16),
            pltpu.VMEM((2, B_HEADS, 2, B_HEAD_DIM, CHUNK), BF16),
            pltpu.VMEM((2, B_HEADS, 2, CHUNK, CHUNK), BF16),
            pltpu.VMEM((2, B_HEADS, 2, 8, LANES), F32),
            pltpu.VMEM((B_HEADS, B_HEAD_DIM, B_HEAD_DIM), F32),
        ],
        compiler_params=pltpu.CompilerParams(
            dimension_semantics=("arbitrary",), vmem_limit_bytes=VMEM_LIMIT),
        name="gdn",
    )(qkv, z, ab, abt3, conv_w.astype(F32), pcol, prow, o_norm.reshape(1, B_HEAD_DIM).astype(F32))


def _out_mlp_kernel(x_ref, oa_ref, ob_ref, wo_ref, g_post_mix_ref, g_pre_mlp_ref, w1_ref, w2_ref,
                    g_post_mlp_ref, o_ref, *, ff_chunk):
    mix = _dot(oa_ref[...], wo_ref[:A_WIDTH, :]) + _dot(ob_ref[...], wo_ref[A_WIDTH:, :])
    x1 = x_ref[...] + _rms(mix, g_post_mix_ref[...])
    h = _rms(x1, g_pre_mlp_ref[...]).astype(BF16)
    d_ff = w1_ref.shape[1]
    y = jnp.zeros(x1.shape, F32)
    for lo in range(0, d_ff, ff_chunk):
        a = jnp.maximum(_dot(h, w1_ref[:, lo:lo + ff_chunk]), 0.0)
        y = y + _dot((a * a).astype(BF16), w2_ref[lo:lo + ff_chunk, :])
    o_ref[...] = x1 + _rms(y, g_post_mlp_ref[...])


def _out_mlp(x2, oa, ob, w_out, g_post_mix, g_pre_mlp, w1, w2, g_post_mlp, tm, ff_chunk):
    m, d = x2.shape
    d_ff = w1.shape[1]
    const = lambda i: (0, 0)
    row = lambda i: (i, 0)
    once = dict(pipeline_mode=pl.Buffered(1))
    vec = lambda g: g.reshape(1, d).astype(F32)
    return pl.pallas_call(
        functools.partial(_out_mlp_kernel, ff_chunk=ff_chunk),
        grid=(m // tm,),
        in_specs=[
            pl.BlockSpec((tm, d), row),
            pl.BlockSpec((tm, A_WIDTH), row),
            pl.BlockSpec((tm, B_WIDTH), row),
            pl.BlockSpec((A_WIDTH + B_WIDTH, d), const, **once),
            pl.BlockSpec((1, d), const),
            pl.BlockSpec((1, d), const),
            pl.BlockSpec((d, d_ff), const, **once),
            pl.BlockSpec((d_ff, d), const, **once),
            pl.BlockSpec((1, d), const),
        ],
        out_specs=pl.BlockSpec((tm, d), row),
        out_shape=jax.ShapeDtypeStruct((m, d), F32),
        compiler_params=pltpu.CompilerParams(
            dimension_semantics=("arbitrary",), vmem_limit_bytes=VMEM_LIMIT),
        name="out_mlp",
    )(x2, oa, ob, w_out.astype(BF16), vec(g_post_mix), vec(g_pre_mlp),
      w1.astype(BF16), w2.astype(BF16), vec(g_post_mlp))


def kernel(x, w_in, c_norm, w_uk, w_uv, rel_bias, conv_w, a_log, dt_bias, o_norm, w_out,
           pre_norm_mix, post_norm_mix, pre_norm_mlp, post_norm_mlp, w_mlp_in, w_mlp_out):
    batch, seq, d = x.shape
    m = batch * seq
    tm = min(512, m)
    tq = min(256, seq)
    xs = x.reshape(m, d)
    for l in range(w_in.shape[0]):
        qat, qit, qkv, z, c, ct3, kidx, wit, ab, abt = _in_proj(
            xs, pre_norm_mix[l], w_in[l], c_norm[l], tm, tq)
        oa = _dsa(qat, qit, wit, kidx, c, ct3, w_uk[l], w_uv[l], rel_bias, batch, seq, tq, nbisect=16)
        ob = _gdn(qkv, z, ab, abt, conv_w[l], a_log[l], dt_bias[l], o_norm[l], batch, seq)
        xs = _out_mlp(xs, oa, ob, w_out[l], post_norm_mix[l], pre_norm_mlp[l],
                      w_mlp_in[l], w_mlp_out[l], post_norm_mlp[l], tm, ff_chunk=1024)
    return xs.reshape(batch, seq, d)
```

```python
import functools
import math

import numpy as np
import jax
import jax.numpy as jnp
from jax import lax
from jax.experimental import pallas as pl
from jax.experimental.pallas import tpu as pltpu

F32 = jnp.float32
BF16 = jnp.bfloat16

EPS = 1e-6
CHUNK = 64
A_HEADS = 8
A_HEAD_DIM = 64
A_WIDTH = A_HEADS * A_HEAD_DIM
A_LATENT = 128
IDX_HEADS = 8
IDX_DIM = 64
TOPK_MAX = 256
REL_BUCKETS = 32
REL_MAX_DIST = 128
B_HEADS = 4
B_HEAD_DIM = 128
B_WIDTH = B_HEADS * B_HEAD_DIM
CONV_WIDTH = 4
IN_SPLITS = (A_WIDTH, A_LATENT, IDX_HEADS * IDX_DIM, IDX_DIM, IDX_HEADS,
             B_WIDTH, B_WIDTH, B_WIDTH, B_HEADS, B_HEADS, B_WIDTH)

LANES = 128
SUBLANES = 8
VMEM_LIMIT = 56 * 1024 * 1024
NEG_BIG = -1e30
LOG2E = 1.0 / math.log(2.0)
DENOM_ROWS = 16

NT_DIMS = (((1,), (1,)), ((), ()))


def _dot(a, b, precision=None):
    return jnp.dot(a, b, preferred_element_type=F32, precision=precision)


def _dot_nt(a, b, precision=None):
    return lax.dot_general(a, b, NT_DIMS, preferred_element_type=F32, precision=precision)


def _rms(x, g):
    return x * lax.rsqrt(jnp.mean(x * x, axis=-1, keepdims=True) + EPS) * g


def _sigmoid(x):
    return 1.0 / (1.0 + jnp.exp(-x))


def _silu(x):
    half = 0.5 * x
    return half + half * jnp.tanh(half)


def _softplus(x):
    return jnp.maximum(x, 0.0) + jnp.log1p(jnp.exp(-jnp.abs(x)))


_C_QKV = 0
_C_Z = _C_QKV + 3 * B_WIDTH
_C_CKV = _C_Z + B_WIDTH
_C_SMALL = _C_CKV + A_LATENT
_C_END = _C_SMALL + LANES
_R_QA = 0
_R_QI = _R_QA + A_WIDTH
_R_CKV = _R_QI + IDX_HEADS * IDX_DIM
_R_SMALL = _R_CKV + A_LATENT
_R_END = _R_SMALL + IDX_HEADS + 2 * B_HEADS


def _in_proj_kernel(x_ref, g_ref, w_ref, wt_ref, cn_ref, cnt_ref,
                    qat_ref, qit_ref, qkv_ref, z_ref, c_ref, ct_ref, kidx_ref, wit_ref, ab_ref, abt_ref,
                    *, ts):
    h = _rms(x_ref[...], g_ref[...]).astype(BF16)

    def proj(lo, hi):
        return _dot(h, w_ref[:, lo:hi])

    def proj_t(lo, hi):
        return _dot_nt(wt_ref[lo:hi, :], h)

    qkv_ref[...] = proj(_C_QKV, _C_Z).astype(BF16)
    z_ref[...] = proj(_C_Z, _C_CKV).astype(BF16)
    c_ref[...] = _rms(proj(_C_CKV, _C_SMALL), cn_ref[...]).astype(BF16)
    small = proj(_C_SMALL, _C_END)
    kidx_ref[...] = small[:, :IDX_DIM].astype(BF16)
    ab_ref[...] = small[:, IDX_DIM:IDX_DIM + 2 * B_HEADS]

    qat_ref[...] = proj_t(_R_QA, _R_QI).astype(BF16)
    qit_ref[...] = (proj_t(_R_QI, _R_CKV) * (IDX_DIM ** -0.5)).astype(BF16)
    ckv_t = proj_t(_R_CKV, _R_SMALL)
    c_t = (ckv_t * lax.rsqrt(jnp.mean(ckv_t * ckv_t, axis=0, keepdims=True) + EPS)
           * cnt_ref[...]).astype(BF16)
    for r in range(ct_ref.shape[0]):
        ct_ref[r] = c_t[:, r * ts:(r + 1) * ts]
    small_t = proj_t(_R_SMALL, _R_END)
    wit_ref[...] = small_t[:IDX_HEADS] * (IDX_HEADS ** -0.5)
    abt_ref[...] = small_t[IDX_HEADS:]


def _in_proj(x2, g, w_in, c_norm, tm, ts):
    m, d = x2.shape
    offs = np.cumsum((0,) + IN_SPLITS)
    seg = [w_in[:, offs[i]:offs[i + 1]] for i in range(len(IN_SPLITS))]
    q_a, c_kv, q_idx, k_idx, w_idx, q_b, k_b, v_b, a_b, b_b, z_b = seg
    pad = jnp.zeros((d, LANES - IDX_DIM - 2 * B_HEADS), w_in.dtype)
    w_all = jnp.concatenate([q_b, k_b, v_b, z_b, c_kv, k_idx, a_b, b_b, pad], axis=1).astype(BF16)
    w_t = jnp.concatenate([q_a, q_idx, c_kv, w_idx, a_b, b_b], axis=1).T.astype(BF16)
    const = lambda i: (0, 0)
    row = lambda i: (i, 0)
    col = lambda i: (0, i)
    out_shape = (
        jax.ShapeDtypeStruct((A_WIDTH, m), BF16),
        jax.ShapeDtypeStruct((IDX_HEADS * IDX_DIM, m), BF16),
        jax.ShapeDtypeStruct((m, 3 * B_WIDTH), BF16),
        jax.ShapeDtypeStruct((m, B_WIDTH), BF16),
        jax.ShapeDtypeStruct((m, A_LATENT), BF16),
        jax.ShapeDtypeStruct((m // ts, A_LATENT, ts), BF16),
        jax.ShapeDtypeStruct((m, IDX_DIM), BF16),
        jax.ShapeDtypeStruct((IDX_HEADS, m), F32),
        jax.ShapeDtypeStruct((m, 2 * B_HEADS), F32),
        jax.ShapeDtypeStruct((2 * B_HEADS, m), F32),
    )
    out_specs = (
        pl.BlockSpec((A_WIDTH, tm), col),
        pl.BlockSpec((IDX_HEADS * IDX_DIM, tm), col),
        pl.BlockSpec((tm, 3 * B_WIDTH), row),
        pl.BlockSpec((tm, B_WIDTH), row),
        pl.BlockSpec((tm, A_LATENT), row),
        pl.BlockSpec((tm // ts, A_LATENT, ts), lambda i: (i, 0, 0)),
        pl.BlockSpec((tm, IDX_DIM), row),
        pl.BlockSpec((IDX_HEADS, tm), col),
        pl.BlockSpec((tm, 2 * B_HEADS), row),
        pl.BlockSpec((2 * B_HEADS, tm), col),
    )
    return pl.pallas_call(
        functools.partial(_in_proj_kernel, ts=ts),
        grid=(m // tm,),
        in_specs=[
            pl.BlockSpec((tm, d), row),
            pl.BlockSpec((1, d), const),
            pl.BlockSpec((d, _C_END), const),
            pl.BlockSpec((_R_END, d), const),
            pl.BlockSpec((1, A_LATENT), const),
            pl.BlockSpec((A_LATENT, 1), const),
        ],
        out_specs=out_specs,
        out_shape=out_shape,
        compiler_params=pltpu.CompilerParams(
            dimension_semantics=("arbitrary",), vmem_limit_bytes=VMEM_LIMIT),
        name="in_proj",
    )(x2, g.reshape(1, d), w_all, w_t, c_norm.reshape(1, A_LATENT), c_norm.reshape(A_LATENT, 1))


def _fold_rows(x, op):
    r = x.shape[0]
    while r > SUBLANES:
        r //= 2
        x = op(x[:r], x[r:])
    return x


def _dsa_kernel(qat_ref, qit_ref, wit_ref, kidx_ref, c_ref, ct_ref, wuk_ref, wuvt_ref, bias_ref, o_ref,
                sc_ref, qabs_ref, m_ref, acc_ref, tie_base_ref, range_ref, *, tq, ts, topk, nbisect):
    i = pl.program_id(1)
    nkb = i + 1
    q_pos = i * tq + lax.broadcasted_iota(jnp.int32, (1, tq), 1)
    limit = (q_pos // CHUNK + 1) * CHUNK
    kf = float(topk)
    ninf = -jnp.inf

    def key_ids(j):
        return j * ts + lax.broadcasted_iota(jnp.int32, (ts, 1), 0)

    def col_reduce(x8, op):
        return op(x8, axis=0, keepdims=True)

    wit = wit_ref[...]

    range_ref[0] = jnp.full((SUBLANES, tq), jnp.inf, F32)
    range_ref[1] = jnp.full((SUBLANES, tq), ninf, F32)

    def score_blocks(js, diagonal):
        k = jnp.concatenate([kidx_ref[j] for j in js], axis=0)
        r = [_dot(k, qit_ref[h * IDX_DIM:(h + 1) * IDX_DIM, :]) for h in range(IDX_HEADS)]
        s = jnp.maximum(r[0], 0.0) * wit[0:1, :]
        for h in range(1, IDX_HEADS):
            s = s + jnp.maximum(r[h], 0.0) * wit[h:h + 1, :]
        lo, hi = s, s
        if diagonal:
            adm = key_ids(js[0]) < limit
            s = jnp.where(adm, s, ninf)
            lo, hi = jnp.where(adm, s, jnp.inf), s
        for n, j in enumerate(js):
            sc_ref[j] = s[n * ts:(n + 1) * ts]
        range_ref[0] = jnp.minimum(range_ref[0], _fold_rows(lo, jnp.minimum))
        range_ref[1] = jnp.maximum(range_ref[1], _fold_rows(hi, jnp.maximum))

    def score_pair(jj, carry):
        score_blocks([2 * jj, 2 * jj + 1], False)
        return carry

    lax.fori_loop(0, i // 2, score_pair, 0)

    @pl.when(i % 2 == 1)
    def _():
        score_blocks([i - 1], False)

    score_blocks([i], True)
    rmin = col_reduce(range_ref[0], jnp.min)
    rmax = col_reduce(range_ref[1], jnp.max)

    def count(pred):
        def body(j, acc):
            return acc + _fold_rows(jnp.where(pred(sc_ref[j], j), 1.0, 0.0), jnp.add)
        acc = lax.fori_loop(0, nkb, body, jnp.zeros((SUBLANES, tq), F32))
        return col_reduce(acc, jnp.sum)

    def count_ge(v):
        return count(lambda s, j: s >= v)

    need = limit > topk

    def bisect_body(_, carry):
        lo, hi, n_hi = carry
        mid = lo + 0.5 * (hi - lo)
        c = count_ge(mid)
        ge = c >= kf
        return jnp.where(ge, mid, lo), jnp.where(ge, hi, mid), jnp.where(ge, n_hi, c)

    hi0 = rmax + jnp.abs(rmax) * 1e-6 + 1e-30
    _, hi, n_hi = lax.fori_loop(0, nbisect, bisect_body, (rmin, hi0, jnp.zeros((1, tq), F32)))

    def max_below(v):
        def body(j, acc):
            s = sc_ref[j]
            return jnp.maximum(acc, _fold_rows(jnp.where(s < v, s, ninf), jnp.maximum))
        return col_reduce(lax.fori_loop(0, nkb, body, jnp.full((SUBLANES, tq), ninf, F32)), jnp.max)

    def finish_cond(state):
        return jnp.min(state[-1]) < 0.5

    def finish_body(state):
        m, n_hi, thr, n_thr, done = state

        def body(j, carry):
            cnt, nxt = carry
            s = sc_ref[j]
            below = s < m
            return (cnt + _fold_rows(jnp.where(below, 0.0, 1.0), jnp.add),
                    jnp.maximum(nxt, _fold_rows(jnp.where(below, s, ninf), jnp.maximum)))

        cnt, nxt = lax.fori_loop(0, nkb, body, (jnp.zeros((SUBLANES, tq), F32),
                                               jnp.full((SUBLANES, tq), ninf, F32)))
        c = col_reduce(cnt, jnp.sum)
        open_ = done < 0.5
        hit = jnp.logical_and(open_, c >= kf)
        miss = jnp.logical_and(open_, c < kf)
        thr = jnp.where(hit, m, thr)
        n_thr = jnp.where(hit, c, n_thr)
        n_hi = jnp.where(miss, c, n_hi)
        m = jnp.where(miss, col_reduce(nxt, jnp.max), m)
        return m, n_hi, thr, n_thr, jnp.where(hit, 1.0, done)

    lowest = float(jnp.finfo(F32).min)
    zero = jnp.zeros((1, tq), F32)
    _, n_hi, thr, n_thr, _ = lax.while_loop(
        finish_cond, finish_body,
        (max_below(hi), n_hi, jnp.full((1, tq), lowest, F32), zero, jnp.where(need, 0.0, 1.0)))

    n_tie = jnp.where(need, kf - n_hi, 0.0)
    surplus = jnp.max(jnp.where(jnp.logical_and(need, n_thr - n_hi > n_tie), 1.0, 0.0)) > 0.5
    tie_base_ref[...] = zero
    kr = lax.broadcasted_iota(jnp.int32, (ts, ts), 0)
    kc = lax.broadcasted_iota(jnp.int32, (ts, ts), 1)

    def mask_bias(j):
        def plain(_):
            return jnp.where(sc_ref[j] >= thr, 0.0, NEG_BIG)

        def with_ties(_):
            s = sc_ref[j]
            tie = s == thr
            prefix_ones = jnp.where(kc <= kr, 1.0, 0.0).astype(BF16)
            rank = _dot(prefix_ones, jnp.where(tie, 1.0, 0.0).astype(BF16)) + tie_base_ref[...]
            tie_base_ref[...] = rank[ts - 1:ts, :]
            keep = jnp.where(tie, jnp.where(rank <= n_tie, 0.0, NEG_BIG), NEG_BIG)
            return jnp.where(s > thr, 0.0, keep)

        return lax.cond(surplus, with_ties, plain, 0)

    for h in range(A_HEADS):
        qh = _dot(wuk_ref[h], qat_ref[h * A_HEAD_DIM:(h + 1) * A_HEAD_DIM, :])
        qabs_ref[h] = (qh * (A_HEAD_DIM ** -0.5 * LOG2E)).astype(BF16)
    m_ref[...] = jnp.full(m_ref.shape, NEG_BIG, F32)
    acc_ref[...] = jnp.zeros(acc_ref.shape, F32)
    heads = range(A_HEADS)

    def attend(js, bias_of_head):
        n = len(js) * ts
        cj = jnp.concatenate([c_ref[j] for j in js], axis=0)
        ctj = jnp.concatenate([jnp.concatenate([ct_ref[j] for j in js], axis=1),
                               jnp.ones((DENOM_ROWS, n), BF16)], axis=0)
        mb = jnp.concatenate([mask_bias(j) for j in js], axis=0)
        s = [_dot(cj, qabs_ref[h]) for h in heads]
        for h in heads:
            b = bias_of_head(h)
            s[h] = s[h] + (mb if b is None else mb + b)
        m_old = [m_ref[h] for h in heads]
        m_new = [jnp.maximum(m_old[h], col_reduce(_fold_rows(s[h], jnp.maximum), jnp.max))
                 for h in heads]
        alpha = [jnp.exp2(m_old[h] - m_new[h]) for h in heads]
        p = [jnp.exp2(s[h] - m_new[h]).astype(BF16) for h in heads]
        pv = [_dot(ctj, p[h]) for h in heads]
        for h in heads:
            acc_ref[h] = alpha[h] * acc_ref[h] + pv[h]
            m_ref[h] = m_new[h]

    n_far = jnp.maximum(i - 1, 0)

    def far_body(jj, carry):
        attend([2 * jj, 2 * jj + 1], lambda h: None)
        return carry

    lax.fori_loop(0, n_far // 2, far_body, 0)

    @pl.when(n_far % 2 == 1)
    def _():
        attend([n_far - 1], lambda h: None)

    @pl.when(i > 0)
    def _():
        attend([i - 1, i], lambda h: bias_ref[h])

    @pl.when(i == 0)
    def _():
        attend([i], lambda h: bias_ref[h, ts:, :])

    outs = [_dot(wuvt_ref[h],
                 (acc_ref[h, :A_LATENT, :] / acc_ref[h, A_LATENT:A_LATENT + 1, :]).astype(BF16))
            for h in range(A_HEADS)]
    o_ref[...] = jnp.concatenate(outs, axis=0).T.astype(o_ref.dtype)


def _t5_bucket(rel):
    nb = REL_BUCKETS // 2
    max_exact = nb // 2
    side = jnp.where(rel > 0, nb, 0)
    n = jnp.abs(rel)
    nf = jnp.maximum(n, 1).astype(jnp.float32)
    large = max_exact + (jnp.log(nf / max_exact) / math.log(REL_MAX_DIST / max_exact)
                         * (nb - max_exact)).astype(jnp.int32)
    large = jnp.minimum(large, nb - 1)
    return side + jnp.where(n < max_exact, n, large)


def _near_bias_kernel(far_ref, rb_ref, bucket_ref, o_ref):
    bucket = bucket_ref[...]
    for h in range(A_HEADS):
        acc = jnp.zeros(bucket.shape, F32)
        for b in range(REL_BUCKETS):
            acc = jnp.where(bucket == b, rb_ref[b, h], acc)
        o_ref[h] = (acc - rb_ref[far_ref[0], h]) * LOG2E


def _near_bias(rel_bias, tq, ts):
    assert ts >= REL_MAX_DIST
    rel = (jnp.arange(2 * ts)[:, None] - ts) - jnp.arange(tq)[None, :]
    far = _t5_bucket(jnp.full((1,), -(ts + 1), jnp.int32))
    smem = pl.BlockSpec(memory_space=pltpu.SMEM)
    return pl.pallas_call(
        _near_bias_kernel,
        in_specs=[smem, smem, pl.BlockSpec(memory_space=pltpu.VMEM)],
        out_specs=pl.BlockSpec(memory_space=pltpu.VMEM),
        out_shape=jax.ShapeDtypeStruct((A_HEADS, 2 * ts, tq), F32),
        name="near_bias",
    )(far, rel_bias.astype(F32), _t5_bucket(rel).astype(jnp.int32))


def _dsa(qat, qit, wit, kidx, c, ct3, w_uk, w_uv, rel_bias, batch, seq, tq, nbisect):
    ts = tq
    m = batch * seq
    nq = seq // tq
    nks = seq // ts
    topk = min(TOPK_MAX, seq // 4)
    kidx3 = kidx.reshape(m // ts, ts, IDX_DIM)
    c3 = c.reshape(m // ts, ts, A_LATENT)
    wuk = w_uk.astype(BF16)
    wuv_t = jnp.swapaxes(w_uv, 1, 2).astype(BF16)
    bias = _near_bias(rel_bias, tq, ts)
    qcol = lambda b, i: (0, b * nq + i)
    kv = lambda b, i: (b, 0, 0)
    const3 = lambda b, i: (0, 0, 0)
    kern = functools.partial(_dsa_kernel, tq=tq, ts=ts, topk=topk, nbisect=nbisect)
    return pl.pallas_call(
        kern,
        grid=(batch, nq),
        in_specs=[
            pl.BlockSpec((A_WIDTH, tq), qcol),
            pl.BlockSpec((IDX_HEADS * IDX_DIM, tq), qcol),
            pl.BlockSpec((IDX_HEADS, tq), qcol),
            pl.BlockSpec((nks, ts, IDX_DIM), kv),
            pl.BlockSpec((nks, ts, A_LATENT), kv),
            pl.BlockSpec((nks, A_LATENT, ts), kv),
            pl.BlockSpec((A_HEADS, A_LATENT, A_HEAD_DIM), const3),
            pl.BlockSpec((A_HEADS, A_HEAD_DIM, A_LATENT), const3),
            pl.BlockSpec((A_HEADS, 2 * ts, tq), const3),
        ],
        out_specs=pl.BlockSpec((tq, A_WIDTH), lambda b, i: (b * nq + i, 0)),
        out_shape=jax.ShapeDtypeStruct((m, A_WIDTH), BF16),
        scratch_shapes=[
            pltpu.VMEM((nks, ts, tq), F32),
            pltpu.VMEM((A_HEADS, A_LATENT, tq), BF16),
            pltpu.VMEM((A_HEADS, 1, tq), F32),
            pltpu.VMEM((A_HEADS, A_LATENT + DENOM_ROWS, tq), F32),
            pltpu.VMEM((1, tq), F32),
            pltpu.VMEM((2, SUBLANES, tq), F32),
        ],
        compiler_params=pltpu.CompilerParams(
            dimension_semantics=("arbitrary", "arbitrary"), vmem_limit_bytes=VMEM_LIMIT),
        name="dsa",
    )(qat, qit, wit, kidx3, c3, ct3, wuk, wuv_t, bias)


PAIR = 2 * CHUNK
PAIRS_PER_STEP = 2
HI = lax.Precision.HIGHEST


def _gdn_kernel(qkv_ref, z_ref, ab_ref, abt_ref, convw_ref, pcol_ref, prow_ref, onorm_ref, o_ref,
                u_s, w_s, qg_s, kdt_s, a_s, gl_s, st_s, *, seq):
    npair = seq // PAIR
    ri = lax.broadcasted_iota(jnp.int32, (PAIR, PAIR), 0)
    ci = lax.broadcasted_iota(jnp.int32, (PAIR, PAIR), 1)
    same = (ri // CHUNK) == (ci // CHUNK)
    lower_bd = jnp.where(jnp.logical_and(same, ci <= ri), 1.0, 0.0)
    upper_bd = jnp.where(jnp.logical_and(same, ri <= ci), 1.0, 0.0)
    causal_bd = jnp.logical_and(same, ci <= ri)
    diag = ci == ri
    eye = jnp.where(diag, 1.0, 0.0)
    neg_exp_a_col = pcol_ref[:, 0:1]
    dt_col = pcol_ref[:, 1:2]
    neg_exp_a_row = prow_ref[0:1, :]
    dt_row = prow_ref[1:2, :]
    convw = convw_ref[...]
    first_chunk = lax.broadcasted_iota(jnp.int32, (PAIR, 1), 0) < CHUNK

    def aligned(x, m):
        return x if isinstance(x, int) else pl.multiple_of(x, m)

    def conv_silu(p, col):
        r0 = aligned(p * PAIR, PAIR)
        cur = qkv_ref[pl.ds(r0, PAIR), col:col + B_HEAD_DIM].astype(F32)
        prev0 = max(r0 - 16, 0) if isinstance(r0, int) else pl.multiple_of(jnp.maximum(r0 - 16, 0), 16)
        prev = qkv_ref[pl.ds(prev0, 16), col:col + B_HEAD_DIM].astype(F32)
        prev = jnp.where(p > 0, prev, 0.0)
        ext = jnp.concatenate([prev, cur], axis=0)
        y = cur * convw[CONV_WIDTH - 1:CONV_WIDTH, col:col + B_HEAD_DIM]
        for back in range(1, CONV_WIDTH):
            sh = ext[16 - back:16 - back + PAIR]
            y = y + sh * convw[CONV_WIDTH - 1 - back:CONV_WIDTH - back, col:col + B_HEAD_DIM]
        return _silu(y)

    def l2n(x):
        return x * lax.rsqrt(jnp.sum(x * x, axis=-1, keepdims=True) + EPS)

    heads = range(B_HEADS)
    pairs = range(PAIRS_PER_STEP)
    units = [(i, h) for i in pairs for h in heads]

    def slot_of(step, i):
        return (step % 2) * PAIRS_PER_STEP + i

    def phase1(step):
        ps = [step * PAIRS_PER_STEP + i for i in pairs]
        gc_col, gc_row, beta_col = [], [], []
        for p in ps:
            ab = ab_ref[pl.ds(aligned(p * PAIR, PAIR), PAIR), :]
            g_col = neg_exp_a_row * _softplus(ab + dt_row)
            gc_col.append(_dot(lower_bd, g_col, HI))
            beta_col.append(_sigmoid(ab))
            g_row = neg_exp_a_col * _softplus(abt_ref[p] + dt_col)
            gc_row.append(_dot(g_row, upper_bd, HI))
        q = {(i, h): l2n(conv_silu(ps[i], h * B_HEAD_DIM)) * (B_HEAD_DIM ** -0.5) for i, h in units}
        k = {(i, h): l2n(conv_silu(ps[i], B_WIDTH + h * B_HEAD_DIM)) for i, h in units}
        v = {(i, h): conv_silu(ps[i], 2 * B_WIDTH + h * B_HEAD_DIM) for i, h in units}
        gcol = {(i, h): gc_col[i][:, h:h + 1] for i, h in units}
        beta = {(i, h): beta_col[i][:, B_HEADS + h:B_HEADS + h + 1] for i, h in units}
        kb = {u: k[u] * beta[u] for u in units}
        qk_kk = {u: _dot_nt(jnp.concatenate([q[u], kb[u]], axis=0).astype(BF16), k[u].astype(BF16))
                 for u in units}
        decay = {(i, h): jnp.exp(jnp.where(causal_bd, gcol[i, h] - gc_row[i][h:h + 1, :], -jnp.inf))
                 for i, h in units}
        intra = {u: (qk_kk[u][:PAIR] * decay[u]).astype(BF16) for u in units}
        low = {u: jnp.where(diag, 0.0, qk_kk[u][PAIR:] * decay[u]) for u in units}

        x = {u: eye - low[u] for u in units}
        low16 = {u: low[u].astype(BF16) for u in units}
        pw = {u: _dot(low16[u], low16[u]) for u in units}
        for _ in range(int(math.log2(CHUNK)) - 2):
            pw16 = {u: pw[u].astype(BF16) for u in units}
            r = {u: _dot(jnp.concatenate([x[u].astype(BF16), pw16[u]], axis=0), pw16[u]) for u in units}
            x = {u: x[u] + r[u][:PAIR] for u in units}
            pw = {u: r[u][PAIR:] for u in units}
        tinv = {u: x[u] + _dot(x[u].astype(BF16), pw[u].astype(BF16)) for u in units}

        eg = {u: jnp.exp(gcol[u]) for u in units}
        uw = {u: _dot(tinv[u].astype(BF16),
                      jnp.concatenate([v[u] * beta[u], kb[u] * eg[u]], axis=1).astype(BF16))
              for u in units}
        for i, h in units:
            u, slot = (i, h), slot_of(step, i)
            g_last = jnp.where(first_chunk, gcol[u][CHUNK - 1:CHUNK], gcol[u][PAIR - 1:PAIR])
            kd = k[u] * jnp.exp(g_last - gcol[u])
            u_s[slot, h] = uw[u][:, :B_HEAD_DIM]
            w_s[slot, h] = uw[u][:, B_HEAD_DIM:].astype(BF16)
            qg_s[slot, h] = (q[u] * eg[u]).astype(BF16)
            for cc in range(2):
                rs = slice(cc * CHUNK, (cc + 1) * CHUNK)
                kdt_s[slot, h, cc] = kd[rs].T.astype(BF16)
                a_s[slot, h, cc] = intra[u][rs, rs]
                gl_s[slot, h, cc] = jnp.broadcast_to(
                    jnp.exp(gcol[u][(cc + 1) * CHUNK - 1:(cc + 1) * CHUNK]), (8, LANES))

    st_s[...] = jnp.zeros(st_s.shape, F32)
    onorm = onorm_ref[...]

    def phase2(step):
        for i in pairs:
            p, slot = step * PAIRS_PER_STEP + i, slot_of(step, i)
            for cc in range(2):
                rs = slice(cc * CHUNK, (cc + 1) * CHUNK)
                rows = pl.ds(aligned(p * PAIR + cc * CHUNK, CHUNK), CHUNK)
                state = [st_s[h] for h in heads]
                sb = [state[h].astype(BF16) for h in heads]
                wq = [_dot(jnp.concatenate([w_s[slot, h, rs, :], qg_s[slot, h, rs, :]], axis=0), sb[h])
                      for h in heads]
                vb = [(u_s[slot, h, rs, :] - wq[h][:CHUNK]).astype(BF16) for h in heads]
                upd = [_dot(kdt_s[slot, h, cc], vb[h]) for h in heads]
                av = [_dot(a_s[slot, h, cc], vb[h]) for h in heads]
                for h in heads:
                    st_s[h] = state[h] * gl_s[slot, h, cc][0:1, :] + upd[h]
                for h in heads:
                    o = wq[h][CHUNK:] + av[h]
                    z = z_ref[rows, h * B_HEAD_DIM:(h + 1) * B_HEAD_DIM].astype(F32)
                    o_ref[rows, h * B_HEAD_DIM:(h + 1) * B_HEAD_DIM] = (
                        _rms(o, onorm) * _silu(z)).astype(o_ref.dtype)

    def body(step, carry):
        phase2(step - 1)
        phase1(step)
        return carry

    nstep = npair // PAIRS_PER_STEP
    phase1(0)
    lax.fori_loop(1, nstep, body, 0)
    phase2(nstep - 1)


def _gdn(qkv, z, ab, abt, conv_w, a_log, dt_bias, o_norm, batch, seq):
    m = batch * seq
    npair = seq // PAIR
    assert npair % PAIRS_PER_STEP == 0
    nslot = 2 * PAIRS_PER_STEP
    abt3 = jnp.swapaxes(abt.reshape(2 * B_HEADS, m // PAIR, PAIR), 0, 1)
    zeros = jnp.zeros((B_HEADS,), F32)
    nea = jnp.concatenate([-jnp.exp(a_log.astype(F32)), zeros])
    dtb = jnp.concatenate([dt_bias.astype(F32), zeros])
    pcol = jnp.stack([nea, dtb], axis=1)
    prow = jnp.stack([nea, dtb], axis=0)
    const = lambda b: (0, 0)
    rows = lambda b: (b, 0)
    return pl.pallas_call(
        functools.partial(_gdn_kernel, seq=seq),
        grid=(batch,),
        in_specs=[
            pl.BlockSpec((seq, 3 * B_WIDTH), rows),
            pl.BlockSpec((seq, B_WIDTH), rows),
            pl.BlockSpec((seq, 2 * B_HEADS), rows),
            pl.BlockSpec((npair, 2 * B_HEADS, PAIR), lambda b: (b, 0, 0)),
            pl.BlockSpec((CONV_WIDTH, 3 * B_WIDTH), const),
            pl.BlockSpec((2 * B_HEADS, 2), const),
            pl.BlockSpec((2, 2 * B_HEADS), const),
            pl.BlockSpec((1, B_HEAD_DIM), const),
        ],
        out_specs=pl.BlockSpec((seq, B_WIDTH), rows),
        out_shape=jax.ShapeDtypeStruct((m, B_WIDTH), BF16),
        scratch_shapes=[
            pltpu.VMEM((nslot, B_HEADS, PAIR, B_HEAD_DIM), F32),
            pltpu.VMEM((nslot, B_HEADS, PAIR, B_HEAD_DIM), BF16),
            pltpu.VMEM((nslot, B_HEADS, PAIR, B_HEAD_DIM), BF16),
            pltpu.VMEM((nslot, B_HEADS, 2, B_HEAD_DIM, CHUNK), BF16),
            pltpu.VMEM((nslot, B_HEADS, 2, CHUNK, CHUNK), BF16),
            pltpu.VMEM((nslot, B_HEADS, 2, 8, LANES), F32),
            pltpu.VMEM((B_HEADS, B_HEAD_DIM, B_HEAD_DIM), F32),
        ],
        compiler_params=pltpu.CompilerParams(
            dimension_semantics=("arbitrary",), vmem_limit_bytes=VMEM_LIMIT),
        name="gdn",
    )(qkv, z, ab, abt3, conv_w.astype(F32), pcol, prow, o_norm.reshape(1, B_HEAD_DIM).astype(F32))


def _out_mlp_kernel(x_ref, oa_ref, ob_ref, wo_ref, g_post_mix_ref, g_pre_mlp_ref, w1_ref, w2_ref,
                    g_post_mlp_ref, o_ref, *, ff_chunk):
    mix = _dot(oa_ref[...], wo_ref[:A_WIDTH, :]) + _dot(ob_ref[...], wo_ref[A_WIDTH:, :])
    x1 = x_ref[...] + _rms(mix, g_post_mix_ref[...])
    h = _rms(x1, g_pre_mlp_ref[...]).astype(BF16)
    d_ff = w1_ref.shape[1]
    y = jnp.zeros(x1.shape, F32)
    for lo in range(0, d_ff, ff_chunk):
        a = jnp.maximum(_dot(h, w1_ref[:, lo:lo + ff_chunk]), 0.0)
        y = y + _dot((a * a).astype(BF16), w2_ref[lo:lo + ff_chunk, :])
    o_ref[...] = x1 + _rms(y, g_post_mlp_ref[...])


def _out_mlp(x2, oa, ob, w_out, g_post_mix, g_pre_mlp, w1, w2, g_post_mlp, tm, ff_chunk):
    m, d = x2.shape
    d_ff = w1.shape[1]
    const = lambda i: (0, 0)
    row = lambda i: (i, 0)
    once = dict(pipeline_mode=pl.Buffered(1))
    vec = lambda g: g.reshape(1, d).astype(F32)
    return pl.pallas_call(
        functools.partial(_out_mlp_kernel, ff_chunk=ff_chunk),
        grid=(m // tm,),
        in_specs=[
            pl.BlockSpec((tm, d), row),
            pl.BlockSpec((tm, A_WIDTH), row),
            pl.BlockSpec((tm, B_WIDTH), row),
            pl.BlockSpec((A_WIDTH + B_WIDTH, d), const, **once),
            pl.BlockSpec((1, d), const),
            pl.BlockSpec((1, d), const),
            pl.BlockSpec((d, d_ff), const, **once),
            pl.BlockSpec((d_ff, d), const, **once),
            pl.BlockSpec((1, d), const),
        ],
        out_specs=pl.BlockSpec((tm, d), row),
        out_shape=jax.ShapeDtypeStruct((m, d), F32),
        compiler_params=pltpu.CompilerParams(
            dimension_semantics=("arbitrary",), vmem_limit_bytes=VMEM_LIMIT),
        name="out_mlp",
    )(x2, oa, ob, w_out.astype(BF16), vec(g_post_mix), vec(g_pre_mlp),
      w1.astype(BF16), w2.astype(BF16), vec(g_post_mlp))


def kernel(x, w_in, c_norm, w_uk, w_uv, rel_bias, conv_w, a_log, dt_bias, o_norm, w_out,
           pre_norm_mix, post_norm_mix, pre_norm_mlp, post_norm_mlp, w_mlp_in, w_mlp_out):
    batch, seq, d = x.shape
    m = batch * seq
    tm = min(512, m)
    tq = min(256, seq)
    xs = x.reshape(m, d)
    for l in range(w_in.shape[0]):
        qat, qit, qkv, z, c, ct3, kidx, wit, ab, abt = _in_proj(
            xs, pre_norm_mix[l], w_in[l], c_norm[l], tm, tq)
        oa = _dsa(qat, qit, wit, kidx, c, ct3, w_uk[l], w_uv[l], rel_bias, batch, seq, tq, nbisect=16)
        ob = _gdn(qkv, z, ab, abt, conv_w[l], a_log[l], dt_bias[l], o_norm[l], batch, seq)
        xs = _out_mlp(xs, oa, ob, w_out[l], post_norm_mix[l], pre_norm_mlp[l],
                      w_mlp_in[l], w_mlp_out[l], post_norm_mlp[l], tm, ff_chunk=1024)
    return xs.reshape(batch, seq, d)
```

```python
import functools
import math

import numpy as np
import jax
import jax.numpy as jnp
from jax import lax
from jax.experimental import pallas as pl
from jax.experimental.pallas import tpu as pltpu

F32 = jnp.float32
BF16 = jnp.bfloat16

EPS = 1e-6
CHUNK = 64
A_HEADS = 8
A_HEAD_DIM = 64
A_WIDTH = A_HEADS * A_HEAD_DIM
A_LATENT = 128
IDX_HEADS = 8
IDX_DIM = 64
TOPK_MAX = 256
REL_BUCKETS = 32
REL_MAX_DIST = 128
B_HEADS = 4
B_HEAD_DIM = 128
B_WIDTH = B_HEADS * B_HEAD_DIM
CONV_WIDTH = 4
IN_SPLITS = (A_WIDTH, A_LATENT, IDX_HEADS * IDX_DIM, IDX_DIM, IDX_HEADS,
             B_WIDTH, B_WIDTH, B_WIDTH, B_HEADS, B_HEADS, B_WIDTH)

LANES = 128
SUBLANES = 8
VMEM_LIMIT = 56 * 1024 * 1024
NEG_BIG = -1e30
LOG2E = 1.0 / math.log(2.0)
DENOM_ROWS = 16
FINISH_UNROLL = 2

NT_DIMS = (((1,), (1,)), ((), ()))


def _dot(a, b, precision=None):
    return jnp.dot(a, b, preferred_element_type=F32, precision=precision)


def _dot_nt(a, b, precision=None):
    return lax.dot_general(a, b, NT_DIMS, preferred_element_type=F32, precision=precision)


def _rms(x, g):
    return x * lax.rsqrt(jnp.mean(x * x, axis=-1, keepdims=True) + EPS) * g


def _sigmoid(x):
    return 1.0 / (1.0 + jnp.exp(-x))


def _silu(x):
    half = 0.5 * x
    return half + half * jnp.tanh(half)


def _softplus(x):
    return jnp.maximum(x, 0.0) + jnp.log1p(jnp.exp(-jnp.abs(x)))


_C_QKV = 0
_C_Z = _C_QKV + 3 * B_WIDTH
_C_CKV = _C_Z + B_WIDTH
_C_SMALL = _C_CKV + A_LATENT
_C_END = _C_SMALL + LANES
_R_QA = 0
_R_QI = _R_QA + A_WIDTH
_R_CKV = _R_QI + IDX_HEADS * IDX_DIM
_R_SMALL = _R_CKV + A_LATENT
_R_END = _R_SMALL + IDX_HEADS + 2 * B_HEADS


def _in_proj_kernel(x_ref, g_ref, w_ref, wt_ref, cn_ref, cnt_ref,
                    qat_ref, qit_ref, qkv_ref, z_ref, c_ref, ct_ref, kidx_ref, wit_ref, ab_ref, abt_ref,
                    *, ts):
    h = _rms(x_ref[...], g_ref[...]).astype(BF16)

    def proj(lo, hi):
        return _dot(h, w_ref[:, lo:hi])

    def proj_t(lo, hi):
        return _dot_nt(wt_ref[lo:hi, :], h)

    qkv_ref[...] = proj(_C_QKV, _C_Z).astype(BF16)
    z_ref[...] = proj(_C_Z, _C_CKV).astype(BF16)
    c_ref[...] = _rms(proj(_C_CKV, _C_SMALL), cn_ref[...]).astype(BF16)
    small = proj(_C_SMALL, _C_END)
    kidx_ref[...] = small[:, :IDX_DIM].astype(BF16)
    ab_ref[...] = small[:, IDX_DIM:IDX_DIM + 2 * B_HEADS]

    qat_ref[...] = proj_t(_R_QA, _R_QI).astype(BF16)
    qit_ref[...] = (proj_t(_R_QI, _R_CKV) * (IDX_DIM ** -0.5)).astype(BF16)
    ckv_t = proj_t(_R_CKV, _R_SMALL)
    c_t = (ckv_t * lax.rsqrt(jnp.mean(ckv_t * ckv_t, axis=0, keepdims=True) + EPS)
           * cnt_ref[...]).astype(BF16)
    for r in range(ct_ref.shape[0]):
        ct_ref[r] = c_t[:, r * ts:(r + 1) * ts]
    small_t = proj_t(_R_SMALL, _R_END)
    wit_ref[...] = small_t[:IDX_HEADS] * (IDX_HEADS ** -0.5)
    abt_ref[...] = small_t[IDX_HEADS:]


def _in_proj(x2, g, w_in, c_norm, tm, ts):
    m, d = x2.shape
    offs = np.cumsum((0,) + IN_SPLITS)
    seg = [w_in[:, offs[i]:offs[i + 1]] for i in range(len(IN_SPLITS))]
    q_a, c_kv, q_idx, k_idx, w_idx, q_b, k_b, v_b, a_b, b_b, z_b = seg
    pad = jnp.zeros((d, LANES - IDX_DIM - 2 * B_HEADS), w_in.dtype)
    w_all = jnp.concatenate([q_b, k_b, v_b, z_b, c_kv, k_idx, a_b, b_b, pad], axis=1).astype(BF16)
    w_t = jnp.concatenate([q_a, q_idx, c_kv, w_idx, a_b, b_b], axis=1).T.astype(BF16)
    const = lambda i: (0, 0)
    row = lambda i: (i, 0)
    col = lambda i: (0, i)
    out_shape = (
        jax.ShapeDtypeStruct((A_WIDTH, m), BF16),
        jax.ShapeDtypeStruct((IDX_HEADS * IDX_DIM, m), BF16),
        jax.ShapeDtypeStruct((m, 3 * B_WIDTH), BF16),
        jax.ShapeDtypeStruct((m, B_WIDTH), BF16),
        jax.ShapeDtypeStruct((m, A_LATENT), BF16),
        jax.ShapeDtypeStruct((m // ts, A_LATENT, ts), BF16),
        jax.ShapeDtypeStruct((m, IDX_DIM), BF16),
        jax.ShapeDtypeStruct((IDX_HEADS, m), F32),
        jax.ShapeDtypeStruct((m, 2 * B_HEADS), F32),
        jax.ShapeDtypeStruct((2 * B_HEADS, m), F32),
    )
    out_specs = (
        pl.BlockSpec((A_WIDTH, tm), col),
        pl.BlockSpec((IDX_HEADS * IDX_DIM, tm), col),
        pl.BlockSpec((tm, 3 * B_WIDTH), row),
        pl.BlockSpec((tm, B_WIDTH), row),
        pl.BlockSpec((tm, A_LATENT), row),
        pl.BlockSpec((tm // ts, A_LATENT, ts), lambda i: (i, 0, 0)),
        pl.BlockSpec((tm, IDX_DIM), row),
        pl.BlockSpec((IDX_HEADS, tm), col),
        pl.BlockSpec((tm, 2 * B_HEADS), row),
        pl.BlockSpec((2 * B_HEADS, tm), col),
    )
    return pl.pallas_call(
        functools.partial(_in_proj_kernel, ts=ts),
        grid=(m // tm,),
        in_specs=[
            pl.BlockSpec((tm, d), row),
            pl.BlockSpec((1, d), const),
            pl.BlockSpec((d, _C_END), const),
            pl.BlockSpec((_R_END, d), const),
            pl.BlockSpec((1, A_LATENT), const),
            pl.BlockSpec((A_LATENT, 1), const),
        ],
        out_specs=out_specs,
        out_shape=out_shape,
        compiler_params=pltpu.CompilerParams(
            dimension_semantics=("arbitrary",), vmem_limit_bytes=VMEM_LIMIT),
        name="in_proj",
    )(x2, g.reshape(1, d), w_all, w_t, c_norm.reshape(1, A_LATENT), c_norm.reshape(A_LATENT, 1))


def _fold_rows(x, op):
    r = x.shape[0]
    while r > SUBLANES:
        r //= 2
        x = op(x[:r], x[r:])
    return x


def _dsa_kernel(qat_ref, qit_ref, wit_ref, kidx_ref, c_ref, ct_ref, wuk_ref, wuvt_ref, bias_ref, o_ref,
                sc_ref, qabs_ref, m_ref, acc_ref, tie_base_ref, range_ref, *, tq, ts, topk, nbisect):
    i = pl.program_id(1)
    nkb = i + 1
    q_pos = i * tq + lax.broadcasted_iota(jnp.int32, (1, tq), 1)
    limit = (q_pos // CHUNK + 1) * CHUNK
    kf = float(topk)
    ninf = -jnp.inf

    def key_ids(j):
        return j * ts + lax.broadcasted_iota(jnp.int32, (ts, 1), 0)

    def col_reduce(x8, op):
        return op(x8, axis=0, keepdims=True)

    wit = wit_ref[...]

    range_ref[0] = jnp.full((SUBLANES, tq), jnp.inf, F32)
    range_ref[1] = jnp.full((SUBLANES, tq), ninf, F32)

    def score_blocks(js, diagonal):
        k = jnp.concatenate([kidx_ref[j] for j in js], axis=0)
        r = [_dot(k, qit_ref[h * IDX_DIM:(h + 1) * IDX_DIM, :]) for h in range(IDX_HEADS)]
        s = jnp.maximum(r[0], 0.0) * wit[0:1, :]
        for h in range(1, IDX_HEADS):
            s = s + jnp.maximum(r[h], 0.0) * wit[h:h + 1, :]
        lo, hi = s, s
        if diagonal:
            adm = key_ids(js[0]) < limit
            s = jnp.where(adm, s, ninf)
            lo, hi = jnp.where(adm, s, jnp.inf), s
        for n, j in enumerate(js):
            sc_ref[j] = s[n * ts:(n + 1) * ts]
        range_ref[0] = jnp.minimum(range_ref[0], _fold_rows(lo, jnp.minimum))
        range_ref[1] = jnp.maximum(range_ref[1], _fold_rows(hi, jnp.maximum))

    def score_pair(jj, carry):
        score_blocks([2 * jj, 2 * jj + 1], False)
        return carry

    lax.fori_loop(0, i // 2, score_pair, 0)

    @pl.when(i % 2 == 1)
    def _():
        score_blocks([i - 1], False)

    score_blocks([i], True)
    rmin = col_reduce(range_ref[0], jnp.min)
    rmax = col_reduce(range_ref[1], jnp.max)

    def count(pred):
        def body(j, acc):
            return acc + _fold_rows(jnp.where(pred(sc_ref[j], j), 1.0, 0.0), jnp.add)
        acc = lax.fori_loop(0, nkb, body, jnp.zeros((SUBLANES, tq), F32))
        return col_reduce(acc, jnp.sum)

    def count_ge(v):
        return count(lambda s, j: s >= v)

    need = limit > topk

    def bisect_body(_, carry):
        lo, hi, n_hi = carry
        mid = lo + 0.5 * (hi - lo)
        c = count_ge(mid)
        ge = c >= kf
        return jnp.where(ge, mid, lo), jnp.where(ge, hi, mid), jnp.where(ge, n_hi, c)

    hi0 = rmax + jnp.abs(rmax) * 1e-6 + 1e-30
    _, hi, n_hi = lax.fori_loop(0, nbisect, bisect_body, (rmin, hi0, jnp.zeros((1, tq), F32)))

    def max_below(v):
        def body(j, acc):
            s = sc_ref[j]
            return jnp.maximum(acc, _fold_rows(jnp.where(s < v, s, ninf), jnp.maximum))
        return col_reduce(lax.fori_loop(0, nkb, body, jnp.full((SUBLANES, tq), ninf, F32)), jnp.max)

    def finish_cond(state):
        return jnp.min(state[-1]) < 0.5

    def finish_body(state):
        m, n_hi, thr, n_thr, done = state

        def body(j, carry):
            cnt, nxt = carry
            s = sc_ref[j]
            below = s < m
            return (cnt + _fold_rows(jnp.where(below, 0.0, 1.0), jnp.add),
                    jnp.maximum(nxt, _fold_rows(jnp.where(below, s, ninf), jnp.maximum)))

        cnt, nxt = lax.fori_loop(0, nkb, body, (jnp.zeros((SUBLANES, tq), F32),
                                               jnp.full((SUBLANES, tq), ninf, F32)))
        c = col_reduce(cnt, jnp.sum)
        open_ = done < 0.5
        hit = jnp.logical_and(open_, c >= kf)
        miss = jnp.logical_and(open_, c < kf)
        thr = jnp.where(hit, m, thr)
        n_thr = jnp.where(hit, c, n_thr)
        n_hi = jnp.where(miss, c, n_hi)
        m = jnp.where(miss, col_reduce(nxt, jnp.max), m)
        return m, n_hi, thr, n_thr, jnp.where(hit, 1.0, done)

    lowest = float(jnp.finfo(F32).min)
    zero = jnp.zeros((1, tq), F32)
    state = (max_below(hi), n_hi, jnp.full((1, tq), lowest, F32), zero, jnp.where(need, 0.0, 1.0))
    for _ in range(FINISH_UNROLL):
        state = finish_body(state)
    _, n_hi, thr, n_thr, _ = lax.while_loop(finish_cond, finish_body, state)

    n_tie = jnp.where(need, kf - n_hi, 0.0)
    surplus = jnp.max(jnp.where(jnp.logical_and(need, n_thr - n_hi > n_tie), 1.0, 0.0)) > 0.5
    tie_base_ref[...] = zero
    kr = lax.broadcasted_iota(jnp.int32, (ts, ts), 0)
    kc = lax.broadcasted_iota(jnp.int32, (ts, ts), 1)

    def mask_bias(j):
        def plain(_):
            return jnp.where(sc_ref[j] >= thr, 0.0, NEG_BIG)

        def with_ties(_):
            s = sc_ref[j]
            tie = s == thr
            prefix_ones = jnp.where(kc <= kr, 1.0, 0.0).astype(BF16)
            rank = _dot(prefix_ones, jnp.where(tie, 1.0, 0.0).astype(BF16)) + tie_base_ref[...]
            tie_base_ref[...] = rank[ts - 1:ts, :]
            keep = jnp.where(tie, jnp.where(rank <= n_tie, 0.0, NEG_BIG), NEG_BIG)
            return jnp.where(s > thr, 0.0, keep)

        return lax.cond(surplus, with_ties, plain, 0)

    for h in range(A_HEADS):
        qh = _dot(wuk_ref[h], qat_ref[h * A_HEAD_DIM:(h + 1) * A_HEAD_DIM, :])
        qabs_ref[h] = (qh * (A_HEAD_DIM ** -0.5 * LOG2E)).astype(BF16)
    m_ref[...] = jnp.full(m_ref.shape, NEG_BIG, F32)
    acc_ref[...] = jnp.zeros(acc_ref.shape, F32)
    heads = range(A_HEADS)

    def attend(js, bias_of_head):
        n = len(js) * ts
        cj = jnp.concatenate([c_ref[j] for j in js], axis=0)
        ctj = jnp.concatenate([jnp.concatenate([ct_ref[j] for j in js], axis=1),
                               jnp.ones((DENOM_ROWS, n), BF16)], axis=0)
        mb = jnp.concatenate([mask_bias(j) for j in js], axis=0)
        s = [_dot(cj, qabs_ref[h]) for h in heads]
        for h in heads:
            b = bias_of_head(h)
            s[h] = s[h] + (mb if b is None else mb + b)
        m_old = [m_ref[h] for h in heads]
        m_new = [jnp.maximum(m_old[h], col_reduce(_fold_rows(s[h], jnp.maximum), jnp.max))
                 for h in heads]
        alpha = [jnp.exp2(m_old[h] - m_new[h]) for h in heads]
        p = [jnp.exp2(s[h] - m_new[h]).astype(BF16) for h in heads]
        pv = [_dot(ctj, p[h]) for h in heads]
        for h in heads:
            acc_ref[h] = alpha[h] * acc_ref[h] + pv[h]
            m_ref[h] = m_new[h]

    n_far = jnp.maximum(i - 1, 0)

    def far_body(jj, carry):
        attend([2 * jj, 2 * jj + 1], lambda h: None)
        return carry

    lax.fori_loop(0, n_far // 2, far_body, 0)

    @pl.when(n_far % 2 == 1)
    def _():
        attend([n_far - 1], lambda h: None)

    @pl.when(i > 0)
    def _():
        attend([i - 1, i], lambda h: bias_ref[h])

    @pl.when(i == 0)
    def _():
        attend([i], lambda h: bias_ref[h, ts:, :])

    outs = [_dot(wuvt_ref[h],
                 (acc_ref[h, :A_LATENT, :] / acc_ref[h, A_LATENT:A_LATENT + 1, :]).astype(BF16))
            for h in range(A_HEADS)]
    o_ref[...] = jnp.concatenate(outs, axis=0).T.astype(o_ref.dtype)


def _t5_bucket(rel):
    nb = REL_BUCKETS // 2
    max_exact = nb // 2
    side = jnp.where(rel > 0, nb, 0)
    n = jnp.abs(rel)
    nf = jnp.maximum(n, 1).astype(jnp.float32)
    large = max_exact + (jnp.log(nf / max_exact) / math.log(REL_MAX_DIST / max_exact)
                         * (nb - max_exact)).astype(jnp.int32)
    large = jnp.minimum(large, nb - 1)
    return side + jnp.where(n < max_exact, n, large)


def _near_bias_kernel(far_ref, rb_ref, bucket_ref, o_ref):
    bucket = bucket_ref[...]
    for h in range(A_HEADS):
        acc = jnp.zeros(bucket.shape, F32)
        for b in range(REL_BUCKETS):
            acc = jnp.where(bucket == b, rb_ref[b, h], acc)
        o_ref[h] = (acc - rb_ref[far_ref[0], h]) * LOG2E


def _near_bias(rel_bias, tq, ts):
    assert ts >= REL_MAX_DIST
    rel = (jnp.arange(2 * ts)[:, None] - ts) - jnp.arange(tq)[None, :]
    far = _t5_bucket(jnp.full((1,), -(ts + 1), jnp.int32))
    smem = pl.BlockSpec(memory_space=pltpu.SMEM)
    return pl.pallas_call(
        _near_bias_kernel,
        in_specs=[smem, smem, pl.BlockSpec(memory_space=pltpu.VMEM)],
        out_specs=pl.BlockSpec(memory_space=pltpu.VMEM),
        out_shape=jax.ShapeDtypeStruct((A_HEADS, 2 * ts, tq), F32),
        name="near_bias",
    )(far, rel_bias.astype(F32), _t5_bucket(rel).astype(jnp.int32))


def _dsa(qat, qit, wit, kidx, c, ct3, w_uk, w_uv, rel_bias, batch, seq, tq, nbisect):
    ts = tq
    m = batch * seq
    nq = seq // tq
    nks = seq // ts
    topk = min(TOPK_MAX, seq // 4)
    kidx3 = kidx.reshape(m // ts, ts, IDX_DIM)
    c3 = c.reshape(m // ts, ts, A_LATENT)
    wuk = w_uk.astype(BF16)
    wuv_t = jnp.swapaxes(w_uv, 1, 2).astype(BF16)
    bias = _near_bias(rel_bias, tq, ts)
    qcol = lambda b, i: (0, b * nq + i)
    kv = lambda b, i: (b, 0, 0)
    const3 = lambda b, i: (0, 0, 0)
    kern = functools.partial(_dsa_kernel, tq=tq, ts=ts, topk=topk, nbisect=nbisect)
    return pl.pallas_call(
        kern,
        grid=(batch, nq),
        in_specs=[
            pl.BlockSpec((A_WIDTH, tq), qcol),
            pl.BlockSpec((IDX_HEADS * IDX_DIM, tq), qcol),
            pl.BlockSpec((IDX_HEADS, tq), qcol),
            pl.BlockSpec((nks, ts, IDX_DIM), kv),
            pl.BlockSpec((nks, ts, A_LATENT), kv),
            pl.BlockSpec((nks, A_LATENT, ts), kv),
            pl.BlockSpec((A_HEADS, A_LATENT, A_HEAD_DIM), const3),
            pl.BlockSpec((A_HEADS, A_HEAD_DIM, A_LATENT), const3),
            pl.BlockSpec((A_HEADS, 2 * ts, tq), const3),
        ],
        out_specs=pl.BlockSpec((tq, A_WIDTH), lambda b, i: (b * nq + i, 0)),
        out_shape=jax.ShapeDtypeStruct((m, A_WIDTH), BF16),
        scratch_shapes=[
            pltpu.VMEM((nks, ts, tq), F32),
            pltpu.VMEM((A_HEADS, A_LATENT, tq), BF16),
            pltpu.VMEM((A_HEADS, 1, tq), F32),
            pltpu.VMEM((A_HEADS, A_LATENT + DENOM_ROWS, tq), F32),
            pltpu.VMEM((1, tq), F32),
            pltpu.VMEM((2, SUBLANES, tq), F32),
        ],
        compiler_params=pltpu.CompilerParams(
            dimension_semantics=("arbitrary", "arbitrary"), vmem_limit_bytes=VMEM_LIMIT),
        name="dsa",
    )(qat, qit, wit, kidx3, c3, ct3, wuk, wuv_t, bias)


PAIR = 2 * CHUNK
PAIRS_PER_STEP = 2
HI = lax.Precision.HIGHEST


def _gdn_kernel(qkv_ref, z_ref, ab_ref, abt_ref, convw_ref, pcol_ref, prow_ref, onorm_ref, o_ref,
                u_s, w_s, qg_s, kdt_s, a_s, gl_s, st_s, *, seq):
    npair = seq // PAIR
    ri = lax.broadcasted_iota(jnp.int32, (PAIR, PAIR), 0)
    ci = lax.broadcasted_iota(jnp.int32, (PAIR, PAIR), 1)
    same = (ri // CHUNK) == (ci // CHUNK)
    lower_bd = jnp.where(jnp.logical_and(same, ci <= ri), 1.0, 0.0)
    upper_bd = jnp.where(jnp.logical_and(same, ri <= ci), 1.0, 0.0)
    causal_bd = jnp.logical_and(same, ci <= ri)
    diag = ci == ri
    eye = jnp.where(diag, 1.0, 0.0)
    neg_exp_a_col = pcol_ref[:, 0:1]
    dt_col = pcol_ref[:, 1:2]
    neg_exp_a_row = prow_ref[0:1, :]
    dt_row = prow_ref[1:2, :]
    convw = convw_ref[...]
    first_chunk = lax.broadcasted_iota(jnp.int32, (PAIR, 1), 0) < CHUNK

    def aligned(x, m):
        return x if isinstance(x, int) else pl.multiple_of(x, m)

    def conv_silu(p, col):
        r0 = aligned(p * PAIR, PAIR)
        cur = qkv_ref[pl.ds(r0, PAIR), col:col + B_HEAD_DIM].astype(F32)
        prev0 = max(r0 - 16, 0) if isinstance(r0, int) else pl.multiple_of(jnp.maximum(r0 - 16, 0), 16)
        prev = qkv_ref[pl.ds(prev0, 16), col:col + B_HEAD_DIM].astype(F32)
        prev = jnp.where(p > 0, prev, 0.0)
        ext = jnp.concatenate([prev, cur], axis=0)
        y = cur * convw[CONV_WIDTH - 1:CONV_WIDTH, col:col + B_HEAD_DIM]
        for back in range(1, CONV_WIDTH):
            sh = ext[16 - back:16 - back + PAIR]
            y = y + sh * convw[CONV_WIDTH - 1 - back:CONV_WIDTH - back, col:col + B_HEAD_DIM]
        return _silu(y)

    def l2n(x):
        return x * lax.rsqrt(jnp.sum(x * x, axis=-1, keepdims=True) + EPS)

    heads = range(B_HEADS)
    pairs = range(PAIRS_PER_STEP)
    units = [(i, h) for i in pairs for h in heads]

    def slot_of(step, i):
        return (step % 2) * PAIRS_PER_STEP + i

    def phase1(step):
        ps = [step * PAIRS_PER_STEP + i for i in pairs]
        gc_col, gc_row, beta_col = [], [], []
        for p in ps:
            ab = ab_ref[pl.ds(aligned(p * PAIR, PAIR), PAIR), :]
            g_col = neg_exp_a_row * _softplus(ab + dt_row)
            gc_col.append(_dot(lower_bd, g_col, HI))
            beta_col.append(_sigmoid(ab))
            g_row = neg_exp_a_col * _softplus(abt_ref[p] + dt_col)
            gc_row.append(_dot(g_row, upper_bd, HI))
        q = {(i, h): l2n(conv_silu(ps[i], h * B_HEAD_DIM)) * (B_HEAD_DIM ** -0.5) for i, h in units}
        k = {(i, h): l2n(conv_silu(ps[i], B_WIDTH + h * B_HEAD_DIM)) for i, h in units}
        v = {(i, h): conv_silu(ps[i], 2 * B_WIDTH + h * B_HEAD_DIM) for i, h in units}
        gcol = {(i, h): gc_col[i][:, h:h + 1] for i, h in units}
        beta = {(i, h): beta_col[i][:, B_HEADS + h:B_HEADS + h + 1] for i, h in units}
        kb = {u: k[u] * beta[u] for u in units}
        qk_kk = {u: _dot_nt(jnp.concatenate([q[u], kb[u]], axis=0).astype(BF16), k[u].astype(BF16))
                 for u in units}
        decay = {(i, h): jnp.exp(jnp.where(causal_bd, gcol[i, h] - gc_row[i][h:h + 1, :], -jnp.inf))
                 for i, h in units}
        intra = {u: (qk_kk[u][:PAIR] * decay[u]).astype(BF16) for u in units}
        low = {u: jnp.where(diag, 0.0, qk_kk[u][PAIR:] * decay[u]) for u in units}

        x = {u: eye - low[u] for u in units}
        low16 = {u: low[u].astype(BF16) for u in units}
        pw = {u: _dot(low16[u], low16[u]) for u in units}
        for _ in range(int(math.log2(CHUNK)) - 2):
            pw16 = {u: pw[u].astype(BF16) for u in units}
            r = {u: _dot(jnp.concatenate([x[u].astype(BF16), pw16[u]], axis=0), pw16[u]) for u in units}
            x = {u: x[u] + r[u][:PAIR] for u in units}
            pw = {u: r[u][PAIR:] for u in units}
        tinv = {u: x[u] + _dot(x[u].astype(BF16), pw[u].astype(BF16)) for u in units}

        eg = {u: jnp.exp(gcol[u]) for u in units}
        uw = {u: _dot(tinv[u].astype(BF16),
                      jnp.concatenate([v[u] * beta[u], kb[u] * eg[u]], axis=1).astype(BF16))
              for u in units}
        for i, h in units:
            u, slot = (i, h), slot_of(step, i)
            g_last = jnp.where(first_chunk, gcol[u][CHUNK - 1:CHUNK], gcol[u][PAIR - 1:PAIR])
            kd = k[u] * jnp.exp(g_last - gcol[u])
            u_s[slot, h] = uw[u][:, :B_HEAD_DIM]
            w_s[slot, h] = uw[u][:, B_HEAD_DIM:].astype(BF16)
            qg_s[slot, h] = (q[u] * eg[u]).astype(BF16)
            for cc in range(2):
                rs = slice(cc * CHUNK, (cc + 1) * CHUNK)
                kdt_s[slot, h, cc] = kd[rs].T.astype(BF16)
                a_s[slot, h, cc] = intra[u][rs, rs]
                gl_s[slot, h, cc] = jnp.broadcast_to(
                    jnp.exp(gcol[u][(cc + 1) * CHUNK - 1:(cc + 1) * CHUNK]), (8, LANES))

    st_s[...] = jnp.zeros(st_s.shape, F32)
    onorm = onorm_ref[...]

    def phase2(step):
        for i in pairs:
            p, slot = step * PAIRS_PER_STEP + i, slot_of(step, i)
            for cc in range(2):
                rs = slice(cc * CHUNK, (cc + 1) * CHUNK)
                rows = pl.ds(aligned(p * PAIR + cc * CHUNK, CHUNK), CHUNK)
                state = [st_s[h] for h in heads]
                sb = [state[h].astype(BF16) for h in heads]
                wq = [_dot(jnp.concatenate([w_s[slot, h, rs, :], qg_s[slot, h, rs, :]], axis=0), sb[h])
                      for h in heads]
                vb = [(u_s[slot, h, rs, :] - wq[h][:CHUNK]).astype(BF16) for h in heads]
                upd = [_dot(kdt_s[slot, h, cc], vb[h]) for h in heads]
                av = [_dot(a_s[slot, h, cc], vb[h]) for h in heads]
                for h in heads:
                    st_s[h] = state[h] * gl_s[slot, h, cc][0:1, :] + upd[h]
                for h in heads:
                    o = wq[h][CHUNK:] + av[h]
                    z = z_ref[rows, h * B_HEAD_DIM:(h + 1) * B_HEAD_DIM].astype(F32)
                    o_ref[rows, h * B_HEAD_DIM:(h + 1) * B_HEAD_DIM] = (
                        _rms(o, onorm) * _silu(z)).astype(o_ref.dtype)

    def body(step, carry):
        phase2(step - 1)
        phase1(step)
        return carry

    nstep = npair // PAIRS_PER_STEP
    phase1(0)
    lax.fori_loop(1, nstep, body, 0)
    phase2(nstep - 1)


def _gdn(qkv, z, ab, abt, conv_w, a_log, dt_bias, o_norm, batch, seq):
    m = batch * seq
    npair = seq // PAIR
    assert npair % PAIRS_PER_STEP == 0
    nslot = 2 * PAIRS_PER_STEP
    abt3 = jnp.swapaxes(abt.reshape(2 * B_HEADS, m // PAIR, PAIR), 0, 1)
    zeros = jnp.zeros((B_HEADS,), F32)
    nea = jnp.concatenate([-jnp.exp(a_log.astype(F32)), zeros])
    dtb = jnp.concatenate([dt_bias.astype(F32), zeros])
    pcol = jnp.stack([nea, dtb], axis=1)
    prow = jnp.stack([nea, dtb], axis=0)
    const = lambda b: (0, 0)
    rows = lambda b: (b, 0)
    return pl.pallas_call(
        functools.partial(_gdn_kernel, seq=seq),
        grid=(batch,),
        in_specs=[
            pl.BlockSpec((seq, 3 * B_WIDTH), rows),
            pl.BlockSpec((seq, B_WIDTH), rows),
            pl.BlockSpec((seq, 2 * B_HEADS), rows),
            pl.BlockSpec((npair, 2 * B_HEADS, PAIR), lambda b: (b, 0, 0)),
            pl.BlockSpec((CONV_WIDTH, 3 * B_WIDTH), const),
            pl.BlockSpec((2 * B_HEADS, 2), const),
            pl.BlockSpec((2, 2 * B_HEADS), const),
            pl.BlockSpec((1, B_HEAD_DIM), const),
        ],
        out_specs=pl.BlockSpec((seq, B_WIDTH), rows),
        out_shape=jax.ShapeDtypeStruct((m, B_WIDTH), BF16),
        scratch_shapes=[
            pltpu.VMEM((nslot, B_HEADS, PAIR, B_HEAD_DIM), F32),
            pltpu.VMEM((nslot, B_HEADS, PAIR, B_HEAD_DIM), BF16),
            pltpu.VMEM((nslot, B_HEADS, PAIR, B_HEAD_DIM), BF16),
            pltpu.VMEM((nslot, B_HEADS, 2, B_HEAD_DIM, CHUNK), BF16),
            pltpu.VMEM((nslot, B_HEADS, 2, CHUNK, CHUNK), BF16),
            pltpu.VMEM((nslot, B_HEADS, 2, 8, LANES), F32),
            pltpu.VMEM((B_HEADS, B_HEAD_DIM, B_HEAD_DIM), F32),
        ],
        compiler_params=pltpu.CompilerParams(
            dimension_semantics=("arbitrary",), vmem_limit_bytes=VMEM_LIMIT),
        name="gdn",
    )(qkv, z, ab, abt3, conv_w.astype(F32), pcol, prow, o_norm.reshape(1, B_HEAD_DIM).astype(F32))


def _out_mlp_kernel(x_ref, oa_ref, ob_ref, wo_ref, g_post_mix_ref, g_pre_mlp_ref, w1_ref, w2_ref,
                    g_post_mlp_ref, o_ref, *, ff_chunk):
    mix = _dot(oa_ref[...], wo_ref[:A_WIDTH, :]) + _dot(ob_ref[...], wo_ref[A_WIDTH:, :])
    x1 = x_ref[...] + _rms(mix, g_post_mix_ref[...])
    h = _rms(x1, g_pre_mlp_ref[...]).astype(BF16)
    d_ff = w1_ref.shape[1]
    y = jnp.zeros(x1.shape, F32)
    for lo in range(0, d_ff, ff_chunk):
        a = jnp.maximum(_dot(h, w1_ref[:, lo:lo + ff_chunk]), 0.0)
        y = y + _dot((a * a).astype(BF16), w2_ref[lo:lo + ff_chunk, :])
    o_ref[...] = x1 + _rms(y, g_post_mlp_ref[...])


def _out_mlp(x2, oa, ob, w_out, g_post_mix, g_pre_mlp, w1, w2, g_post_mlp, tm, ff_chunk):
    m, d = x2.shape
    d_ff = w1.shape[1]
    const = lambda i: (0, 0)
    row = lambda i: (i, 0)
    once = dict(pipeline_mode=pl.Buffered(1))
    vec = lambda g: g.reshape(1, d).astype(F32)
    return pl.pallas_call(
        functools.partial(_out_mlp_kernel, ff_chunk=ff_chunk),
        grid=(m // tm,),
        in_specs=[
            pl.BlockSpec((tm, d), row),
            pl.BlockSpec((tm, A_WIDTH), row),
            pl.BlockSpec((tm, B_WIDTH), row),
            pl.BlockSpec((A_WIDTH + B_WIDTH, d), const, **once),
            pl.BlockSpec((1, d), const),
            pl.BlockSpec((1, d), const),
            pl.BlockSpec((d, d_ff), const, **once),
            pl.BlockSpec((d_ff, d), const, **once),
            pl.BlockSpec((1, d), const),
        ],
        out_specs=pl.BlockSpec((tm, d), row),
        out_shape=jax.ShapeDtypeStruct((m, d), F32),
        compiler_params=pltpu.CompilerParams(
            dimension_semantics=("arbitrary",), vmem_limit_bytes=VMEM_LIMIT),
        name="out_mlp",
    )(x2, oa, ob, w_out.astype(BF16), vec(g_post_mix), vec(g_pre_mlp),
      w1.astype(BF16), w2.astype(BF16), vec(g_post_mlp))


def kernel(x, w_in, c_norm, w_uk, w_uv, rel_bias, conv_w, a_log, dt_bias, o_norm, w_out,
           pre_norm_mix, post_norm_mix, pre_norm_mlp, post_norm_mlp, w_mlp_in, w_mlp_out):
    batch, seq, d = x.shape
    m = batch * seq
    tm = min(512, m)
    tq = min(256, seq)
    xs = x.reshape(m, d)
    for l in range(w_in.shape[0]):
        qat, qit, qkv, z, c, ct3, kidx, wit, ab, abt = _in_proj(
            xs, pre_norm_mix[l], w_in[l], c_norm[l], tm, tq)
        oa = _dsa(qat, qit, wit, kidx, c, ct3, w_uk[l], w_uv[l], rel_bias, batch, seq, tq, nbisect=16)
        ob = _gdn(qkv, z, ab, abt, conv_w[l], a_log[l], dt_bias[l], o_norm[l], batch, seq)
        xs = _out_mlp(xs, oa, ob, w_out[l], post_norm_mix[l], pre_norm_mlp[l],
                      w_mlp_in[l], w_mlp_out[l], post_norm_mlp[l], tm, ff_chunk=1024)
    return xs.reshape(batch, seq, d)
```

```python
import functools
import math

import numpy as np
import jax
import jax.numpy as jnp
from jax import lax
from jax.experimental import pallas as pl
from jax.experimental.pallas import tpu as pltpu

F32 = jnp.float32
BF16 = jnp.bfloat16

EPS = 1e-6
CHUNK = 64
A_HEADS = 8
A_HEAD_DIM = 64
A_WIDTH = A_HEADS * A_HEAD_DIM
A_LATENT = 128
IDX_HEADS = 8
IDX_DIM = 64
TOPK_MAX = 256
REL_BUCKETS = 32
REL_MAX_DIST = 128
B_HEADS = 4
B_HEAD_DIM = 128
B_WIDTH = B_HEADS * B_HEAD_DIM
CONV_WIDTH = 4
IN_SPLITS = (A_WIDTH, A_LATENT, IDX_HEADS * IDX_DIM, IDX_DIM, IDX_HEADS,
             B_WIDTH, B_WIDTH, B_WIDTH, B_HEADS, B_HEADS, B_WIDTH)

LANES = 128
SUBLANES = 8
BF16_ROWS = 16
VMEM_LIMIT = 56 * 1024 * 1024
ROW_TILE = 1024
IN_ROW_TILE = 1024
QUERY_TILE = 256
FF_CHUNK = 1024
N_BISECT = 16
NEG_BIG = -1e30
LOG2E = 1.0 / math.log(2.0)
DENOM_ROWS = 16
FINISH_UNROLL = 2

NT_DIMS = (((1,), (1,)), ((), ()))


def _dot(a, b, precision=None):
    return jnp.dot(a, b, preferred_element_type=F32, precision=precision)


def _dot_nt(a, b, precision=None):
    return lax.dot_general(a, b, NT_DIMS, preferred_element_type=F32, precision=precision)


def _rms(x, g):
    return x * lax.rsqrt(jnp.mean(x * x, axis=-1, keepdims=True) + EPS) * g


def _sigmoid(x):
    return 1.0 / (1.0 + jnp.exp(-x))


def _silu(x):
    half = 0.5 * x
    return half + half * jnp.tanh(half)


def _softplus(x):
    return jnp.maximum(x, 0.0) + jnp.log1p(jnp.exp(-jnp.abs(x)))


_C_QKV = 0
_C_Z = _C_QKV + 3 * B_WIDTH
_C_CKV = _C_Z + B_WIDTH
_C_SMALL = _C_CKV + A_LATENT
_C_END = _C_SMALL + LANES
_R_QA = 0
_R_QI = _R_QA + A_WIDTH
_R_CKV = _R_QI + IDX_HEADS * IDX_DIM
_R_SMALL = _R_CKV + A_LATENT
_R_END = _R_SMALL + IDX_HEADS + 2 * B_HEADS


def _in_proj_kernel(x_ref, g_ref, w_ref, wt_ref, cn_ref, cnt_ref,
                    qat_ref, qit_ref, qkv_ref, z_ref, c_ref, ct_ref, kidx_ref, wit_ref, ab_ref, abt_ref,
                    *, ts):
    h = _rms(x_ref[...], g_ref[...]).astype(BF16)

    def proj(lo, hi):
        return _dot(h, w_ref[:, lo:hi])

    def proj_t(lo, hi):
        return _dot_nt(wt_ref[lo:hi, :], h)

    qkv_ref[...] = proj(_C_QKV, _C_Z).astype(BF16)
    z_ref[...] = proj(_C_Z, _C_CKV).astype(BF16)
    c_ref[...] = _rms(proj(_C_CKV, _C_SMALL), cn_ref[...]).astype(BF16)
    small = proj(_C_SMALL, _C_END)
    kidx_ref[...] = small[:, :IDX_DIM].astype(BF16)
    ab_ref[...] = small[:, IDX_DIM:IDX_DIM + 2 * B_HEADS]

    qat_ref[...] = proj_t(_R_QA, _R_QI).astype(BF16)
    qit_ref[...] = (proj_t(_R_QI, _R_CKV) * (IDX_DIM ** -0.5)).astype(BF16)
    ckv_t = proj_t(_R_CKV, _R_SMALL)
    c_t = (ckv_t * lax.rsqrt(jnp.mean(ckv_t * ckv_t, axis=0, keepdims=True) + EPS)
           * cnt_ref[...]).astype(BF16)
    for r in range(ct_ref.shape[0]):
        ct_ref[r] = c_t[:, r * ts:(r + 1) * ts]
    small_t = proj_t(_R_SMALL, _R_END)
    wit_ref[...] = small_t[:IDX_HEADS] * (IDX_HEADS ** -0.5)
    abt_ref[...] = small_t[IDX_HEADS:]


def _in_proj(x2, g, w_in, c_norm, tm, ts):
    m, d = x2.shape
    offs = np.cumsum((0,) + IN_SPLITS)
    seg = [w_in[:, offs[i]:offs[i + 1]] for i in range(len(IN_SPLITS))]
    q_a, c_kv, q_idx, k_idx, w_idx, q_b, k_b, v_b, a_b, b_b, z_b = seg
    pad = jnp.zeros((d, LANES - IDX_DIM - 2 * B_HEADS), w_in.dtype)
    w_all = jnp.concatenate([q_b, k_b, v_b, z_b, c_kv, k_idx, a_b, b_b, pad], axis=1).astype(BF16)
    w_t = jnp.concatenate([q_a, q_idx, c_kv, w_idx, a_b, b_b], axis=1).T.astype(BF16)
    const = lambda i: (0, 0)
    row = lambda i: (i, 0)
    col = lambda i: (0, i)
    out_shape = (
        jax.ShapeDtypeStruct((A_WIDTH, m), BF16),
        jax.ShapeDtypeStruct((IDX_HEADS * IDX_DIM, m), BF16),
        jax.ShapeDtypeStruct((m, 3 * B_WIDTH), BF16),
        jax.ShapeDtypeStruct((m, B_WIDTH), BF16),
        jax.ShapeDtypeStruct((m, A_LATENT), BF16),
        jax.ShapeDtypeStruct((m // ts, A_LATENT, ts), BF16),
        jax.ShapeDtypeStruct((m, IDX_DIM), BF16),
        jax.ShapeDtypeStruct((IDX_HEADS, m), F32),
        jax.ShapeDtypeStruct((m, 2 * B_HEADS), F32),
        jax.ShapeDtypeStruct((2 * B_HEADS, m), F32),
    )
    out_specs = (
        pl.BlockSpec((A_WIDTH, tm), col),
        pl.BlockSpec((IDX_HEADS * IDX_DIM, tm), col),
        pl.BlockSpec((tm, 3 * B_WIDTH), row),
        pl.BlockSpec((tm, B_WIDTH), row),
        pl.BlockSpec((tm, A_LATENT), row),
        pl.BlockSpec((tm // ts, A_LATENT, ts), lambda i: (i, 0, 0)),
        pl.BlockSpec((tm, IDX_DIM), row),
        pl.BlockSpec((IDX_HEADS, tm), col),
        pl.BlockSpec((tm, 2 * B_HEADS), row),
        pl.BlockSpec((2 * B_HEADS, tm), col),
    )
    return pl.pallas_call(
        functools.partial(_in_proj_kernel, ts=ts),
        grid=(m // tm,),
        in_specs=[
            pl.BlockSpec((tm, d), row),
            pl.BlockSpec((1, d), const),
            pl.BlockSpec((d, _C_END), const),
            pl.BlockSpec((_R_END, d), const),
            pl.BlockSpec((1, A_LATENT), const),
            pl.BlockSpec((A_LATENT, 1), const),
        ],
        out_specs=out_specs,
        out_shape=out_shape,
        compiler_params=pltpu.CompilerParams(
            dimension_semantics=("arbitrary",), vmem_limit_bytes=VMEM_LIMIT),
        name="in_proj",
    )(x2, g.reshape(1, d), w_all, w_t, c_norm.reshape(1, A_LATENT), c_norm.reshape(A_LATENT, 1))


def _fold_rows(x, op):
    r = x.shape[0]
    while r > SUBLANES:
        r //= 2
        x = op(x[:r], x[r:])
    return x


def _dsa_kernel(qat_ref, qit_ref, wit_ref, kidx_ref, c_ref, ct_ref, wuk_ref, wuvt_ref, bias_ref, o_ref,
                sc_ref, qabs_ref, m_ref, acc_ref, tie_base_ref, range_ref, *, tq, ts, topk, nbisect):
    i = pl.program_id(1)
    nkb = i + 1
    q_pos = i * tq + lax.broadcasted_iota(jnp.int32, (1, tq), 1)
    limit = (q_pos // CHUNK + 1) * CHUNK
    kf = float(topk)
    ninf = -jnp.inf

    def key_ids(j):
        return j * ts + lax.broadcasted_iota(jnp.int32, (ts, 1), 0)

    def col_reduce(x8, op):
        return op(x8, axis=0, keepdims=True)

    wit = wit_ref[...]

    range_ref[0] = jnp.full((SUBLANES, tq), jnp.inf, F32)
    range_ref[1] = jnp.full((SUBLANES, tq), ninf, F32)

    def score_blocks(js, diagonal):
        k = jnp.concatenate([kidx_ref[j] for j in js], axis=0)
        r = [_dot(k, qit_ref[h * IDX_DIM:(h + 1) * IDX_DIM, :]) for h in range(IDX_HEADS)]
        s = jnp.maximum(r[0], 0.0) * wit[0:1, :]
        for h in range(1, IDX_HEADS):
            s = s + jnp.maximum(r[h], 0.0) * wit[h:h + 1, :]
        lo, hi = s, s
        if diagonal:
            adm = key_ids(js[0]) < limit
            s = jnp.where(adm, s, ninf)
            lo, hi = jnp.where(adm, s, jnp.inf), s
        for n, j in enumerate(js):
            sc_ref[j] = s[n * ts:(n + 1) * ts]
        range_ref[0] = jnp.minimum(range_ref[0], _fold_rows(lo, jnp.minimum))
        range_ref[1] = jnp.maximum(range_ref[1], _fold_rows(hi, jnp.maximum))

    def score_pair(jj, carry):
        score_blocks([2 * jj, 2 * jj + 1], False)
        return carry

    lax.fori_loop(0, i // 2, score_pair, 0)

    @pl.when(i % 2 == 1)
    def _():
        score_blocks([i - 1], False)

    score_blocks([i], True)
    rmin = col_reduce(range_ref[0], jnp.min)
    rmax = col_reduce(range_ref[1], jnp.max)

    def count(pred):
        def body(j, acc):
            return acc + _fold_rows(jnp.where(pred(sc_ref[j], j), 1.0, 0.0), jnp.add)
        acc = lax.fori_loop(0, nkb, body, jnp.zeros((SUBLANES, tq), F32))
        return col_reduce(acc, jnp.sum)

    def count_ge(v):
        return count(lambda s, j: s >= v)

    need = limit > topk

    def bisect_body(_, carry):
        lo, hi, n_hi = carry
        mid = lo + 0.5 * (hi - lo)
        c = count_ge(mid)
        ge = c >= kf
        return jnp.where(ge, mid, lo), jnp.where(ge, hi, mid), jnp.where(ge, n_hi, c)

    hi0 = rmax + jnp.abs(rmax) * 1e-6 + 1e-30
    _, hi, n_hi = lax.fori_loop(0, nbisect, bisect_body, (rmin, hi0, jnp.zeros((1, tq), F32)))

    def max_below(v):
        def body(j, acc):
            s = sc_ref[j]
            return jnp.maximum(acc, _fold_rows(jnp.where(s < v, s, ninf), jnp.maximum))
        return col_reduce(lax.fori_loop(0, nkb, body, jnp.full((SUBLANES, tq), ninf, F32)), jnp.max)

    def finish_cond(state):
        return jnp.min(state[-1]) < 0.5

    def finish_body(state):
        m, n_hi, thr, n_thr, done = state

        def body(j, carry):
            cnt, nxt = carry
            s = sc_ref[j]
            below = s < m
            return (cnt + _fold_rows(jnp.where(below, 0.0, 1.0), jnp.add),
                    jnp.maximum(nxt, _fold_rows(jnp.where(below, s, ninf), jnp.maximum)))

        cnt, nxt = lax.fori_loop(0, nkb, body, (jnp.zeros((SUBLANES, tq), F32),
                                               jnp.full((SUBLANES, tq), ninf, F32)))
        c = col_reduce(cnt, jnp.sum)
        open_ = done < 0.5
        hit = jnp.logical_and(open_, c >= kf)
        miss = jnp.logical_and(open_, c < kf)
        thr = jnp.where(hit, m, thr)
        n_thr = jnp.where(hit, c, n_thr)
        n_hi = jnp.where(miss, c, n_hi)
        m = jnp.where(miss, col_reduce(nxt, jnp.max), m)
        return m, n_hi, thr, n_thr, jnp.where(hit, 1.0, done)

    lowest = float(jnp.finfo(F32).min)
    zero = jnp.zeros((1, tq), F32)
    state = (max_below(hi), n_hi, jnp.full((1, tq), lowest, F32), zero, jnp.where(need, 0.0, 1.0))
    for _ in range(FINISH_UNROLL):
        state = finish_body(state)
    _, n_hi, thr, n_thr, _ = lax.while_loop(finish_cond, finish_body, state)

    n_tie = jnp.where(need, kf - n_hi, 0.0)
    surplus = jnp.max(jnp.where(jnp.logical_and(need, n_thr - n_hi > n_tie), 1.0, 0.0)) > 0.5
    tie_base_ref[...] = zero
    kr = lax.broadcasted_iota(jnp.int32, (ts, ts), 0)
    kc = lax.broadcasted_iota(jnp.int32, (ts, ts), 1)

    def mask_bias(j):
        def plain(_):
            return jnp.where(sc_ref[j] >= thr, 0.0, NEG_BIG)

        def with_ties(_):
            s = sc_ref[j]
            tie = s == thr
            prefix_ones = jnp.where(kc <= kr, 1.0, 0.0).astype(BF16)
            rank = _dot(prefix_ones, jnp.where(tie, 1.0, 0.0).astype(BF16)) + tie_base_ref[...]
            tie_base_ref[...] = rank[ts - 1:ts, :]
            keep = jnp.where(tie, jnp.where(rank <= n_tie, 0.0, NEG_BIG), NEG_BIG)
            return jnp.where(s > thr, 0.0, keep)

        return lax.cond(surplus, with_ties, plain, 0)

    for h in range(A_HEADS):
        qh = _dot(wuk_ref[h], qat_ref[h * A_HEAD_DIM:(h + 1) * A_HEAD_DIM, :])
        qabs_ref[h] = (qh * (A_HEAD_DIM ** -0.5 * LOG2E)).astype(BF16)
    m_ref[...] = jnp.full(m_ref.shape, NEG_BIG, F32)
    acc_ref[...] = jnp.zeros(acc_ref.shape, F32)
    heads = range(A_HEADS)

    def attend(js, bias_of_head):
        n = len(js) * ts
        cj = jnp.concatenate([c_ref[j] for j in js], axis=0)
        ctj = jnp.concatenate([jnp.concatenate([ct_ref[j] for j in js], axis=1),
                               jnp.ones((DENOM_ROWS, n), BF16)], axis=0)
        mb = jnp.concatenate([mask_bias(j) for j in js], axis=0)
        s = [_dot(cj, qabs_ref[h]) for h in heads]
        for h in heads:
            b = bias_of_head(h)
            s[h] = s[h] + (mb if b is None else mb + b)
        m_old = [m_ref[h] for h in heads]
        m_new = [jnp.maximum(m_old[h], col_reduce(_fold_rows(s[h], jnp.maximum), jnp.max))
                 for h in heads]
        alpha = [jnp.exp2(m_old[h] - m_new[h]) for h in heads]
        p = [jnp.exp2(s[h] - m_new[h]).astype(BF16) for h in heads]
        pv = [_dot(ctj, p[h]) for h in heads]
        for h in heads:
            acc_ref[h] = alpha[h] * acc_ref[h] + pv[h]
            m_ref[h] = m_new[h]

    n_far = jnp.maximum(i - 1, 0)

    def far_body(jj, carry):
        attend([2 * jj, 2 * jj + 1], lambda h: None)
        return carry

    lax.fori_loop(0, n_far // 2, far_body, 0)

    @pl.when(n_far % 2 == 1)
    def _():
        attend([n_far - 1], lambda h: None)

    @pl.when(i > 0)
    def _():
        attend([i - 1, i], lambda h: bias_ref[h])

    @pl.when(i == 0)
    def _():
        attend([i], lambda h: bias_ref[h, ts:, :])

    outs = [_dot(wuvt_ref[h],
                 (acc_ref[h, :A_LATENT, :] / acc_ref[h, A_LATENT:A_LATENT + 1, :]).astype(BF16))
            for h in range(A_HEADS)]
    o_ref[...] = jnp.concatenate(outs, axis=0).T.astype(o_ref.dtype)


def _t5_bucket(rel):
    nb = REL_BUCKETS // 2
    max_exact = nb // 2
    side = jnp.where(rel > 0, nb, 0)
    n = jnp.abs(rel)
    nf = jnp.maximum(n, 1).astype(jnp.float32)
    large = max_exact + (jnp.log(nf / max_exact) / math.log(REL_MAX_DIST / max_exact)
                         * (nb - max_exact)).astype(jnp.int32)
    large = jnp.minimum(large, nb - 1)
    return side + jnp.where(n < max_exact, n, large)


def _near_bias_kernel(far_ref, rb_ref, bucket_ref, o_ref):
    bucket = bucket_ref[...]
    for h in range(A_HEADS):
        acc = jnp.zeros(bucket.shape, F32)
        for b in range(REL_BUCKETS):
            acc = jnp.where(bucket == b, rb_ref[b, h], acc)
        o_ref[h] = (acc - rb_ref[far_ref[0], h]) * LOG2E


def _near_bias(rel_bias, tq, ts):
    assert ts >= REL_MAX_DIST
    rel = (jnp.arange(2 * ts)[:, None] - ts) - jnp.arange(tq)[None, :]
    far = _t5_bucket(jnp.full((1,), -(ts + 1), jnp.int32))
    smem = pl.BlockSpec(memory_space=pltpu.SMEM)
    return pl.pallas_call(
        _near_bias_kernel,
        in_specs=[smem, smem, pl.BlockSpec(memory_space=pltpu.VMEM)],
        out_specs=pl.BlockSpec(memory_space=pltpu.VMEM),
        out_shape=jax.ShapeDtypeStruct((A_HEADS, 2 * ts, tq), F32),
        name="near_bias",
    )(far, rel_bias.astype(F32), _t5_bucket(rel).astype(jnp.int32))


def _dsa(qat, qit, wit, kidx, c, ct3, w_uk, w_uv, rel_bias, batch, seq, tq, nbisect):
    ts = tq
    m = batch * seq
    nq = seq // tq
    nks = seq // ts
    topk = min(TOPK_MAX, seq // 4)
    kidx3 = kidx.reshape(m // ts, ts, IDX_DIM)
    c3 = c.reshape(m // ts, ts, A_LATENT)
    wuk = w_uk.astype(BF16)
    wuv_t = jnp.swapaxes(w_uv, 1, 2).astype(BF16)
    bias = _near_bias(rel_bias, tq, ts)
    qcol = lambda b, i: (0, b * nq + i)
    kv = lambda b, i: (b, 0, 0)
    const3 = lambda b, i: (0, 0, 0)
    kern = functools.partial(_dsa_kernel, tq=tq, ts=ts, topk=topk, nbisect=nbisect)
    return pl.pallas_call(
        kern,
        grid=(batch, nq),
        in_specs=[
            pl.BlockSpec((A_WIDTH, tq), qcol),
            pl.BlockSpec((IDX_HEADS * IDX_DIM, tq), qcol),
            pl.BlockSpec((IDX_HEADS, tq), qcol),
            pl.BlockSpec((nks, ts, IDX_DIM), kv),
            pl.BlockSpec((nks, ts, A_LATENT), kv),
            pl.BlockSpec((nks, A_LATENT, ts), kv),
            pl.BlockSpec((A_HEADS, A_LATENT, A_HEAD_DIM), const3),
            pl.BlockSpec((A_HEADS, A_HEAD_DIM, A_LATENT), const3),
            pl.BlockSpec((A_HEADS, 2 * ts, tq), const3),
        ],
        out_specs=pl.BlockSpec((tq, A_WIDTH), lambda b, i: (b * nq + i, 0)),
        out_shape=jax.ShapeDtypeStruct((m, A_WIDTH), BF16),
        scratch_shapes=[
            pltpu.VMEM((nks, ts, tq), F32),
            pltpu.VMEM((A_HEADS, A_LATENT, tq), BF16),
            pltpu.VMEM((A_HEADS, 1, tq), F32),
            pltpu.VMEM((A_HEADS, A_LATENT + DENOM_ROWS, tq), F32),
            pltpu.VMEM((1, tq), F32),
            pltpu.VMEM((2, SUBLANES, tq), F32),
        ],
        compiler_params=pltpu.CompilerParams(
            dimension_semantics=("arbitrary", "arbitrary"), vmem_limit_bytes=VMEM_LIMIT),
        name="dsa",
    )(qat, qit, wit, kidx3, c3, ct3, wuk, wuv_t, bias)


PAIR = 2 * CHUNK
PAIRS_PER_STEP = 2
HI = lax.Precision.HIGHEST


def _gdn_kernel(qkv_ref, z_ref, ab_ref, abt_ref, convw_ref, pcol_ref, prow_ref, onorm_ref, o_ref,
                u_s, w_s, qg_s, kdt_s, a_s, gl_s, st_s, *, seq):
    npair = seq // PAIR
    ri = lax.broadcasted_iota(jnp.int32, (PAIR, PAIR), 0)
    ci = lax.broadcasted_iota(jnp.int32, (PAIR, PAIR), 1)
    same = (ri // CHUNK) == (ci // CHUNK)
    lower_bd = jnp.where(jnp.logical_and(same, ci <= ri), 1.0, 0.0)
    upper_bd = jnp.where(jnp.logical_and(same, ri <= ci), 1.0, 0.0)
    causal_bd = jnp.logical_and(same, ci <= ri)
    diag = ci == ri
    eye = jnp.where(diag, 1.0, 0.0)
    neg_exp_a_col = -jnp.exp(pcol_ref[:, 0:1])
    dt_col = pcol_ref[:, 1:2]
    neg_exp_a_row = -jnp.exp(prow_ref[0:1, :])
    dt_row = prow_ref[1:2, :]
    convw = convw_ref[...]
    first_chunk = lax.broadcasted_iota(jnp.int32, (PAIR, 1), 0) < CHUNK

    def aligned(x, m):
        return x if isinstance(x, int) else pl.multiple_of(x, m)

    def conv_silu(p, col):
        r0 = aligned(p * PAIR, PAIR)
        cur = qkv_ref[pl.ds(r0, PAIR), col:col + B_HEAD_DIM].astype(F32)
        lead = BF16_ROWS
        prev0 = (max(r0 - lead, 0) if isinstance(r0, int)
                 else pl.multiple_of(jnp.maximum(r0 - lead, 0), lead))
        prev = qkv_ref[pl.ds(prev0, lead), col:col + B_HEAD_DIM].astype(F32)
        prev = jnp.where(p > 0, prev, 0.0)
        ext = jnp.concatenate([prev, cur], axis=0)
        y = cur * convw[CONV_WIDTH - 1:CONV_WIDTH, col:col + B_HEAD_DIM]
        for back in range(1, CONV_WIDTH):
            sh = ext[lead - back:lead - back + PAIR]
            y = y + sh * convw[CONV_WIDTH - 1 - back:CONV_WIDTH - back, col:col + B_HEAD_DIM]
        return _silu(y)

    def l2n(x):
        return x * lax.rsqrt(jnp.sum(x * x, axis=-1, keepdims=True) + EPS)

    heads = range(B_HEADS)
    pairs = range(PAIRS_PER_STEP)
    units = [(i, h) for i in pairs for h in heads]

    def slot_of(step, i):
        return (step % 2) * PAIRS_PER_STEP + i

    def phase1(step):
        ps = [step * PAIRS_PER_STEP + i for i in pairs]
        gc_col, gc_row, beta_col = [], [], []
        for p in ps:
            ab = ab_ref[pl.ds(aligned(p * PAIR, PAIR), PAIR), :]
            g_col = neg_exp_a_row * _softplus(ab + dt_row)
            gc_col.append(_dot(lower_bd, g_col, HI))
            beta_col.append(_sigmoid(ab))
            g_row = neg_exp_a_col * _softplus(abt_ref[p] + dt_col)
            gc_row.append(_dot(g_row, upper_bd, HI))
        q = {(i, h): l2n(conv_silu(ps[i], h * B_HEAD_DIM)) * (B_HEAD_DIM ** -0.5) for i, h in units}
        k = {(i, h): l2n(conv_silu(ps[i], B_WIDTH + h * B_HEAD_DIM)) for i, h in units}
        v = {(i, h): conv_silu(ps[i], 2 * B_WIDTH + h * B_HEAD_DIM) for i, h in units}
        gcol = {(i, h): gc_col[i][:, h:h + 1] for i, h in units}
        beta = {(i, h): beta_col[i][:, B_HEADS + h:B_HEADS + h + 1] for i, h in units}
        kb = {u: k[u] * beta[u] for u in units}
        qk_kk = {u: _dot_nt(jnp.concatenate([q[u], kb[u]], axis=0).astype(BF16), k[u].astype(BF16))
                 for u in units}
        decay = {(i, h): jnp.exp(jnp.where(causal_bd, gcol[i, h] - gc_row[i][h:h + 1, :], -jnp.inf))
                 for i, h in units}
        intra = {u: (qk_kk[u][:PAIR] * decay[u]).astype(BF16) for u in units}
        low = {u: jnp.where(diag, 0.0, qk_kk[u][PAIR:] * decay[u]) for u in units}

        x = {u: eye - low[u] for u in units}
        low16 = {u: low[u].astype(BF16) for u in units}
        pw = {u: _dot(low16[u], low16[u]) for u in units}
        for _ in range(int(math.log2(CHUNK)) - 2):
            pw16 = {u: pw[u].astype(BF16) for u in units}
            r = {u: _dot(jnp.concatenate([x[u].astype(BF16), pw16[u]], axis=0), pw16[u]) for u in units}
            x = {u: x[u] + r[u][:PAIR] for u in units}
            pw = {u: r[u][PAIR:] for u in units}
        tinv = {u: x[u] + _dot(x[u].astype(BF16), pw[u].astype(BF16)) for u in units}

        eg = {u: jnp.exp(gcol[u]) for u in units}
        uw = {u: _dot(tinv[u].astype(BF16),
                      jnp.concatenate([v[u] * beta[u], kb[u] * eg[u]], axis=1).astype(BF16))
              for u in units}
        for i, h in units:
            u, slot = (i, h), slot_of(step, i)
            g_last = jnp.where(first_chunk, gcol[u][CHUNK - 1:CHUNK], gcol[u][PAIR - 1:PAIR])
            kd = k[u] * jnp.exp(g_last - gcol[u])
            u_s[slot, h] = uw[u][:, :B_HEAD_DIM]
            w_s[slot, h] = uw[u][:, B_HEAD_DIM:].astype(BF16)
            qg_s[slot, h] = (q[u] * eg[u]).astype(BF16)
            for cc in range(2):
                rs = slice(cc * CHUNK, (cc + 1) * CHUNK)
                kdt_s[slot, h, cc] = kd[rs].T.astype(BF16)
                a_s[slot, h, cc] = intra[u][rs, rs]
                gl_s[slot, h, cc] = jnp.broadcast_to(
                    jnp.exp(gcol[u][(cc + 1) * CHUNK - 1:(cc + 1) * CHUNK]), (SUBLANES, LANES))

    st_s[...] = jnp.zeros(st_s.shape, F32)
    onorm = onorm_ref[...]

    def phase2(step):
        for i in pairs:
            p, slot = step * PAIRS_PER_STEP + i, slot_of(step, i)
            for cc in range(2):
                rs = slice(cc * CHUNK, (cc + 1) * CHUNK)
                rows = pl.ds(aligned(p * PAIR + cc * CHUNK, CHUNK), CHUNK)
                state = [st_s[h] for h in heads]
                sb = [state[h].astype(BF16) for h in heads]
                wq = [_dot(jnp.concatenate([w_s[slot, h, rs, :], qg_s[slot, h, rs, :]], axis=0), sb[h])
                      for h in heads]
                vb = [(u_s[slot, h, rs, :] - wq[h][:CHUNK]).astype(BF16) for h in heads]
                upd = [_dot(kdt_s[slot, h, cc], vb[h]) for h in heads]
                av = [_dot(a_s[slot, h, cc], vb[h]) for h in heads]
                for h in heads:
                    st_s[h] = state[h] * gl_s[slot, h, cc][0:1, :] + upd[h]
                for h in heads:
                    o = wq[h][CHUNK:] + av[h]
                    z = z_ref[rows, h * B_HEAD_DIM:(h + 1) * B_HEAD_DIM].astype(F32)
                    o_ref[rows, h * B_HEAD_DIM:(h + 1) * B_HEAD_DIM] = (
                        _rms(o, onorm) * _silu(z)).astype(o_ref.dtype)

    def body(step, carry):
        phase2(step - 1)
        phase1(step)
        return carry

    nstep = npair // PAIRS_PER_STEP
    phase1(0)
    lax.fori_loop(1, nstep, body, 0)
    phase2(nstep - 1)


def _gdn(qkv, z, ab, abt, conv_w, a_log, dt_bias, o_norm, batch, seq):
    m = batch * seq
    npair = seq // PAIR
    assert npair % PAIRS_PER_STEP == 0
    nslot = 2 * PAIRS_PER_STEP
    abt3 = jnp.swapaxes(abt.reshape(2 * B_HEADS, m // PAIR, PAIR), 0, 1)
    zeros = jnp.zeros((B_HEADS,), F32)
    alog = jnp.concatenate([a_log.astype(F32), zeros])
    dtb = jnp.concatenate([dt_bias.astype(F32), zeros])
    pcol = jnp.stack([alog, dtb], axis=1)
    prow = jnp.stack([alog, dtb], axis=0)
    const = lambda b: (0, 0)
    rows = lambda b: (b, 0)
    return pl.pallas_call(
        functools.partial(_gdn_kernel, seq=seq),
        grid=(batch,),
        in_specs=[
            pl.BlockSpec((seq, 3 * B_WIDTH), rows),
            pl.BlockSpec((seq, B_WIDTH), rows),
            pl.BlockSpec((seq, 2 * B_HEADS), rows),
            pl.BlockSpec((npair, 2 * B_HEADS, PAIR), lambda b: (b, 0, 0)),
            pl.BlockSpec((CONV_WIDTH, 3 * B_WIDTH), const),
            pl.BlockSpec((2 * B_HEADS, 2), const),
            pl.BlockSpec((2, 2 * B_HEADS), const),
            pl.BlockSpec((1, B_HEAD_DIM), const),
        ],
        out_specs=pl.BlockSpec((seq, B_WIDTH), rows),
        out_shape=jax.ShapeDtypeStruct((m, B_WIDTH), BF16),
        scratch_shapes=[
            pltpu.VMEM((nslot, B_HEADS, PAIR, B_HEAD_DIM), F32),
            pltpu.VMEM((nslot, B_HEADS, PAIR, B_HEAD_DIM), BF16),
            pltpu.VMEM((nslot, B_HEADS, PAIR, B_HEAD_DIM), BF16),
            pltpu.VMEM((nslot, B_HEADS, 2, B_HEAD_DIM, CHUNK), BF16),
            pltpu.VMEM((nslot, B_HEADS, 2, CHUNK, CHUNK), BF16),
            pltpu.VMEM((nslot, B_HEADS, 2, SUBLANES, LANES), F32),
            pltpu.VMEM((B_HEADS, B_HEAD_DIM, B_HEAD_DIM), F32),
        ],
        compiler_params=pltpu.CompilerParams(
            dimension_semantics=("arbitrary",), vmem_limit_bytes=VMEM_LIMIT),
        name="gdn",
    )(qkv, z, ab, abt3, conv_w.astype(F32), pcol, prow, o_norm.reshape(1, B_HEAD_DIM).astype(F32))


def _out_mlp_kernel(x_ref, oa_ref, ob_ref, wo_ref, g_post_mix_ref, g_pre_mlp_ref, w1_ref, w2_ref,
                    g_post_mlp_ref, o_ref, *, ff_chunk):
    mix = _dot(oa_ref[...], wo_ref[:A_WIDTH, :]) + _dot(ob_ref[...], wo_ref[A_WIDTH:, :])
    x1 = x_ref[...] + _rms(mix, g_post_mix_ref[...])
    h = _rms(x1, g_pre_mlp_ref[...]).astype(BF16)
    d_ff = w1_ref.shape[1]
    y = jnp.zeros(x1.shape, F32)
    for lo in range(0, d_ff, ff_chunk):
        a = jnp.maximum(_dot(h, w1_ref[:, lo:lo + ff_chunk]), 0.0)
        y = y + _dot((a * a).astype(BF16), w2_ref[lo:lo + ff_chunk, :])
    o_ref[...] = x1 + _rms(y, g_post_mlp_ref[...])


def _out_mlp(x2, oa, ob, w_out, g_post_mix, g_pre_mlp, w1, w2, g_post_mlp, tm, ff_chunk):
    m, d = x2.shape
    d_ff = w1.shape[1]
    const = lambda i: (0, 0)
    row = lambda i: (i, 0)
    once = dict(pipeline_mode=pl.Buffered(1))
    vec = lambda g: g.reshape(1, d).astype(F32)
    return pl.pallas_call(
        functools.partial(_out_mlp_kernel, ff_chunk=ff_chunk),
        grid=(m // tm,),
        in_specs=[
            pl.BlockSpec((tm, d), row),
            pl.BlockSpec((tm, A_WIDTH), row),
            pl.BlockSpec((tm, B_WIDTH), row),
            pl.BlockSpec((A_WIDTH + B_WIDTH, d), const, **once),
            pl.BlockSpec((1, d), const),
            pl.BlockSpec((1, d), const),
            pl.BlockSpec((d, d_ff), const, **once),
            pl.BlockSpec((d_ff, d), const, **once),
            pl.BlockSpec((1, d), const),
        ],
        out_specs=pl.BlockSpec((tm, d), row),
        out_shape=jax.ShapeDtypeStruct((m, d), F32),
        compiler_params=pltpu.CompilerParams(
            dimension_semantics=("arbitrary",), vmem_limit_bytes=VMEM_LIMIT),
        name="out_mlp",
    )(x2, oa, ob, w_out.astype(BF16), vec(g_post_mix), vec(g_pre_mlp),
      w1.astype(BF16), w2.astype(BF16), vec(g_post_mlp))


def kernel(x, w_in, c_norm, w_uk, w_uv, rel_bias, conv_w, a_log, dt_bias, o_norm, w_out,
           pre_norm_mix, post_norm_mix, pre_norm_mlp, post_norm_mlp, w_mlp_in, w_mlp_out):
    batch, seq, d = x.shape
    m = batch * seq
    tm = min(ROW_TILE, m)
    tm_in = min(IN_ROW_TILE, m)
    tq = min(QUERY_TILE, seq)
    assert m % tm == 0 and m % tm_in == 0 and seq % tq == 0 and seq % (PAIR * PAIRS_PER_STEP) == 0
    xs = x.reshape(m, d)
    for l in range(w_in.shape[0]):
        qat, qit, qkv, z, c, ct3, kidx, wit, ab, abt = _in_proj(
            xs, pre_norm_mix[l], w_in[l], c_norm[l], tm_in, tq)
        oa = _dsa(qat, qit, wit, kidx, c, ct3, w_uk[l], w_uv[l], rel_bias, batch, seq, tq, N_BISECT)
        ob = _gdn(qkv, z, ab, abt, conv_w[l], a_log[l], dt_bias[l], o_norm[l], batch, seq)
        xs = _out_mlp(xs, oa, ob, w_out[l], post_norm_mix[l], pre_norm_mlp[l],
                      w_mlp_in[l], w_mlp_out[l], post_norm_mlp[l], tm, FF_CHUNK)
    return xs.reshape(batch, seq, d)
```

```python
import functools
import math

import numpy as np
import jax
import jax.numpy as jnp
from jax import lax
from jax.experimental import pallas as pl
from jax.experimental.pallas import tpu as pltpu

F32 = jnp.float32
BF16 = jnp.bfloat16

EPS = 1e-6
CHUNK = 64
A_HEADS = 8
A_HEAD_DIM = 64
A_WIDTH = A_HEADS * A_HEAD_DIM
A_LATENT = 128
IDX_HEADS = 8
IDX_DIM = 64
TOPK_MAX = 256
REL_BUCKETS = 32
REL_MAX_DIST = 128
B_HEADS = 4
B_HEAD_DIM = 128
B_WIDTH = B_HEADS * B_HEAD_DIM
CONV_WIDTH = 4
IN_SPLITS = (A_WIDTH, A_LATENT, IDX_HEADS * IDX_DIM, IDX_DIM, IDX_HEADS,
             B_WIDTH, B_WIDTH, B_WIDTH, B_HEADS, B_HEADS, B_WIDTH)

LANES = 128
SUBLANES = 8
BF16_ROWS = 16
VMEM_LIMIT = 56 * 1024 * 1024
ROW_TILE = 1024
IN_ROW_TILE = 1024
QUERY_TILE = 256
FF_CHUNK = 1024
N_BISECT = 16
NEG_BIG = -1e30
LOG2E = 1.0 / math.log(2.0)
DENOM_ROWS = 16
FINISH_UNROLL = 2

NT_DIMS = (((1,), (1,)), ((), ()))


def _dot(a, b, precision=None):
    return jnp.dot(a, b, preferred_element_type=F32, precision=precision)


def _dot_nt(a, b, precision=None):
    return lax.dot_general(a, b, NT_DIMS, preferred_element_type=F32, precision=precision)


def _rms(x, g):
    return x * lax.rsqrt(jnp.mean(x * x, axis=-1, keepdims=True) + EPS) * g


def _sigmoid(x):
    return 1.0 / (1.0 + jnp.exp(-x))


def _silu(x):
    half = 0.5 * x
    return half + half * jnp.tanh(half)


def _softplus(x):
    return jnp.maximum(x, 0.0) + jnp.log1p(jnp.exp(-jnp.abs(x)))


_C_QKV = 0
_C_Z = _C_QKV + 3 * B_WIDTH
_C_CKV = _C_Z + B_WIDTH
_C_SMALL = _C_CKV + A_LATENT
_C_END = _C_SMALL + LANES
_R_QA = 0
_R_QI = _R_QA + A_WIDTH
_R_CKV = _R_QI + IDX_HEADS * IDX_DIM
_R_SMALL = _R_CKV + A_LATENT
_R_END = _R_SMALL + IDX_HEADS + 2 * B_HEADS


def _in_proj_kernel(x_ref, g_ref, w_ref, wt_ref, cn_ref, cnt_ref,
                    qat_ref, qit_ref, qkv_ref, z_ref, c_ref, ct_ref, kidx_ref, wit_ref, ab_ref, abt_ref,
                    *, ts):
    h = _rms(x_ref[...], g_ref[...]).astype(BF16)

    def proj(lo, hi):
        return _dot(h, w_ref[:, lo:hi])

    def proj_t(lo, hi):
        return _dot_nt(wt_ref[lo:hi, :], h)

    qkv_ref[...] = proj(_C_QKV, _C_Z).astype(BF16)
    z_ref[...] = proj(_C_Z, _C_CKV).astype(BF16)
    c_ref[...] = _rms(proj(_C_CKV, _C_SMALL), cn_ref[...]).astype(BF16)
    small = proj(_C_SMALL, _C_END)
    kidx_ref[...] = small[:, :IDX_DIM].astype(BF16)
    ab_ref[...] = small[:, IDX_DIM:IDX_DIM + 2 * B_HEADS]

    qat_ref[...] = proj_t(_R_QA, _R_QI).astype(BF16)
    qit_ref[...] = (proj_t(_R_QI, _R_CKV) * (IDX_DIM ** -0.5)).astype(BF16)
    ckv_t = proj_t(_R_CKV, _R_SMALL)
    c_t = (ckv_t * lax.rsqrt(jnp.mean(ckv_t * ckv_t, axis=0, keepdims=True) + EPS)
           * cnt_ref[...]).astype(BF16)
    for r in range(ct_ref.shape[0]):
        ct_ref[r] = c_t[:, r * ts:(r + 1) * ts]
    small_t = proj_t(_R_SMALL, _R_END)
    wit_ref[...] = small_t[:IDX_HEADS] * (IDX_HEADS ** -0.5)
    abt_ref[...] = small_t[IDX_HEADS:]


def _in_proj(x2, g, w_in, c_norm, tm, ts):
    m, d = x2.shape
    offs = np.cumsum((0,) + IN_SPLITS)
    seg = [w_in[:, offs[i]:offs[i + 1]] for i in range(len(IN_SPLITS))]
    q_a, c_kv, q_idx, k_idx, w_idx, q_b, k_b, v_b, a_b, b_b, z_b = seg
    pad = jnp.zeros((d, LANES - IDX_DIM - 2 * B_HEADS), w_in.dtype)
    w_all = jnp.concatenate([q_b, k_b, v_b, z_b, c_kv, k_idx, a_b, b_b, pad], axis=1).astype(BF16)
    w_t = jnp.concatenate([q_a, q_idx, c_kv, w_idx, a_b, b_b], axis=1).T.astype(BF16)
    const = lambda i: (0, 0)
    row = lambda i: (i, 0)
    col = lambda i: (0, i)
    out_shape = (
        jax.ShapeDtypeStruct((A_WIDTH, m), BF16),
        jax.ShapeDtypeStruct((IDX_HEADS * IDX_DIM, m), BF16),
        jax.ShapeDtypeStruct((m, 3 * B_WIDTH), BF16),
        jax.ShapeDtypeStruct((m, B_WIDTH), BF16),
        jax.ShapeDtypeStruct((m, A_LATENT), BF16),
        jax.ShapeDtypeStruct((m // ts, A_LATENT, ts), BF16),
        jax.ShapeDtypeStruct((m, IDX_DIM), BF16),
        jax.ShapeDtypeStruct((IDX_HEADS, m), F32),
        jax.ShapeDtypeStruct((m, 2 * B_HEADS), F32),
        jax.ShapeDtypeStruct((2 * B_HEADS, m), F32),
    )
    out_specs = (
        pl.BlockSpec((A_WIDTH, tm), col),
        pl.BlockSpec((IDX_HEADS * IDX_DIM, tm), col),
        pl.BlockSpec((tm, 3 * B_WIDTH), row),
        pl.BlockSpec((tm, B_WIDTH), row),
        pl.BlockSpec((tm, A_LATENT), row),
        pl.BlockSpec((tm // ts, A_LATENT, ts), lambda i: (i, 0, 0)),
        pl.BlockSpec((tm, IDX_DIM), row),
        pl.BlockSpec((IDX_HEADS, tm), col),
        pl.BlockSpec((tm, 2 * B_HEADS), row),
        pl.BlockSpec((2 * B_HEADS, tm), col),
    )
    return pl.pallas_call(
        functools.partial(_in_proj_kernel, ts=ts),
        grid=(m // tm,),
        in_specs=[
            pl.BlockSpec((tm, d), row),
            pl.BlockSpec((1, d), const),
            pl.BlockSpec((d, _C_END), const),
            pl.BlockSpec((_R_END, d), const),
            pl.BlockSpec((1, A_LATENT), const),
            pl.BlockSpec((A_LATENT, 1), const),
        ],
        out_specs=out_specs,
        out_shape=out_shape,
        compiler_params=pltpu.CompilerParams(
            dimension_semantics=("arbitrary",), vmem_limit_bytes=VMEM_LIMIT),
        name="in_proj",
    )(x2, g.reshape(1, d), w_all, w_t, c_norm.reshape(1, A_LATENT), c_norm.reshape(A_LATENT, 1))


def _fold_rows(x, op):
    r = x.shape[0]
    while r > SUBLANES:
        r //= 2
        x = op(x[:r], x[r:])
    return x


def _dsa_kernel(qat_ref, qit_ref, wit_ref, kidx_ref, c_ref, ct_ref, wuk_ref, wuvt_ref, bias_ref, o_ref,
                sc_ref, qabs_ref, m_ref, acc_ref, tie_base_ref, range_ref, *, tq, ts, topk, nbisect):
    i = pl.program_id(1)
    nkb = i + 1
    q_pos = i * tq + lax.broadcasted_iota(jnp.int32, (1, tq), 1)
    limit = (q_pos // CHUNK + 1) * CHUNK
    kf = float(topk)
    ninf = -jnp.inf

    def key_ids(j):
        return j * ts + lax.broadcasted_iota(jnp.int32, (ts, 1), 0)

    def col_reduce(x8, op):
        return op(x8, axis=0, keepdims=True)

    wit = wit_ref[...]

    range_ref[0] = jnp.full((SUBLANES, tq), jnp.inf, F32)
    range_ref[1] = jnp.full((SUBLANES, tq), ninf, F32)

    def score_blocks(js, diagonal):
        k = jnp.concatenate([kidx_ref[j] for j in js], axis=0)
        r = [_dot(k, qit_ref[h * IDX_DIM:(h + 1) * IDX_DIM, :]) for h in range(IDX_HEADS)]
        s = jnp.maximum(r[0], 0.0) * wit[0:1, :]
        for h in range(1, IDX_HEADS):
            s = s + jnp.maximum(r[h], 0.0) * wit[h:h + 1, :]
        lo, hi = s, s
        if diagonal:
            adm = key_ids(js[0]) < limit
            s = jnp.where(adm, s, ninf)
            lo, hi = jnp.where(adm, s, jnp.inf), s
        for n, j in enumerate(js):
            sc_ref[j] = s[n * ts:(n + 1) * ts]
        range_ref[0] = jnp.minimum(range_ref[0], _fold_rows(lo, jnp.minimum))
        range_ref[1] = jnp.maximum(range_ref[1], _fold_rows(hi, jnp.maximum))

    def score_pair(jj, carry):
        score_blocks([2 * jj, 2 * jj + 1], False)
        return carry

    lax.fori_loop(0, i // 2, score_pair, 0)

    @pl.when(i % 2 == 1)
    def _():
        score_blocks([i - 1], False)

    score_blocks([i], True)
    rmin = col_reduce(range_ref[0], jnp.min)
    rmax = col_reduce(range_ref[1], jnp.max)

    def count(pred):
        def body(j, acc):
            return acc + _fold_rows(jnp.where(pred(sc_ref[j], j), 1.0, 0.0), jnp.add)
        acc = lax.fori_loop(0, nkb, body, jnp.zeros((SUBLANES, tq), F32))
        return col_reduce(acc, jnp.sum)

    def count_ge(v):
        return count(lambda s, j: s >= v)

    need = limit > topk

    def bisect_body(_, carry):
        lo, hi, n_hi = carry
        mid = lo + 0.5 * (hi - lo)
        c = count_ge(mid)
        ge = c >= kf
        return jnp.where(ge, mid, lo), jnp.where(ge, hi, mid), jnp.where(ge, n_hi, c)

    hi0 = rmax + jnp.abs(rmax) * 1e-6 + 1e-30
    _, hi, n_hi = lax.fori_loop(0, nbisect, bisect_body, (rmin, hi0, jnp.zeros((1, tq), F32)))

    def max_below(v):
        def body(j, acc):
            s = sc_ref[j]
            return jnp.maximum(acc, _fold_rows(jnp.where(s < v, s, ninf), jnp.maximum))
        return col_reduce(lax.fori_loop(0, nkb, body, jnp.full((SUBLANES, tq), ninf, F32)), jnp.max)

    def finish_cond(state):
        return jnp.min(state[-1]) < 0.5

    def finish_body(state):
        m, n_hi, thr, n_thr, done = state

        def body(j, carry):
            cnt, nxt = carry
            s = sc_ref[j]
            below = s < m
            return (cnt + _fold_rows(jnp.where(below, 0.0, 1.0), jnp.add),
                    jnp.maximum(nxt, _fold_rows(jnp.where(below, s, ninf), jnp.maximum)))

        cnt, nxt = lax.fori_loop(0, nkb, body, (jnp.zeros((SUBLANES, tq), F32),
                                               jnp.full((SUBLANES, tq), ninf, F32)))
        c = col_reduce(cnt, jnp.sum)
        open_ = done < 0.5
        hit = jnp.logical_and(open_, c >= kf)
        miss = jnp.logical_and(open_, c < kf)
        thr = jnp.where(hit, m, thr)
        n_thr = jnp.where(hit, c, n_thr)
        n_hi = jnp.where(miss, c, n_hi)
        m = jnp.where(miss, col_reduce(nxt, jnp.max), m)
        return m, n_hi, thr, n_thr, jnp.where(hit, 1.0, done)

    lowest = float(jnp.finfo(F32).min)
    zero = jnp.zeros((1, tq), F32)
    state = (max_below(hi), n_hi, jnp.full((1, tq), lowest, F32), zero, jnp.where(need, 0.0, 1.0))
    for _ in range(FINISH_UNROLL):
        state = finish_body(state)
    _, n_hi, thr, n_thr, _ = lax.while_loop(finish_cond, finish_body, state)

    n_tie = jnp.where(need, kf - n_hi, 0.0)
    surplus = jnp.max(jnp.where(jnp.logical_and(need, n_thr - n_hi > n_tie), 1.0, 0.0)) > 0.5
    tie_base_ref[...] = zero
    kr = lax.broadcasted_iota(jnp.int32, (ts, ts), 0)
    kc = lax.broadcasted_iota(jnp.int32, (ts, ts), 1)

    def mask_bias(j):
        def plain(_):
            return jnp.where(sc_ref[j] >= thr, 0.0, NEG_BIG)

        def with_ties(_):
            s = sc_ref[j]
            tie = s == thr
            prefix_ones = jnp.where(kc <= kr, 1.0, 0.0).astype(BF16)
            rank = _dot(prefix_ones, jnp.where(tie, 1.0, 0.0).astype(BF16)) + tie_base_ref[...]
            tie_base_ref[...] = rank[ts - 1:ts, :]
            keep = jnp.where(tie, jnp.where(rank <= n_tie, 0.0, NEG_BIG), NEG_BIG)
            return jnp.where(s > thr, 0.0, keep)

        return lax.cond(surplus, with_ties, plain, 0)

    for h in range(A_HEADS):
        qh = _dot(wuk_ref[h], qat_ref[h * A_HEAD_DIM:(h + 1) * A_HEAD_DIM, :])
        qabs_ref[h] = (qh * (A_HEAD_DIM ** -0.5 * LOG2E)).astype(BF16)
    m_ref[...] = jnp.full(m_ref.shape, NEG_BIG, F32)
    acc_ref[...] = jnp.zeros(acc_ref.shape, F32)
    heads = range(A_HEADS)

    def attend(js, bias_of_head):
        n = len(js) * ts
        cj = jnp.concatenate([c_ref[j] for j in js], axis=0)
        ctj = jnp.concatenate([jnp.concatenate([ct_ref[j] for j in js], axis=1),
                               jnp.ones((DENOM_ROWS, n), BF16)], axis=0)
        mb = jnp.concatenate([mask_bias(j) for j in js], axis=0)
        s = [_dot(cj, qabs_ref[h]) for h in heads]
        for h in heads:
            b = bias_of_head(h)
            s[h] = s[h] + (mb if b is None else mb + b)
        m_old = [m_ref[h] for h in heads]
        m_new = [jnp.maximum(m_old[h], col_reduce(_fold_rows(s[h], jnp.maximum), jnp.max))
                 for h in heads]
        alpha = [jnp.exp2(m_old[h] - m_new[h]) for h in heads]
        p = [jnp.exp2(s[h] - m_new[h]).astype(BF16) for h in heads]
        pv = [_dot(ctj, p[h]) for h in heads]
        for h in heads:
            acc_ref[h] = alpha[h] * acc_ref[h] + pv[h]
            m_ref[h] = m_new[h]

    n_far = jnp.maximum(i - 1, 0)

    def far_body(jj, carry):
        attend([2 * jj, 2 * jj + 1], lambda h: None)
        return carry

    lax.fori_loop(0, n_far // 2, far_body, 0)

    @pl.when(n_far % 2 == 1)
    def _():
        attend([n_far - 1], lambda h: None)

    @pl.when(i > 0)
    def _():
        attend([i - 1, i], lambda h: bias_ref[h])

    @pl.when(i == 0)
    def _():
        attend([i], lambda h: bias_ref[h, ts:, :])

    outs = [_dot(wuvt_ref[h],
                 (acc_ref[h, :A_LATENT, :] / acc_ref[h, A_LATENT:A_LATENT + 1, :]).astype(BF16))
            for h in range(A_HEADS)]
    o_ref[...] = jnp.concatenate(outs, axis=0).T.astype(o_ref.dtype)


def _t5_bucket(rel):
    nb = REL_BUCKETS // 2
    max_exact = nb // 2
    side = jnp.where(rel > 0, nb, 0)
    n = jnp.abs(rel)
    nf = jnp.maximum(n, 1).astype(jnp.float32)
    large = max_exact + (jnp.log(nf / max_exact) / math.log(REL_MAX_DIST / max_exact)
                         * (nb - max_exact)).astype(jnp.int32)
    large = jnp.minimum(large, nb - 1)
    return side + jnp.where(n < max_exact, n, large)


def _near_bias_kernel(far_ref, rb_ref, bucket_ref, o_ref):
    bucket = bucket_ref[...]
    for h in range(A_HEADS):
        acc = jnp.zeros(bucket.shape, F32)
        for b in range(REL_BUCKETS):
            acc = jnp.where(bucket == b, rb_ref[b, h], acc)
        o_ref[h] = (acc - rb_ref[far_ref[0], h]) * LOG2E


def _near_bias(rel_bias, tq, ts):
    assert ts >= REL_MAX_DIST
    rel = (jnp.arange(2 * ts)[:, None] - ts) - jnp.arange(tq)[None, :]
    far = _t5_bucket(jnp.full((1,), -(ts + 1), jnp.int32))
    smem = pl.BlockSpec(memory_space=pltpu.SMEM)
    return pl.pallas_call(
        _near_bias_kernel,
        in_specs=[smem, smem, pl.BlockSpec(memory_space=pltpu.VMEM)],
        out_specs=pl.BlockSpec(memory_space=pltpu.VMEM),
        out_shape=jax.ShapeDtypeStruct((A_HEADS, 2 * ts, tq), F32),
        name="near_bias",
    )(far, rel_bias.astype(F32), _t5_bucket(rel).astype(jnp.int32))


def _dsa(qat, qit, wit, kidx, c, ct3, w_uk, w_uv, rel_bias, batch, seq, tq, nbisect):
    ts = tq
    m = batch * seq
    nq = seq // tq
    nks = seq // ts
    topk = min(TOPK_MAX, seq // 4)
    kidx3 = kidx.reshape(m // ts, ts, IDX_DIM)
    c3 = c.reshape(m // ts, ts, A_LATENT)
    wuk = w_uk.astype(BF16)
    wuv_t = jnp.swapaxes(w_uv, 1, 2).astype(BF16)
    bias = _near_bias(rel_bias, tq, ts)
    qcol = lambda b, i: (0, b * nq + i)
    kv = lambda b, i: (b, 0, 0)
    const3 = lambda b, i: (0, 0, 0)
    kern = functools.partial(_dsa_kernel, tq=tq, ts=ts, topk=topk, nbisect=nbisect)
    return pl.pallas_call(
        kern,
        grid=(batch, nq),
        in_specs=[
            pl.BlockSpec((A_WIDTH, tq), qcol),
            pl.BlockSpec((IDX_HEADS * IDX_DIM, tq), qcol),
            pl.BlockSpec((IDX_HEADS, tq), qcol),
            pl.BlockSpec((nks, ts, IDX_DIM), kv),
            pl.BlockSpec((nks, ts, A_LATENT), kv),
            pl.BlockSpec((nks, A_LATENT, ts), kv),
            pl.BlockSpec((A_HEADS, A_LATENT, A_HEAD_DIM), const3),
            pl.BlockSpec((A_HEADS, A_HEAD_DIM, A_LATENT), const3),
            pl.BlockSpec((A_HEADS, 2 * ts, tq), const3),
        ],
        out_specs=pl.BlockSpec((tq, A_WIDTH), lambda b, i: (b * nq + i, 0)),
        out_shape=jax.ShapeDtypeStruct((m, A_WIDTH), BF16),
        scratch_shapes=[
            pltpu.VMEM((nks, ts, tq), F32),
            pltpu.VMEM((A_HEADS, A_LATENT, tq), BF16),
            pltpu.VMEM((A_HEADS, 1, tq), F32),
            pltpu.VMEM((A_HEADS, A_LATENT + DENOM_ROWS, tq), F32),
            pltpu.VMEM((1, tq), F32),
            pltpu.VMEM((2, SUBLANES, tq), F32),
        ],
        compiler_params=pltpu.CompilerParams(
            dimension_semantics=("arbitrary", "arbitrary"), vmem_limit_bytes=VMEM_LIMIT),
        name="dsa",
    )(qat, qit, wit, kidx3, c3, ct3, wuk, wuv_t, bias)


PAIR = 2 * CHUNK
PAIRS_PER_STEP = 2
HI = lax.Precision.HIGHEST


def _gdn_kernel(qkv_ref, z_ref, ab_ref, abt_ref, convw_ref, pcol_ref, prow_ref, onorm_ref, o_ref,
                u_s, w_s, qg_s, kdt_s, a_s, gl_s, st_s, stage_ref, *, seq):
    npair = seq // PAIR
    ri = lax.broadcasted_iota(jnp.int32, (PAIR, PAIR), 0)
    ci = lax.broadcasted_iota(jnp.int32, (PAIR, PAIR), 1)
    same = (ri // CHUNK) == (ci // CHUNK)
    lower_bd = jnp.where(jnp.logical_and(same, ci <= ri), 1.0, 0.0)
    upper_bd = jnp.where(jnp.logical_and(same, ri <= ci), 1.0, 0.0)
    causal_bd = jnp.logical_and(same, ci <= ri)
    diag = ci == ri
    eye = jnp.where(diag, 1.0, 0.0)
    neg_exp_a_col = -jnp.exp(pcol_ref[:, 0:1])
    dt_col = pcol_ref[:, 1:2]
    neg_exp_a_row = -jnp.exp(prow_ref[0:1, :])
    dt_row = prow_ref[1:2, :]
    convw = convw_ref[...]
    first_chunk = lax.broadcasted_iota(jnp.int32, (PAIR, 1), 0) < CHUNK

    def aligned(x, m):
        return x if isinstance(x, int) else pl.multiple_of(x, m)

    def conv_silu(i, p, col):
        slot = i * 3 * B_HEADS + col // B_HEAD_DIM
        r0 = aligned(p * PAIR, PAIR)
        cur = qkv_ref[pl.ds(r0, PAIR), col:col + B_HEAD_DIM].astype(F32)
        lead = BF16_ROWS
        prev0 = (max(r0 - lead, 0) if isinstance(r0, int)
                 else pl.multiple_of(jnp.maximum(r0 - lead, 0), lead))
        prev = qkv_ref[pl.ds(prev0, lead), col:col + B_HEAD_DIM].astype(F32)
        prev = jnp.where(p > 0, prev, 0.0)
        stage = stage_ref.at[slot]
        stage[0:lead, :] = prev
        stage[lead:lead + PAIR, :] = cur
        y = cur * convw[CONV_WIDTH - 1:CONV_WIDTH, col:col + B_HEAD_DIM]
        for back in range(1, CONV_WIDTH):
            sh = stage[lead - back:lead - back + PAIR, :]
            y = y + sh * convw[CONV_WIDTH - 1 - back:CONV_WIDTH - back, col:col + B_HEAD_DIM]
        return _silu(y)

    def l2n(x):
        return x * lax.rsqrt(jnp.sum(x * x, axis=-1, keepdims=True) + EPS)

    heads = range(B_HEADS)
    pairs = range(PAIRS_PER_STEP)
    units = [(i, h) for i in pairs for h in heads]

    def slot_of(step, i):
        return (step % 2) * PAIRS_PER_STEP + i

    def phase1(step):
        ps = [step * PAIRS_PER_STEP + i for i in pairs]
        gc_col, gc_row, beta_col = [], [], []
        for p in ps:
            ab = ab_ref[pl.ds(aligned(p * PAIR, PAIR), PAIR), :]
            g_col = neg_exp_a_row * _softplus(ab + dt_row)
            gc_col.append(_dot(lower_bd, g_col, HI))
            beta_col.append(_sigmoid(ab))
            g_row = neg_exp_a_col * _softplus(abt_ref[p] + dt_col)
            gc_row.append(_dot(g_row, upper_bd, HI))
        q = {(i, h): l2n(conv_silu(i, ps[i], h * B_HEAD_DIM)) * (B_HEAD_DIM ** -0.5) for i, h in units}
        k = {(i, h): l2n(conv_silu(i, ps[i], B_WIDTH + h * B_HEAD_DIM)) for i, h in units}
        v = {(i, h): conv_silu(i, ps[i], 2 * B_WIDTH + h * B_HEAD_DIM) for i, h in units}
        gcol = {(i, h): gc_col[i][:, h:h + 1] for i, h in units}
        beta = {(i, h): beta_col[i][:, B_HEADS + h:B_HEADS + h + 1] for i, h in units}
        kb = {u: k[u] * beta[u] for u in units}
        qk_kk = {u: _dot_nt(jnp.concatenate([q[u], kb[u]], axis=0).astype(BF16), k[u].astype(BF16))
                 for u in units}
        decay = {(i, h): jnp.exp(jnp.where(causal_bd, gcol[i, h] - gc_row[i][h:h + 1, :], -jnp.inf))
                 for i, h in units}
        intra = {u: (qk_kk[u][:PAIR] * decay[u]).astype(BF16) for u in units}
        low = {u: jnp.where(diag, 0.0, qk_kk[u][PAIR:] * decay[u]) for u in units}

        x = {u: eye - low[u] for u in units}
        low16 = {u: low[u].astype(BF16) for u in units}
        pw = {u: _dot(low16[u], low16[u]) for u in units}
        for _ in range(int(math.log2(CHUNK)) - 2):
            pw16 = {u: pw[u].astype(BF16) for u in units}
            r = {u: _dot(jnp.concatenate([x[u].astype(BF16), pw16[u]], axis=0), pw16[u]) for u in units}
            x = {u: x[u] + r[u][:PAIR] for u in units}
            pw = {u: r[u][PAIR:] for u in units}
        tinv = {u: x[u] + _dot(x[u].astype(BF16), pw[u].astype(BF16)) for u in units}

        eg = {u: jnp.exp(gcol[u]) for u in units}
        uw = {u: _dot(tinv[u].astype(BF16),
                      jnp.concatenate([v[u] * beta[u], kb[u] * eg[u]], axis=1).astype(BF16))
              for u in units}
        for i, h in units:
            u, slot = (i, h), slot_of(step, i)
            g_last = jnp.where(first_chunk, gcol[u][CHUNK - 1:CHUNK], gcol[u][PAIR - 1:PAIR])
            kd = k[u] * jnp.exp(g_last - gcol[u])
            u_s[slot, h] = uw[u][:, :B_HEAD_DIM]
            w_s[slot, h] = uw[u][:, B_HEAD_DIM:].astype(BF16)
            qg_s[slot, h] = (q[u] * eg[u]).astype(BF16)
            for cc in range(2):
                rs = slice(cc * CHUNK, (cc + 1) * CHUNK)
                kdt_s[slot, h, cc] = kd[rs].T.astype(BF16)
                a_s[slot, h, cc] = intra[u][rs, rs]
                gl_s[slot, h, cc] = jnp.broadcast_to(
                    jnp.exp(gcol[u][(cc + 1) * CHUNK - 1:(cc + 1) * CHUNK]), (SUBLANES, LANES))

    st_s[...] = jnp.zeros(st_s.shape, F32)
    onorm = onorm_ref[...]

    def phase2(step):
        for i in pairs:
            p, slot = step * PAIRS_PER_STEP + i, slot_of(step, i)
            for cc in range(2):
                rs = slice(cc * CHUNK, (cc + 1) * CHUNK)
                rows = pl.ds(aligned(p * PAIR + cc * CHUNK, CHUNK), CHUNK)
                state = [st_s[h] for h in heads]
                sb = [state[h].astype(BF16) for h in heads]
                wq = [_dot(jnp.concatenate([w_s[slot, h, rs, :], qg_s[slot, h, rs, :]], axis=0), sb[h])
                      for h in heads]
                vb = [(u_s[slot, h, rs, :] - wq[h][:CHUNK]).astype(BF16) for h in heads]
                upd = [_dot(kdt_s[slot, h, cc], vb[h]) for h in heads]
                av = [_dot(a_s[slot, h, cc], vb[h]) for h in heads]
                for h in heads:
                    st_s[h] = state[h] * gl_s[slot, h, cc][0:1, :] + upd[h]
                for h in heads:
                    o = wq[h][CHUNK:] + av[h]
                    z = z_ref[rows, h * B_HEAD_DIM:(h + 1) * B_HEAD_DIM].astype(F32)
                    o_ref[rows, h * B_HEAD_DIM:(h + 1) * B_HEAD_DIM] = (
                        _rms(o, onorm) * _silu(z)).astype(o_ref.dtype)

    def body(step, carry):
        phase2(step - 1)
        phase1(step)
        return carry

    nstep = npair // PAIRS_PER_STEP
    phase1(0)
    lax.fori_loop(1, nstep, body, 0)
    phase2(nstep - 1)


def _gdn(qkv, z, ab, abt, conv_w, a_log, dt_bias, o_norm, batch, seq):
    m = batch * seq
    npair = seq // PAIR
    assert npair % PAIRS_PER_STEP == 0
    nslot = 2 * PAIRS_PER_STEP
    abt3 = jnp.swapaxes(abt.reshape(2 * B_HEADS, m // PAIR, PAIR), 0, 1)
    zeros = jnp.zeros((B_HEADS,), F32)
    alog = jnp.concatenate([a_log.astype(F32), zeros])
    dtb = jnp.concatenate([dt_bias.astype(F32), zeros])
    pcol = jnp.stack([alog, dtb], axis=1)
    prow = jnp.stack([alog, dtb], axis=0)
    const = lambda b: (0, 0)
    rows = lambda b: (b, 0)
    return pl.pallas_call(
        functools.partial(_gdn_kernel, seq=seq),
        grid=(batch,),
        in_specs=[
            pl.BlockSpec((seq, 3 * B_WIDTH), rows),
            pl.BlockSpec((seq, B_WIDTH), rows),
            pl.BlockSpec((seq, 2 * B_HEADS), rows),
            pl.BlockSpec((npair, 2 * B_HEADS, PAIR), lambda b: (b, 0, 0)),
            pl.BlockSpec((CONV_WIDTH, 3 * B_WIDTH), const),
            pl.BlockSpec((2 * B_HEADS, 2), const),
            pl.BlockSpec((2, 2 * B_HEADS), const),
            pl.BlockSpec((1, B_HEAD_DIM), const),
        ],
        out_specs=pl.BlockSpec((seq, B_WIDTH), rows),
        out_shape=jax.ShapeDtypeStruct((m, B_WIDTH), BF16),
        scratch_shapes=[
            pltpu.VMEM((nslot, B_HEADS, PAIR, B_HEAD_DIM), F32),
            pltpu.VMEM((nslot, B_HEADS, PAIR, B_HEAD_DIM), BF16),
            pltpu.VMEM((nslot, B_HEADS, PAIR, B_HEAD_DIM), BF16),
            pltpu.VMEM((nslot, B_HEADS, 2, B_HEAD_DIM, CHUNK), BF16),
            pltpu.VMEM((nslot, B_HEADS, 2, CHUNK, CHUNK), BF16),
            pltpu.VMEM((nslot, B_HEADS, 2, SUBLANES, LANES), F32),
            pltpu.VMEM((B_HEADS, B_HEAD_DIM, B_HEAD_DIM), F32),
            pltpu.VMEM((PAIRS_PER_STEP * 3 * B_HEADS, BF16_ROWS + PAIR, B_HEAD_DIM), F32),
        ],
        compiler_params=pltpu.CompilerParams(
            dimension_semantics=("arbitrary",), vmem_limit_bytes=VMEM_LIMIT),
        name="gdn",
    )(qkv, z, ab, abt3, conv_w.astype(F32), pcol, prow, o_norm.reshape(1, B_HEAD_DIM).astype(F32))


def _out_mlp_kernel(x_ref, oa_ref, ob_ref, wo_ref, g_post_mix_ref, g_pre_mlp_ref, w1_ref, w2_ref,
                    g_post_mlp_ref, o_ref, *, ff_chunk):
    mix = _dot(oa_ref[...], wo_ref[:A_WIDTH, :]) + _dot(ob_ref[...], wo_ref[A_WIDTH:, :])
    x1 = x_ref[...] + _rms(mix, g_post_mix_ref[...])
    h = _rms(x1, g_pre_mlp_ref[...]).astype(BF16)
    d_ff = w1_ref.shape[1]
    y = jnp.zeros(x1.shape, F32)
    for lo in range(0, d_ff, ff_chunk):
        a = jnp.maximum(_dot(h, w1_ref[:, lo:lo + ff_chunk]), 0.0)
        y = y + _dot((a * a).astype(BF16), w2_ref[lo:lo + ff_chunk, :])
    o_ref[...] = x1 + _rms(y, g_post_mlp_ref[...])


def _out_mlp(x2, oa, ob, w_out, g_post_mix, g_pre_mlp, w1, w2, g_post_mlp, tm, ff_chunk):
    m, d = x2.shape
    d_ff = w1.shape[1]
    const = lambda i: (0, 0)
    row = lambda i: (i, 0)
    once = dict(pipeline_mode=pl.Buffered(1))
    vec = lambda g: g.reshape(1, d).astype(F32)
    return pl.pallas_call(
        functools.partial(_out_mlp_kernel, ff_chunk=ff_chunk),
        grid=(m // tm,),
        in_specs=[
            pl.BlockSpec((tm, d), row),
            pl.BlockSpec((tm, A_WIDTH), row),
            pl.BlockSpec((tm, B_WIDTH), row),
            pl.BlockSpec((A_WIDTH + B_WIDTH, d), const, **once),
            pl.BlockSpec((1, d), const),
            pl.BlockSpec((1, d), const),
            pl.BlockSpec((d, d_ff), const, **once),
            pl.BlockSpec((d_ff, d), const, **once),
            pl.BlockSpec((1, d), const),
        ],
        out_specs=pl.BlockSpec((tm, d), row),
        out_shape=jax.ShapeDtypeStruct((m, d), F32),
        compiler_params=pltpu.CompilerParams(
            dimension_semantics=("arbitrary",), vmem_limit_bytes=VMEM_LIMIT),
        name="out_mlp",
    )(x2, oa, ob, w_out.astype(BF16), vec(g_post_mix), vec(g_pre_mlp),
      w1.astype(BF16), w2.astype(BF16), vec(g_post_mlp))


def kernel(x, w_in, c_norm, w_uk, w_uv, rel_bias, conv_w, a_log, dt_bias, o_norm, w_out,
           pre_norm_mix, post_norm_mix, pre_norm_mlp, post_norm_mlp, w_mlp_in, w_mlp_out):
    batch, seq, d = x.shape
    m = batch * seq
    tm = min(ROW_TILE, m)
    tm_in = min(IN_ROW_TILE, m)
    tq = min(QUERY_TILE, seq)
    assert m % tm == 0 and m % tm_in == 0 and seq % tq == 0 and seq % (PAIR * PAIRS_PER_STEP) == 0
    xs = x.reshape(m, d)
    for l in range(w_in.shape[0]):
        qat, qit, qkv, z, c, ct3, kidx, wit, ab, abt = _in_proj(
            xs, pre_norm_mix[l], w_in[l], c_norm[l], tm_in, tq)
        oa = _dsa(qat, qit, wit, kidx, c, ct3, w_uk[l], w_uv[l], rel_bias, batch, seq, tq, N_BISECT)
        ob = _gdn(qkv, z, ab, abt, conv_w[l], a_log[l], dt_bias[l], o_norm[l], batch, seq)
        xs = _out_mlp(xs, oa, ob, w_out[l], post_norm_mix[l], pre_norm_mlp[l],
                      w_mlp_in[l], w_mlp_out[l], post_norm_mlp[l], tm, FF_CHUNK)
    return xs.reshape(batch, seq, d)
```

```python
import functools
import math

import numpy as np
import jax
import jax.numpy as jnp
from jax import lax
from jax.experimental import pallas as pl
from jax.experimental.pallas import tpu as pltpu

F32 = jnp.float32
BF16 = jnp.bfloat16

EPS = 1e-6
CHUNK = 64
A_HEADS = 8
A_HEAD_DIM = 64
A_WIDTH = A_HEADS * A_HEAD_DIM
A_LATENT = 128
IDX_HEADS = 8
IDX_DIM = 64
TOPK_MAX = 256
REL_BUCKETS = 32
REL_MAX_DIST = 128
B_HEADS = 4
B_HEAD_DIM = 128
B_WIDTH = B_HEADS * B_HEAD_DIM
CONV_WIDTH = 4
IN_SPLITS = (A_WIDTH, A_LATENT, IDX_HEADS * IDX_DIM, IDX_DIM, IDX_HEADS,
             B_WIDTH, B_WIDTH, B_WIDTH, B_HEADS, B_HEADS, B_WIDTH)

LANES = 128
SUBLANES = 8
BF16_ROWS = 16
VMEM_LIMIT = 56 * 1024 * 1024
ROW_TILE = 1024
IN_ROW_TILE = 1024
QUERY_TILE = 256
FF_CHUNK = 1024
N_BISECT = 14
NEG_BIG = -1e30
LOG2E = 1.0 / math.log(2.0)
DENOM_ROWS = 16
FINISH_UNROLL = 3

NT_DIMS = (((1,), (1,)), ((), ()))


def _dot(a, b, precision=None):
    return jnp.dot(a, b, preferred_element_type=F32, precision=precision)


def _dot_nt(a, b, precision=None):
    return lax.dot_general(a, b, NT_DIMS, preferred_element_type=F32, precision=precision)


def _rms(x, g):
    return x * lax.rsqrt(jnp.mean(x * x, axis=-1, keepdims=True) + EPS) * g


def _sigmoid(x):
    return 1.0 / (1.0 + jnp.exp(-x))


def _silu(x):
    half = 0.5 * x
    return half + half * jnp.tanh(half)


def _softplus(x):
    return jnp.maximum(x, 0.0) + jnp.log1p(jnp.exp(-jnp.abs(x)))


_C_QKV = 0
_C_Z = _C_QKV + 3 * B_WIDTH
_C_CKV = _C_Z + B_WIDTH
_C_SMALL = _C_CKV + A_LATENT
_C_END = _C_SMALL + LANES
_R_QA = 0
_R_QI = _R_QA + A_WIDTH
_R_CKV = _R_QI + IDX_HEADS * IDX_DIM
_R_SMALL = _R_CKV + A_LATENT
_R_END = _R_SMALL + IDX_HEADS + 2 * B_HEADS


def _in_proj_kernel(x_ref, g_ref, w_ref, wt_ref, cn_ref, cnt_ref,
                    qat_ref, qit_ref, qkv_ref, z_ref, c_ref, ct_ref, kidx_ref, wit_ref, ab_ref, abt_ref,
                    *, ts):
    h = _rms(x_ref[...], g_ref[...]).astype(BF16)

    def proj(lo, hi):
        return _dot(h, w_ref[:, lo:hi])

    def proj_t(lo, hi):
        return _dot_nt(wt_ref[lo:hi, :], h)

    qkv_ref[...] = proj(_C_QKV, _C_Z).astype(BF16)
    z_ref[...] = proj(_C_Z, _C_CKV).astype(BF16)
    c_ref[...] = _rms(proj(_C_CKV, _C_SMALL), cn_ref[...]).astype(BF16)
    small = proj(_C_SMALL, _C_END)
    kidx_ref[...] = small[:, :IDX_DIM].astype(BF16)
    ab_ref[...] = small[:, IDX_DIM:IDX_DIM + 2 * B_HEADS]

    qat_ref[...] = proj_t(_R_QA, _R_QI).astype(BF16)
    qit_ref[...] = (proj_t(_R_QI, _R_CKV) * (IDX_DIM ** -0.5)).astype(BF16)
    ckv_t = proj_t(_R_CKV, _R_SMALL)
    c_t = (ckv_t * lax.rsqrt(jnp.mean(ckv_t * ckv_t, axis=0, keepdims=True) + EPS)
           * cnt_ref[...]).astype(BF16)
    for r in range(ct_ref.shape[0]):
        ct_ref[r] = c_t[:, r * ts:(r + 1) * ts]
    small_t = proj_t(_R_SMALL, _R_END)
    wit_ref[...] = small_t[:IDX_HEADS] * (IDX_HEADS ** -0.5)
    abt_ref[...] = small_t[IDX_HEADS:]


def _in_proj(x2, g, w_in, c_norm, tm, ts):
    m, d = x2.shape
    offs = np.cumsum((0,) + IN_SPLITS)
    seg = [w_in[:, offs[i]:offs[i + 1]] for i in range(len(IN_SPLITS))]
    q_a, c_kv, q_idx, k_idx, w_idx, q_b, k_b, v_b, a_b, b_b, z_b = seg
    pad = jnp.zeros((d, LANES - IDX_DIM - 2 * B_HEADS), w_in.dtype)
    w_all = jnp.concatenate([q_b, k_b, v_b, z_b, c_kv, k_idx, a_b, b_b, pad], axis=1).astype(BF16)
    w_t = jnp.concatenate([q_a, q_idx, c_kv, w_idx, a_b, b_b], axis=1).T.astype(BF16)
    const = lambda i: (0, 0)
    row = lambda i: (i, 0)
    col = lambda i: (0, i)
    out_shape = (
        jax.ShapeDtypeStruct((A_WIDTH, m), BF16),
        jax.ShapeDtypeStruct((IDX_HEADS * IDX_DIM, m), BF16),
        jax.ShapeDtypeStruct((m, 3 * B_WIDTH), BF16),
        jax.ShapeDtypeStruct((m, B_WIDTH), BF16),
        jax.ShapeDtypeStruct((m, A_LATENT), BF16),
        jax.ShapeDtypeStruct((m // ts, A_LATENT, ts), BF16),
        jax.ShapeDtypeStruct((m, IDX_DIM), BF16),
        jax.ShapeDtypeStruct((IDX_HEADS, m), F32),
        jax.ShapeDtypeStruct((m, 2 * B_HEADS), F32),
        jax.ShapeDtypeStruct((2 * B_HEADS, m), F32),
    )
    out_specs = (
        pl.BlockSpec((A_WIDTH, tm), col),
        pl.BlockSpec((IDX_HEADS * IDX_DIM, tm), col),
        pl.BlockSpec((tm, 3 * B_WIDTH), row),
        pl.BlockSpec((tm, B_WIDTH), row),
        pl.BlockSpec((tm, A_LATENT), row),
        pl.BlockSpec((tm // ts, A_LATENT, ts), lambda i: (i, 0, 0)),
        pl.BlockSpec((tm, IDX_DIM), row),
        pl.BlockSpec((IDX_HEADS, tm), col),
        pl.BlockSpec((tm, 2 * B_HEADS), row),
        pl.BlockSpec((2 * B_HEADS, tm), col),
    )
    return pl.pallas_call(
        functools.partial(_in_proj_kernel, ts=ts),
        grid=(m // tm,),
        in_specs=[
            pl.BlockSpec((tm, d), row),
            pl.BlockSpec((1, d), const),
            pl.BlockSpec((d, _C_END), const),
            pl.BlockSpec((_R_END, d), const),
            pl.BlockSpec((1, A_LATENT), const),
            pl.BlockSpec((A_LATENT, 1), const),
        ],
        out_specs=out_specs,
        out_shape=out_shape,
        compiler_params=pltpu.CompilerParams(
            dimension_semantics=("arbitrary",), vmem_limit_bytes=VMEM_LIMIT),
        name="in_proj",
    )(x2, g.reshape(1, d), w_all, w_t, c_norm.reshape(1, A_LATENT), c_norm.reshape(A_LATENT, 1))


def _fold_rows(x, op):
    r = x.shape[0]
    while r > SUBLANES:
        r //= 2
        x = op(x[:r], x[r:])
    return x


def _dsa_kernel(qat_ref, qit_ref, wit_ref, kidx_ref, c_ref, ct_ref, wuk_ref, wuvt_ref, bias_ref, o_ref,
                sc_ref, qabs_ref, m_ref, acc_ref, tie_base_ref, range_ref, *, tq, ts, topk, nbisect):
    i = pl.program_id(1)
    nkb = i + 1
    q_pos = i * tq + lax.broadcasted_iota(jnp.int32, (1, tq), 1)
    limit = (q_pos // CHUNK + 1) * CHUNK
    kf = float(topk)
    ninf = -jnp.inf

    def key_ids(j):
        return j * ts + lax.broadcasted_iota(jnp.int32, (ts, 1), 0)

    def col_reduce(x8, op):
        return op(x8, axis=0, keepdims=True)

    wit = wit_ref[...]

    range_ref[0] = jnp.full((SUBLANES, tq), jnp.inf, F32)
    range_ref[1] = jnp.full((SUBLANES, tq), ninf, F32)

    def score_blocks(js, diagonal):
        k = jnp.concatenate([kidx_ref[j] for j in js], axis=0)
        r = [_dot(k, qit_ref[h * IDX_DIM:(h + 1) * IDX_DIM, :]) for h in range(IDX_HEADS)]
        s = jnp.maximum(r[0], 0.0) * wit[0:1, :]
        for h in range(1, IDX_HEADS):
            s = s + jnp.maximum(r[h], 0.0) * wit[h:h + 1, :]
        lo, hi = s, s
        if diagonal:
            adm = key_ids(js[0]) < limit
            s = jnp.where(adm, s, ninf)
            lo, hi = jnp.where(adm, s, jnp.inf), s
        for n, j in enumerate(js):
            sc_ref[j] = s[n * ts:(n + 1) * ts]
        range_ref[0] = jnp.minimum(range_ref[0], _fold_rows(lo, jnp.minimum))
        range_ref[1] = jnp.maximum(range_ref[1], _fold_rows(hi, jnp.maximum))

    def score_pair(jj, carry):
        score_blocks([2 * jj, 2 * jj + 1], False)
        return carry

    lax.fori_loop(0, i // 2, score_pair, 0)

    @pl.when(i % 2 == 1)
    def _():
        score_blocks([i - 1], False)

    score_blocks([i], True)
    rmin = col_reduce(range_ref[0], jnp.min)
    rmax = col_reduce(range_ref[1], jnp.max)

    def count(pred):
        def body(j, acc):
            return acc + _fold_rows(jnp.where(pred(sc_ref[j], j), 1.0, 0.0), jnp.add)
        acc = lax.fori_loop(0, nkb, body, jnp.zeros((SUBLANES, tq), F32))
        return col_reduce(acc, jnp.sum)

    def count_ge(v):
        return count(lambda s, j: s >= v)

    need = limit > topk

    def bisect_body(_, carry):
        lo, hi, n_hi = carry
        mid = lo + 0.5 * (hi - lo)
        c = count_ge(mid)
        ge = c >= kf
        return jnp.where(ge, mid, lo), jnp.where(ge, hi, mid), jnp.where(ge, n_hi, c)

    hi0 = rmax + jnp.abs(rmax) * 1e-6 + 1e-30
    _, hi, n_hi = lax.fori_loop(0, nbisect, bisect_body, (rmin, hi0, jnp.zeros((1, tq), F32)))

    def max_below(v):
        def body(j, acc):
            s = sc_ref[j]
            return jnp.maximum(acc, _fold_rows(jnp.where(s < v, s, ninf), jnp.maximum))
        return col_reduce(lax.fori_loop(0, nkb, body, jnp.full((SUBLANES, tq), ninf, F32)), jnp.max)

    def finish_cond(state):
        return jnp.min(state[-1]) < 0.5

    def finish_body(state):
        m, n_hi, thr, n_thr, done = state

        def body(j, carry):
            cnt, nxt = carry
            s = sc_ref[j]
            below = s < m
            return (cnt + _fold_rows(jnp.where(below, 0.0, 1.0), jnp.add),
                    jnp.maximum(nxt, _fold_rows(jnp.where(below, s, ninf), jnp.maximum)))

        cnt, nxt = lax.fori_loop(0, nkb, body, (jnp.zeros((SUBLANES, tq), F32),
                                               jnp.full((SUBLANES, tq), ninf, F32)))
        c = col_reduce(cnt, jnp.sum)
        open_ = done < 0.5
        hit = jnp.logical_and(open_, c >= kf)
        miss = jnp.logical_and(open_, c < kf)
        thr = jnp.where(hit, m, thr)
        n_thr = jnp.where(hit, c, n_thr)
        n_hi = jnp.where(miss, c, n_hi)
        m = jnp.where(miss, col_reduce(nxt, jnp.max), m)
        return m, n_hi, thr, n_thr, jnp.where(hit, 1.0, done)

    lowest = float(jnp.finfo(F32).min)
    zero = jnp.zeros((1, tq), F32)
    state = (max_below(hi), n_hi, jnp.full((1, tq), lowest, F32), zero, jnp.where(need, 0.0, 1.0))
    for _ in range(FINISH_UNROLL):
        state = finish_body(state)
    _, n_hi, thr, n_thr, _ = lax.while_loop(finish_cond, finish_body, state)

    n_tie = jnp.where(need, kf - n_hi, 0.0)
    surplus = jnp.max(jnp.where(jnp.logical_and(need, n_thr - n_hi > n_tie), 1.0, 0.0)) > 0.5
    tie_base_ref[...] = zero
    kr = lax.broadcasted_iota(jnp.int32, (ts, ts), 0)
    kc = lax.broadcasted_iota(jnp.int32, (ts, ts), 1)

    def mask_bias(j):
        def plain(_):
            return jnp.where(sc_ref[j] >= thr, 0.0, NEG_BIG)

        def with_ties(_):
            s = sc_ref[j]
            tie = s == thr
            prefix_ones = jnp.where(kc <= kr, 1.0, 0.0).astype(BF16)
            rank = _dot(prefix_ones, jnp.where(tie, 1.0, 0.0).astype(BF16)) + tie_base_ref[...]
            tie_base_ref[...] = rank[ts - 1:ts, :]
            keep = jnp.where(tie, jnp.where(rank <= n_tie, 0.0, NEG_BIG), NEG_BIG)
            return jnp.where(s > thr, 0.0, keep)

        return lax.cond(surplus, with_ties, plain, 0)

    for h in range(A_HEADS):
        qh = _dot(wuk_ref[h], qat_ref[h * A_HEAD_DIM:(h + 1) * A_HEAD_DIM, :])
        qabs_ref[h] = (qh * (A_HEAD_DIM ** -0.5 * LOG2E)).astype(BF16)
    m_ref[...] = jnp.full(m_ref.shape, NEG_BIG, F32)
    acc_ref[...] = jnp.zeros(acc_ref.shape, F32)
    heads = range(A_HEADS)

    def attend(js, bias_of_head):
        n = len(js) * ts
        cj = jnp.concatenate([c_ref[j] for j in js], axis=0)
        ctj = jnp.concatenate([jnp.concatenate([ct_ref[j] for j in js], axis=1),
                               jnp.ones((DENOM_ROWS, n), BF16)], axis=0)
        mb = jnp.concatenate([mask_bias(j) for j in js], axis=0)
        s = [_dot(cj, qabs_ref[h]) for h in heads]
        for h in heads:
            b = bias_of_head(h)
            s[h] = s[h] + (mb if b is None else mb + b)
        m_old = [m_ref[h] for h in heads]
        m_new = [jnp.maximum(m_old[h], col_reduce(_fold_rows(s[h], jnp.maximum), jnp.max))
                 for h in heads]
        alpha = [jnp.exp2(m_old[h] - m_new[h]) for h in heads]
        p = [jnp.exp2(s[h] - m_new[h]).astype(BF16) for h in heads]
        pv = [_dot(ctj, p[h]) for h in heads]
        for h in heads:
            acc_ref[h] = alpha[h] * acc_ref[h] + pv[h]
            m_ref[h] = m_new[h]

    n_far = jnp.maximum(i - 1, 0)

    def far_body(jj, carry):
        attend([2 * jj, 2 * jj + 1], lambda h: None)
        return carry

    lax.fori_loop(0, n_far // 2, far_body, 0)

    @pl.when(n_far % 2 == 1)
    def _():
        attend([n_far - 1], lambda h: None)

    @pl.when(i > 0)
    def _():
        attend([i - 1, i], lambda h: bias_ref[h])

    @pl.when(i == 0)
    def _():
        attend([i], lambda h: bias_ref[h, ts:, :])

    outs = [_dot(wuvt_ref[h],
                 (acc_ref[h, :A_LATENT, :] / acc_ref[h, A_LATENT:A_LATENT + 1, :]).astype(BF16))
            for h in range(A_HEADS)]
    o_ref[...] = jnp.concatenate(outs, axis=0).T.astype(o_ref.dtype)


def _t5_bucket(rel):
    nb = REL_BUCKETS // 2
    max_exact = nb // 2
    side = jnp.where(rel > 0, nb, 0)
    n = jnp.abs(rel)
    nf = jnp.maximum(n, 1).astype(jnp.float32)
    large = max_exact + (jnp.log(nf / max_exact) / math.log(REL_MAX_DIST / max_exact)
                         * (nb - max_exact)).astype(jnp.int32)
    large = jnp.minimum(large, nb - 1)
    return side + jnp.where(n < max_exact, n, large)


def _near_bias_kernel(far_ref, rb_ref, bucket_ref, o_ref):
    bucket = bucket_ref[...]
    for h in range(A_HEADS):
        acc = jnp.zeros(bucket.shape, F32)
        for b in range(REL_BUCKETS):
            acc = jnp.where(bucket == b, rb_ref[b, h], acc)
        o_ref[h] = (acc - rb_ref[far_ref[0], h]) * LOG2E


def _near_bias(rel_bias, tq, ts):
    assert ts >= REL_MAX_DIST
    rel = (jnp.arange(2 * ts)[:, None] - ts) - jnp.arange(tq)[None, :]
    far = _t5_bucket(jnp.full((1,), -(ts + 1), jnp.int32))
    smem = pl.BlockSpec(memory_space=pltpu.SMEM)
    return pl.pallas_call(
        _near_bias_kernel,
        in_specs=[smem, smem, pl.BlockSpec(memory_space=pltpu.VMEM)],
        out_specs=pl.BlockSpec(memory_space=pltpu.VMEM),
        out_shape=jax.ShapeDtypeStruct((A_HEADS, 2 * ts, tq), F32),
        name="near_bias",
    )(far, rel_bias.astype(F32), _t5_bucket(rel).astype(jnp.int32))


def _dsa(qat, qit, wit, kidx, c, ct3, w_uk, w_uv, rel_bias, batch, seq, tq, nbisect):
    ts = tq
    m = batch * seq
    nq = seq // tq
    nks = seq // ts
    topk = min(TOPK_MAX, seq // 4)
    kidx3 = kidx.reshape(m // ts, ts, IDX_DIM)
    c3 = c.reshape(m // ts, ts, A_LATENT)
    wuk = w_uk.astype(BF16)
    wuv_t = jnp.swapaxes(w_uv, 1, 2).astype(BF16)
    bias = _near_bias(rel_bias, tq, ts)
    qcol = lambda b, i: (0, b * nq + i)
    kv = lambda b, i: (b, 0, 0)
    const3 = lambda b, i: (0, 0, 0)
    kern = functools.partial(_dsa_kernel, tq=tq, ts=ts, topk=topk, nbisect=nbisect)
    return pl.pallas_call(
        kern,
        grid=(batch, nq),
        in_specs=[
            pl.BlockSpec((A_WIDTH, tq), qcol),
            pl.BlockSpec((IDX_HEADS * IDX_DIM, tq), qcol),
            pl.BlockSpec((IDX_HEADS, tq), qcol),
            pl.BlockSpec((nks, ts, IDX_DIM), kv),
            pl.BlockSpec((nks, ts, A_LATENT), kv),
            pl.BlockSpec((nks, A_LATENT, ts), kv),
            pl.BlockSpec((A_HEADS, A_LATENT, A_HEAD_DIM), const3),
            pl.BlockSpec((A_HEADS, A_HEAD_DIM, A_LATENT), const3),
            pl.BlockSpec((A_HEADS, 2 * ts, tq), const3),
        ],
        out_specs=pl.BlockSpec((tq, A_WIDTH), lambda b, i: (b * nq + i, 0)),
        out_shape=jax.ShapeDtypeStruct((m, A_WIDTH), BF16),
        scratch_shapes=[
            pltpu.VMEM((nks, ts, tq), F32),
            pltpu.VMEM((A_HEADS, A_LATENT, tq), BF16),
            pltpu.VMEM((A_HEADS, 1, tq), F32),
            pltpu.VMEM((A_HEADS, A_LATENT + DENOM_ROWS, tq), F32),
            pltpu.VMEM((1, tq), F32),
            pltpu.VMEM((2, SUBLANES, tq), F32),
        ],
        compiler_params=pltpu.CompilerParams(
            dimension_semantics=("arbitrary", "arbitrary"), vmem_limit_bytes=VMEM_LIMIT),
        name="dsa",
    )(qat, qit, wit, kidx3, c3, ct3, wuk, wuv_t, bias)


PAIR = 2 * CHUNK
PAIRS_PER_STEP = 2
HI = lax.Precision.HIGHEST


def _gdn_kernel(qkv_ref, z_ref, ab_ref, abt_ref, convw_ref, pcol_ref, prow_ref, onorm_ref, o_ref,
                u_s, w_s, qg_s, kdt_s, a_s, gl_s, st_s, stage_ref, *, seq):
    npair = seq // PAIR
    ri = lax.broadcasted_iota(jnp.int32, (PAIR, PAIR), 0)
    ci = lax.broadcasted_iota(jnp.int32, (PAIR, PAIR), 1)
    same = (ri // CHUNK) == (ci // CHUNK)
    lower_bd = jnp.where(jnp.logical_and(same, ci <= ri), 1.0, 0.0)
    upper_bd = jnp.where(jnp.logical_and(same, ri <= ci), 1.0, 0.0)
    causal_bd = jnp.logical_and(same, ci <= ri)
    diag = ci == ri
    eye = jnp.where(diag, 1.0, 0.0)
    neg_exp_a_col = -jnp.exp(pcol_ref[:, 0:1])
    dt_col = pcol_ref[:, 1:2]
    neg_exp_a_row = -jnp.exp(prow_ref[0:1, :])
    dt_row = prow_ref[1:2, :]
    convw = convw_ref[...]
    first_chunk = lax.broadcasted_iota(jnp.int32, (PAIR, 1), 0) < CHUNK

    def aligned(x, m):
        return x if isinstance(x, int) else pl.multiple_of(x, m)

    def conv_silu(i, p, col):
        slot = i * 3 * B_HEADS + col // B_HEAD_DIM
        r0 = aligned(p * PAIR, PAIR)
        cur = qkv_ref[pl.ds(r0, PAIR), col:col + B_HEAD_DIM].astype(F32)
        lead = BF16_ROWS
        prev0 = (max(r0 - lead, 0) if isinstance(r0, int)
                 else pl.multiple_of(jnp.maximum(r0 - lead, 0), lead))
        prev = qkv_ref[pl.ds(prev0, lead), col:col + B_HEAD_DIM].astype(F32)
        prev = jnp.where(p > 0, prev, 0.0)
        stage = stage_ref.at[slot]
        stage[0:lead, :] = prev
        stage[lead:lead + PAIR, :] = cur
        y = cur * convw[CONV_WIDTH - 1:CONV_WIDTH, col:col + B_HEAD_DIM]
        for back in range(1, CONV_WIDTH):
            sh = stage[lead - back:lead - back + PAIR, :]
            y = y + sh * convw[CONV_WIDTH - 1 - back:CONV_WIDTH - back, col:col + B_HEAD_DIM]
        return _silu(y)

    def l2n(x):
        return x * lax.rsqrt(jnp.sum(x * x, axis=-1, keepdims=True) + EPS)

    heads = range(B_HEADS)
    pairs = range(PAIRS_PER_STEP)
    units = [(i, h) for i in pairs for h in heads]

    def slot_of(step, i):
        return (step % 2) * PAIRS_PER_STEP + i

    def phase1(step):
        ps = [step * PAIRS_PER_STEP + i for i in pairs]
        gc_col, gc_row, beta_col = [], [], []
        for p in ps:
            ab = ab_ref[pl.ds(aligned(p * PAIR, PAIR), PAIR), :]
            g_col = neg_exp_a_row * _softplus(ab + dt_row)
            gc_col.append(_dot(lower_bd, g_col, HI))
            beta_col.append(_sigmoid(ab))
            g_row = neg_exp_a_col * _softplus(abt_ref[p] + dt_col)
            gc_row.append(_dot(g_row, upper_bd, HI))
        q = {(i, h): l2n(conv_silu(i, ps[i], h * B_HEAD_DIM)) * (B_HEAD_DIM ** -0.5) for i, h in units}
        k = {(i, h): l2n(conv_silu(i, ps[i], B_WIDTH + h * B_HEAD_DIM)) for i, h in units}
        v = {(i, h): conv_silu(i, ps[i], 2 * B_WIDTH + h * B_HEAD_DIM) for i, h in units}
        gcol = {(i, h): gc_col[i][:, h:h + 1] for i, h in units}
        beta = {(i, h): beta_col[i][:, B_HEADS + h:B_HEADS + h + 1] for i, h in units}
        kb = {u: k[u] * beta[u] for u in units}
        qk_kk = {u: _dot_nt(jnp.concatenate([q[u], kb[u]], axis=0).astype(BF16), k[u].astype(BF16))
                 for u in units}
        decay = {(i, h): jnp.exp(jnp.where(causal_bd, gcol[i, h] - gc_row[i][h:h + 1, :], -jnp.inf))
                 for i, h in units}
        intra = {u: (qk_kk[u][:PAIR] * decay[u]).astype(BF16) for u in units}
        low = {u: jnp.where(diag, 0.0, qk_kk[u][PAIR:] * decay[u]) for u in units}

        x = {u: eye - low[u] for u in units}
        low16 = {u: low[u].astype(BF16) for u in units}
        pw = {u: _dot(low16[u], low16[u]) for u in units}
        for _ in range(int(math.log2(CHUNK)) - 2):
            pw16 = {u: pw[u].astype(BF16) for u in units}
            r = {u: _dot(jnp.concatenate([x[u].astype(BF16), pw16[u]], axis=0), pw16[u]) for u in units}
            x = {u: x[u] + r[u][:PAIR] for u in units}
            pw = {u: r[u][PAIR:] for u in units}
        tinv = {u: x[u] + _dot(x[u].astype(BF16), pw[u].astype(BF16)) for u in units}

        eg = {u: jnp.exp(gcol[u]) for u in units}
        uw = {u: _dot(tinv[u].astype(BF16),
                      jnp.concatenate([v[u] * beta[u], kb[u] * eg[u]], axis=1).astype(BF16))
              for u in units}
        for i, h in units:
            u, slot = (i, h), slot_of(step, i)
            g_last = jnp.where(first_chunk, gcol[u][CHUNK - 1:CHUNK], gcol[u][PAIR - 1:PAIR])
            kd = k[u] * jnp.exp(g_last - gcol[u])
            u_s[slot, h] = uw[u][:, :B_HEAD_DIM]
            w_s[slot, h] = uw[u][:, B_HEAD_DIM:].astype(BF16)
            qg_s[slot, h] = (q[u] * eg[u]).astype(BF16)
            for cc in range(2):
                rs = slice(cc * CHUNK, (cc + 1) * CHUNK)
                kdt_s[slot, h, cc] = kd[rs].T.astype(BF16)
                a_s[slot, h, cc] = intra[u][rs, rs]
                gl_s[slot, h, cc] = jnp.broadcast_to(
                    jnp.exp(gcol[u][(cc + 1) * CHUNK - 1:(cc + 1) * CHUNK]), (SUBLANES, LANES))

    st_s[...] = jnp.zeros(st_s.shape, F32)
    onorm = onorm_ref[...]

    def phase2(step):
        for i in pairs:
            p, slot = step * PAIRS_PER_STEP + i, slot_of(step, i)
            for cc in range(2):
                rs = slice(cc * CHUNK, (cc + 1) * CHUNK)
                rows = pl.ds(aligned(p * PAIR + cc * CHUNK, CHUNK), CHUNK)
                state = [st_s[h] for h in heads]
                sb = [state[h].astype(BF16) for h in heads]
                wq = [_dot(jnp.concatenate([w_s[slot, h, rs, :], qg_s[slot, h, rs, :]], axis=0), sb[h])
                      for h in heads]
                vb = [(u_s[slot, h, rs, :] - wq[h][:CHUNK]).astype(BF16) for h in heads]
                upd = [_dot(kdt_s[slot, h, cc], vb[h]) for h in heads]
                av = [_dot(a_s[slot, h, cc], vb[h]) for h in heads]
                for h in heads:
                    st_s[h] = state[h] * gl_s[slot, h, cc][0:1, :] + upd[h]
                for h in heads:
                    o = wq[h][CHUNK:] + av[h]
                    z = z_ref[rows, h * B_HEAD_DIM:(h + 1) * B_HEAD_DIM].astype(F32)
                    o_ref[rows, h * B_HEAD_DIM:(h + 1) * B_HEAD_DIM] = (
                        _rms(o, onorm) * _silu(z)).astype(o_ref.dtype)

    def body(step, carry):
        phase2(step - 1)
        phase1(step)
        return carry

    nstep = npair // PAIRS_PER_STEP
    phase1(0)
    lax.fori_loop(1, nstep, body, 0)
    phase2(nstep - 1)


def _gdn(qkv, z, ab, abt, conv_w, a_log, dt_bias, o_norm, batch, seq):
    m = batch * seq
    npair = seq // PAIR
    assert npair % PAIRS_PER_STEP == 0
    nslot = 2 * PAIRS_PER_STEP
    abt3 = jnp.swapaxes(abt.reshape(2 * B_HEADS, m // PAIR, PAIR), 0, 1)
    zeros = jnp.zeros((B_HEADS,), F32)
    alog = jnp.concatenate([a_log.astype(F32), zeros])
    dtb = jnp.concatenate([dt_bias.astype(F32), zeros])
    pcol = jnp.stack([alog, dtb], axis=1)
    prow = jnp.stack([alog, dtb], axis=0)
    const = lambda b: (0, 0)
    rows = lambda b: (b, 0)
    return pl.pallas_call(
        functools.partial(_gdn_kernel, seq=seq),
        grid=(batch,),
        in_specs=[
            pl.BlockSpec((seq, 3 * B_WIDTH), rows),
            pl.BlockSpec((seq, B_WIDTH), rows),
            pl.BlockSpec((seq, 2 * B_HEADS), rows),
            pl.BlockSpec((npair, 2 * B_HEADS, PAIR), lambda b: (b, 0, 0)),
            pl.BlockSpec((CONV_WIDTH, 3 * B_WIDTH), const),
            pl.BlockSpec((2 * B_HEADS, 2), const),
            pl.BlockSpec((2, 2 * B_HEADS), const),
            pl.BlockSpec((1, B_HEAD_DIM), const),
        ],
        out_specs=pl.BlockSpec((seq, B_WIDTH), rows),
        out_shape=jax.ShapeDtypeStruct((m, B_WIDTH), BF16),
        scratch_shapes=[
            pltpu.VMEM((nslot, B_HEADS, PAIR, B_HEAD_DIM), F32),
            pltpu.VMEM((nslot, B_HEADS, PAIR, B_HEAD_DIM), BF16),
            pltpu.VMEM((nslot, B_HEADS, PAIR, B_HEAD_DIM), BF16),
            pltpu.VMEM((nslot, B_HEADS, 2, B_HEAD_DIM, CHUNK), BF16),
            pltpu.VMEM((nslot, B_HEADS, 2, CHUNK, CHUNK), BF16),
            pltpu.VMEM((nslot, B_HEADS, 2, SUBLANES, LANES), F32),
            pltpu.VMEM((B_HEADS, B_HEAD_DIM, B_HEAD_DIM), F32),
            pltpu.VMEM((PAIRS_PER_STEP * 3 * B_HEADS, BF16_ROWS + PAIR, B_HEAD_DIM), F32),
        ],
        compiler_params=pltpu.CompilerParams(
            dimension_semantics=("arbitrary",), vmem_limit_bytes=VMEM_LIMIT),
        name="gdn",
    )(qkv, z, ab, abt3, conv_w.astype(F32), pcol, prow, o_norm.reshape(1, B_HEAD_DIM).astype(F32))


def _out_mlp_kernel(x_ref, oa_ref, ob_ref, wo_ref, g_post_mix_ref, g_pre_mlp_ref, w1_ref, w2_ref,
                    g_post_mlp_ref, o_ref, *, ff_chunk):
    mix = _dot(oa_ref[...], wo_ref[:A_WIDTH, :]) + _dot(ob_ref[...], wo_ref[A_WIDTH:, :])
    x1 = x_ref[...] + _rms(mix, g_post_mix_ref[...])
    h = _rms(x1, g_pre_mlp_ref[...]).astype(BF16)
    d_ff = w1_ref.shape[1]
    y = jnp.zeros(x1.shape, F32)
    for lo in range(0, d_ff, ff_chunk):
        a = jnp.maximum(_dot(h, w1_ref[:, lo:lo + ff_chunk]), 0.0)
        y = y + _dot((a * a).astype(BF16), w2_ref[lo:lo + ff_chunk, :])
    o_ref[...] = x1 + _rms(y, g_post_mlp_ref[...])


def _out_mlp(x2, oa, ob, w_out, g_post_mix, g_pre_mlp, w1, w2, g_post_mlp, tm, ff_chunk):
    m, d = x2.shape
    d_ff = w1.shape[1]
    const = lambda i: (0, 0)
    row = lambda i: (i, 0)
    once = dict(pipeline_mode=pl.Buffered(1))
    vec = lambda g: g.reshape(1, d).astype(F32)
    return pl.pallas_call(
        functools.partial(_out_mlp_kernel, ff_chunk=ff_chunk),
        grid=(m // tm,),
        in_specs=[
            pl.BlockSpec((tm, d), row),
            pl.BlockSpec((tm, A_WIDTH), row),
            pl.BlockSpec((tm, B_WIDTH), row),
            pl.BlockSpec((A_WIDTH + B_WIDTH, d), const, **once),
            pl.BlockSpec((1, d), const),
            pl.BlockSpec((1, d), const),
            pl.BlockSpec((d, d_ff), const, **once),
            pl.BlockSpec((d_ff, d), const, **once),
            pl.BlockSpec((1, d), const),
        ],
        out_specs=pl.BlockSpec((tm, d), row),
        out_shape=jax.ShapeDtypeStruct((m, d), F32),
        compiler_params=pltpu.CompilerParams(
            dimension_semantics=("arbitrary",), vmem_limit_bytes=VMEM_LIMIT),
        name="out_mlp",
    )(x2, oa, ob, w_out.astype(BF16), vec(g_post_mix), vec(g_pre_mlp),
      w1.astype(BF16), w2.astype(BF16), vec(g_post_mlp))


def kernel(x, w_in, c_norm, w_uk, w_uv, rel_bias, conv_w, a_log, dt_bias, o_norm, w_out,
           pre_norm_mix, post_norm_mix, pre_norm_mlp, post_norm_mlp, w_mlp_in, w_mlp_out):
    batch, seq, d = x.shape
    m = batch * seq
    tm = min(ROW_TILE, m)
    tm_in = min(IN_ROW_TILE, m)
    tq = min(QUERY_TILE, seq)
    assert m % tm == 0 and m % tm_in == 0 and seq % tq == 0 and seq % (PAIR * PAIRS_PER_STEP) == 0
    xs = x.reshape(m, d)
    for l in range(w_in.shape[0]):
        qat, qit, qkv, z, c, ct3, kidx, wit, ab, abt = _in_proj(
            xs, pre_norm_mix[l], w_in[l], c_norm[l], tm_in, tq)
        oa = _dsa(qat, qit, wit, kidx, c, ct3, w_uk[l], w_uv[l], rel_bias, batch, seq, tq, N_BISECT)
        ob = _gdn(qkv, z, ab, abt, conv_w[l], a_log[l], dt_bias[l], o_norm[l], batch, seq)
        xs = _out_mlp(xs, oa, ob, w_out[l], post_norm_mix[l], pre_norm_mlp[l],
                      w_mlp_in[l], w_mlp_out[l], post_norm_mlp[l], tm, FF_CHUNK)
    return xs.reshape(batch, seq, d)
```

```python
import functools
import math

import numpy as np
import jax
import jax.numpy as jnp
from jax import lax
from jax.experimental import pallas as pl
from jax.experimental.pallas import tpu as pltpu

F32 = jnp.float32
BF16 = jnp.bfloat16

EPS = 1e-6
CHUNK = 64
A_HEADS = 8
A_HEAD_DIM = 64
A_WIDTH = A_HEADS * A_HEAD_DIM
A_LATENT = 128
IDX_HEADS = 8
IDX_DIM = 64
TOPK_MAX = 256
REL_BUCKETS = 32
REL_MAX_DIST = 128
B_HEADS = 4
B_HEAD_DIM = 128
B_WIDTH = B_HEADS * B_HEAD_DIM
CONV_WIDTH = 4
IN_SPLITS = (A_WIDTH, A_LATENT, IDX_HEADS * IDX_DIM, IDX_DIM, IDX_HEADS,
             B_WIDTH, B_WIDTH, B_WIDTH, B_HEADS, B_HEADS, B_WIDTH)

LANES = 128
SUBLANES = 8
BF16_ROWS = 16
VMEM_LIMIT = 56 * 1024 * 1024
ROW_TILE = 1024
IN_ROW_TILE = 1024
QUERY_TILE = 256
FF_CHUNK = 1024
N_COARSE = 10
N_BISECT = 7
NEG_BIG = -1e30
LOG2E = 1.0 / math.log(2.0)
DENOM_ROWS = 16
FINISH_UNROLL = 2

NT_DIMS = (((1,), (1,)), ((), ()))


def _dot(a, b, precision=None):
    return jnp.dot(a, b, preferred_element_type=F32, precision=precision)


def _dot_nt(a, b, precision=None):
    return lax.dot_general(a, b, NT_DIMS, preferred_element_type=F32, precision=precision)


def _rms(x, g):
    return x * lax.rsqrt(jnp.mean(x * x, axis=-1, keepdims=True) + EPS) * g


def _sigmoid(x):
    return 1.0 / (1.0 + jnp.exp(-x))


def _silu(x):
    half = 0.5 * x
    return half + half * jnp.tanh(half)


def _softplus(x):
    return jnp.maximum(x, 0.0) + jnp.log1p(jnp.exp(-jnp.abs(x)))


_C_QKV = 0
_C_Z = _C_QKV + 3 * B_WIDTH
_C_CKV = _C_Z + B_WIDTH
_C_SMALL = _C_CKV + A_LATENT
_C_END = _C_SMALL + LANES
_R_QA = 0
_R_QI = _R_QA + A_WIDTH
_R_CKV = _R_QI + IDX_HEADS * IDX_DIM
_R_SMALL = _R_CKV + A_LATENT
_R_END = _R_SMALL + IDX_HEADS + 2 * B_HEADS


def _in_proj_kernel(x_ref, g_ref, w_ref, wt_ref, cn_ref, cnt_ref,
                    qat_ref, qit_ref, qkv_ref, z_ref, c_ref, ct_ref, kidx_ref, wit_ref, ab_ref, abt_ref,
                    *, ts):
    h = _rms(x_ref[...], g_ref[...]).astype(BF16)

    def proj(lo, hi):
        return _dot(h, w_ref[:, lo:hi])

    def proj_t(lo, hi):
        return _dot_nt(wt_ref[lo:hi, :], h)

    qkv_ref[...] = proj(_C_QKV, _C_Z).astype(BF16)
    z_ref[...] = proj(_C_Z, _C_CKV).astype(BF16)
    c_ref[...] = _rms(proj(_C_CKV, _C_SMALL), cn_ref[...]).astype(BF16)
    small = proj(_C_SMALL, _C_END)
    kidx_ref[...] = small[:, :IDX_DIM].astype(BF16)
    ab_ref[...] = small[:, IDX_DIM:IDX_DIM + 2 * B_HEADS]

    qat_ref[...] = proj_t(_R_QA, _R_QI).astype(BF16)
    qit_ref[...] = (proj_t(_R_QI, _R_CKV) * (IDX_DIM ** -0.5)).astype(BF16)
    ckv_t = proj_t(_R_CKV, _R_SMALL)
    c_t = (ckv_t * lax.rsqrt(jnp.mean(ckv_t * ckv_t, axis=0, keepdims=True) + EPS)
           * cnt_ref[...]).astype(BF16)
    for r in range(ct_ref.shape[0]):
        ct_ref[r] = c_t[:, r * ts:(r + 1) * ts]
    small_t = proj_t(_R_SMALL, _R_END)
    wit_ref[...] = small_t[:IDX_HEADS] * (IDX_HEADS ** -0.5)
    abt_ref[...] = small_t[IDX_HEADS:]


def _in_proj(x2, g, w_in, c_norm, tm, ts):
    m, d = x2.shape
    offs = np.cumsum((0,) + IN_SPLITS)
    seg = [w_in[:, offs[i]:offs[i + 1]] for i in range(len(IN_SPLITS))]
    q_a, c_kv, q_idx, k_idx, w_idx, q_b, k_b, v_b, a_b, b_b, z_b = seg
    pad = jnp.zeros((d, LANES - IDX_DIM - 2 * B_HEADS), w_in.dtype)
    w_all = jnp.concatenate([q_b, k_b, v_b, z_b, c_kv, k_idx, a_b, b_b, pad], axis=1).astype(BF16)
    w_t = jnp.concatenate([q_a, q_idx, c_kv, w_idx, a_b, b_b], axis=1).T.astype(BF16)
    const = lambda i: (0, 0)
    row = lambda i: (i, 0)
    col = lambda i: (0, i)
    out_shape = (
        jax.ShapeDtypeStruct((A_WIDTH, m), BF16),
        jax.ShapeDtypeStruct((IDX_HEADS * IDX_DIM, m), BF16),
        jax.ShapeDtypeStruct((m, 3 * B_WIDTH), BF16),
        jax.ShapeDtypeStruct((m, B_WIDTH), BF16),
        jax.ShapeDtypeStruct((m, A_LATENT), BF16),
        jax.ShapeDtypeStruct((m // ts, A_LATENT, ts), BF16),
        jax.ShapeDtypeStruct((m, IDX_DIM), BF16),
        jax.ShapeDtypeStruct((IDX_HEADS, m), F32),
        jax.ShapeDtypeStruct((m, 2 * B_HEADS), F32),
        jax.ShapeDtypeStruct((2 * B_HEADS, m), F32),
    )
    out_specs = (
        pl.BlockSpec((A_WIDTH, tm), col),
        pl.BlockSpec((IDX_HEADS * IDX_DIM, tm), col),
        pl.BlockSpec((tm, 3 * B_WIDTH), row),
        pl.BlockSpec((tm, B_WIDTH), row),
        pl.BlockSpec((tm, A_LATENT), row),
        pl.BlockSpec((tm // ts, A_LATENT, ts), lambda i: (i, 0, 0)),
        pl.BlockSpec((tm, IDX_DIM), row),
        pl.BlockSpec((IDX_HEADS, tm), col),
        pl.BlockSpec((tm, 2 * B_HEADS), row),
        pl.BlockSpec((2 * B_HEADS, tm), col),
    )
    return pl.pallas_call(
        functools.partial(_in_proj_kernel, ts=ts),
        grid=(m // tm,),
        in_specs=[
            pl.BlockSpec((tm, d), row),
            pl.BlockSpec((1, d), const),
            pl.BlockSpec((d, _C_END), const),
            pl.BlockSpec((_R_END, d), const),
            pl.BlockSpec((1, A_LATENT), const),
            pl.BlockSpec((A_LATENT, 1), const),
        ],
        out_specs=out_specs,
        out_shape=out_shape,
        compiler_params=pltpu.CompilerParams(
            dimension_semantics=("arbitrary",), vmem_limit_bytes=VMEM_LIMIT),
        name="in_proj",
    )(x2, g.reshape(1, d), w_all, w_t, c_norm.reshape(1, A_LATENT), c_norm.reshape(A_LATENT, 1))


def _fold_rows(x, op, stop=SUBLANES):
    r = x.shape[0]
    while r > stop:
        r //= 2
        x = op(x[:r], x[r:])
    return x


def _dsa_kernel(qat_ref, qit_ref, wit_ref, kidx_ref, c_ref, ct_ref, wuk_ref, wuvt_ref, bias_ref, o_ref,
                sc_ref, qabs_ref, m_ref, acc_ref, tie_base_ref, range_ref, sc16_ref, *, tq, ts, topk,
                ncoarse, nbisect):
    i = pl.program_id(1)
    nkb = i + 1
    q_pos = i * tq + lax.broadcasted_iota(jnp.int32, (1, tq), 1)
    limit = (q_pos // CHUNK + 1) * CHUNK
    kf = float(topk)
    ninf = -jnp.inf

    def key_ids(j):
        return j * ts + lax.broadcasted_iota(jnp.int32, (ts, 1), 0)

    def col_reduce(x8, op):
        return op(x8, axis=0, keepdims=True)

    wit = wit_ref[...]

    range_ref[0] = jnp.full((SUBLANES, tq), jnp.inf, F32)
    range_ref[1] = jnp.full((SUBLANES, tq), ninf, F32)

    def score_blocks(js, diagonal):
        k = jnp.concatenate([kidx_ref[j] for j in js], axis=0)
        r = [_dot(k, qit_ref[h * IDX_DIM:(h + 1) * IDX_DIM, :]) for h in range(IDX_HEADS)]
        s = jnp.maximum(r[0], 0.0) * wit[0:1, :]
        for h in range(1, IDX_HEADS):
            s = s + jnp.maximum(r[h], 0.0) * wit[h:h + 1, :]
        lo, hi = s, s
        if diagonal:
            adm = key_ids(js[0]) < limit
            s = jnp.where(adm, s, ninf)
            lo, hi = jnp.where(adm, s, jnp.inf), s
        for n, j in enumerate(js):
            sc_ref[j] = s[n * ts:(n + 1) * ts]
            sc16_ref[j] = s[n * ts:(n + 1) * ts].astype(BF16)
        range_ref[0] = jnp.minimum(range_ref[0], _fold_rows(lo, jnp.minimum))
        range_ref[1] = jnp.maximum(range_ref[1], _fold_rows(hi, jnp.maximum))

    def score_pair(jj, carry):
        score_blocks([2 * jj, 2 * jj + 1], False)
        return carry

    lax.fori_loop(0, i // 2, score_pair, 0)

    @pl.when(i % 2 == 1)
    def _():
        score_blocks([i - 1], False)

    score_blocks([i], True)
    rmin = col_reduce(range_ref[0], jnp.min)
    rmax = col_reduce(range_ref[1], jnp.max)

    def count(pred):
        def body(j, acc):
            return acc + _fold_rows(jnp.where(pred(sc_ref[j], j), 1.0, 0.0), jnp.add)
        acc = lax.fori_loop(0, nkb, body, jnp.zeros((SUBLANES, tq), F32))
        return col_reduce(acc, jnp.sum)

    def count_ge(v):
        return count(lambda s, j: s >= v)

    need = limit > topk

    def bisect_body(_, carry):
        lo, hi, n_hi = carry
        mid = lo + 0.5 * (hi - lo)
        c = count_ge(mid)
        ge = c >= kf
        return jnp.where(ge, mid, lo), jnp.where(ge, hi, mid), jnp.where(ge, n_hi, c)

    def count16_ge(v16):
        def body(j, acc):
            hit = jnp.where(sc16_ref[j] >= v16, jnp.ones((), BF16), jnp.zeros((), BF16))
            return acc + _fold_rows(hit, jnp.add, stop=BF16_ROWS).astype(F32)
        acc = lax.fori_loop(0, nkb, body, jnp.zeros((BF16_ROWS, tq), F32))
        return col_reduce(acc, jnp.sum)

    def coarse_body(_, carry):
        a, b = carry
        mid = (a + 0.5 * (b - a)).astype(BF16)
        ge = count16_ge(mid) >= kf
        mid = mid.astype(F32)
        return jnp.where(ge, mid, a), jnp.where(ge, b, mid)

    a0 = rmin.astype(BF16).astype(F32)
    top = rmax.astype(BF16).astype(F32)
    b0 = (top + jnp.abs(top) * 2.0 ** -6 + 1e-30).astype(BF16).astype(F32)
    a, b = lax.fori_loop(0, ncoarse, coarse_body, (a0, b0))
    lo0 = a - jnp.abs(a) * 2.0 ** -7 - 1e-30
    _, hi, n_hi = lax.fori_loop(0, nbisect, bisect_body, (lo0, b, count_ge(b)))

    def max_below(v):
        def body(j, acc):
            s = sc_ref[j]
            return jnp.maximum(acc, _fold_rows(jnp.where(s < v, s, ninf), jnp.maximum))
        return col_reduce(lax.fori_loop(0, nkb, body, jnp.full((SUBLANES, tq), ninf, F32)), jnp.max)

    def finish_cond(state):
        return jnp.min(state[-1]) < 0.5

    def finish_body(state):
        m, n_hi, thr, n_thr, done = state

        def body(j, carry):
            cnt, nxt = carry
            s = sc_ref[j]
            below = s < m
            return (cnt + _fold_rows(jnp.where(below, 0.0, 1.0), jnp.add),
                    jnp.maximum(nxt, _fold_rows(jnp.where(below, s, ninf), jnp.maximum)))

        cnt, nxt = lax.fori_loop(0, nkb, body, (jnp.zeros((SUBLANES, tq), F32),
                                               jnp.full((SUBLANES, tq), ninf, F32)))
        c = col_reduce(cnt, jnp.sum)
        open_ = done < 0.5
        hit = jnp.logical_and(open_, c >= kf)
        miss = jnp.logical_and(open_, c < kf)
        thr = jnp.where(hit, m, thr)
        n_thr = jnp.where(hit, c, n_thr)
        n_hi = jnp.where(miss, c, n_hi)
        m = jnp.where(miss, col_reduce(nxt, jnp.max), m)
        return m, n_hi, thr, n_thr, jnp.where(hit, 1.0, done)

    lowest = float(jnp.finfo(F32).min)
    zero = jnp.zeros((1, tq), F32)
    state = (max_below(hi), n_hi, jnp.full((1, tq), lowest, F32), zero, jnp.where(need, 0.0, 1.0))
    for _ in range(FINISH_UNROLL):
        state = finish_body(state)
    _, n_hi, thr, n_thr, _ = lax.while_loop(finish_cond, finish_body, state)

    n_tie = jnp.where(need, kf - n_hi, 0.0)
    surplus = jnp.max(jnp.where(jnp.logical_and(need, n_thr - n_hi > n_tie), 1.0, 0.0)) > 0.5
    tie_base_ref[...] = zero
    kr = lax.broadcasted_iota(jnp.int32, (ts, ts), 0)
    kc = lax.broadcasted_iota(jnp.int32, (ts, ts), 1)

    def mask_bias(j):
        def plain(_):
            return jnp.where(sc_ref[j] >= thr, 0.0, NEG_BIG)

        def with_ties(_):
            s = sc_ref[j]
            tie = s == thr
            prefix_ones = jnp.where(kc <= kr, 1.0, 0.0).astype(BF16)
            rank = _dot(prefix_ones, jnp.where(tie, 1.0, 0.0).astype(BF16)) + tie_base_ref[...]
            tie_base_ref[...] = rank[ts - 1:ts, :]
            keep = jnp.where(tie, jnp.where(rank <= n_tie, 0.0, NEG_BIG), NEG_BIG)
            return jnp.where(s > thr, 0.0, keep)

        return lax.cond(surplus, with_ties, plain, 0)

    for h in range(A_HEADS):
        qh = _dot(wuk_ref[h], qat_ref[h * A_HEAD_DIM:(h + 1) * A_HEAD_DIM, :])
        qabs_ref[h] = (qh * (A_HEAD_DIM ** -0.5 * LOG2E)).astype(BF16)
    m_ref[...] = jnp.full(m_ref.shape, NEG_BIG, F32)
    acc_ref[...] = jnp.zeros(acc_ref.shape, F32)
    heads = range(A_HEADS)

    def attend(js, bias_of_head):
        n = len(js) * ts
        cj = jnp.concatenate([c_ref[j] for j in js], axis=0)
        ctj = jnp.concatenate([jnp.concatenate([ct_ref[j] for j in js], axis=1),
                               jnp.ones((DENOM_ROWS, n), BF16)], axis=0)
        mb = jnp.concatenate([mask_bias(j) for j in js], axis=0)
        s = [_dot(cj, qabs_ref[h]) for h in heads]
        for h in heads:
            b = bias_of_head(h)
            s[h] = s[h] + (mb if b is None else mb + b)
        m_old = [m_ref[h] for h in heads]
        m_new = [jnp.maximum(m_old[h], col_reduce(_fold_rows(s[h], jnp.maximum), jnp.max))
                 for h in heads]
        alpha = [jnp.exp2(m_old[h] - m_new[h]) for h in heads]
        p = [jnp.exp2(s[h] - m_new[h]).astype(BF16) for h in heads]
        pv = [_dot(ctj, p[h]) for h in heads]
        for h in heads:
            acc_ref[h] = alpha[h] * acc_ref[h] + pv[h]
            m_ref[h] = m_new[h]

    n_far = jnp.maximum(i - 1, 0)

    def far_body(jj, carry):
        attend([2 * jj, 2 * jj + 1], lambda h: None)
        return carry

    lax.fori_loop(0, n_far // 2, far_body, 0)

    @pl.when(n_far % 2 == 1)
    def _():
        attend([n_far - 1], lambda h: None)

    @pl.when(i > 0)
    def _():
        attend([i - 1, i], lambda h: bias_ref[h])

    @pl.when(i == 0)
    def _():
        attend([i], lambda h: bias_ref[h, ts:, :])

    outs = [_dot(wuvt_ref[h],
                 (acc_ref[h, :A_LATENT, :] / acc_ref[h, A_LATENT:A_LATENT + 1, :]).astype(BF16))
            for h in range(A_HEADS)]
    o_ref[...] = jnp.concatenate(outs, axis=0).T.astype(o_ref.dtype)


def _t5_bucket(rel):
    nb = REL_BUCKETS // 2
    max_exact = nb // 2
    side = jnp.where(rel > 0, nb, 0)
    n = jnp.abs(rel)
    nf = jnp.maximum(n, 1).astype(jnp.float32)
    large = max_exact + (jnp.log(nf / max_exact) / math.log(REL_MAX_DIST / max_exact)
                         * (nb - max_exact)).astype(jnp.int32)
    large = jnp.minimum(large, nb - 1)
    return side + jnp.where(n < max_exact, n, large)


def _near_bias_kernel(far_ref, rb_ref, bucket_ref, o_ref):
    bucket = bucket_ref[...]
    for h in range(A_HEADS):
        acc = jnp.zeros(bucket.shape, F32)
        for b in range(REL_BUCKETS):
            acc = jnp.where(bucket == b, rb_ref[b, h], acc)
        o_ref[h] = (acc - rb_ref[far_ref[0], h]) * LOG2E


def _near_bias(rel_bias, tq, ts):
    assert ts >= REL_MAX_DIST
    rel = (jnp.arange(2 * ts)[:, None] - ts) - jnp.arange(tq)[None, :]
    far = _t5_bucket(jnp.full((1,), -(ts + 1), jnp.int32))
    smem = pl.BlockSpec(memory_space=pltpu.SMEM)
    return pl.pallas_call(
        _near_bias_kernel,
        in_specs=[smem, smem, pl.BlockSpec(memory_space=pltpu.VMEM)],
        out_specs=pl.BlockSpec(memory_space=pltpu.VMEM),
        out_shape=jax.ShapeDtypeStruct((A_HEADS, 2 * ts, tq), F32),
        name="near_bias",
    )(far, rel_bias.astype(F32), _t5_bucket(rel).astype(jnp.int32))


def _dsa(qat, qit, wit, kidx, c, ct3, w_uk, w_uv, rel_bias, batch, seq, tq, nbisect):
    ts = tq
    m = batch * seq
    nq = seq // tq
    nks = seq // ts
    topk = min(TOPK_MAX, seq // 4)
    kidx3 = kidx.reshape(m // ts, ts, IDX_DIM)
    c3 = c.reshape(m // ts, ts, A_LATENT)
    wuk = w_uk.astype(BF16)
    wuv_t = jnp.swapaxes(w_uv, 1, 2).astype(BF16)
    bias = _near_bias(rel_bias, tq, ts)
    qcol = lambda b, i: (0, b * nq + i)
    kv = lambda b, i: (b, 0, 0)
    const3 = lambda b, i: (0, 0, 0)
    kern = functools.partial(_dsa_kernel, tq=tq, ts=ts, topk=topk, ncoarse=N_COARSE, nbisect=nbisect)
    return pl.pallas_call(
        kern,
        grid=(batch, nq),
        in_specs=[
            pl.BlockSpec((A_WIDTH, tq), qcol),
            pl.BlockSpec((IDX_HEADS * IDX_DIM, tq), qcol),
            pl.BlockSpec((IDX_HEADS, tq), qcol),
            pl.BlockSpec((nks, ts, IDX_DIM), kv),
            pl.BlockSpec((nks, ts, A_LATENT), kv),
            pl.BlockSpec((nks, A_LATENT, ts), kv),
            pl.BlockSpec((A_HEADS, A_LATENT, A_HEAD_DIM), const3),
            pl.BlockSpec((A_HEADS, A_HEAD_DIM, A_LATENT), const3),
            pl.BlockSpec((A_HEADS, 2 * ts, tq), const3),
        ],
        out_specs=pl.BlockSpec((tq, A_WIDTH), lambda b, i: (b * nq + i, 0)),
        out_shape=jax.ShapeDtypeStruct((m, A_WIDTH), BF16),
        scratch_shapes=[
            pltpu.VMEM((nks, ts, tq), F32),
            pltpu.VMEM((A_HEADS, A_LATENT, tq), BF16),
            pltpu.VMEM((A_HEADS, 1, tq), F32),
            pltpu.VMEM((A_HEADS, A_LATENT + DENOM_ROWS, tq), F32),
            pltpu.VMEM((1, tq), F32),
            pltpu.VMEM((2, SUBLANES, tq), F32),
            pltpu.VMEM((nks, ts, tq), BF16),
        ],
        compiler_params=pltpu.CompilerParams(
            dimension_semantics=("arbitrary", "arbitrary"), vmem_limit_bytes=VMEM_LIMIT),
        name="dsa",
    )(qat, qit, wit, kidx3, c3, ct3, wuk, wuv_t, bias)


PAIR = 2 * CHUNK
PAIRS_PER_STEP = 2
HI = lax.Precision.HIGHEST


def _gdn_kernel(qkv_ref, z_ref, ab_ref, abt_ref, convw_ref, pcol_ref, prow_ref, onorm_ref, o_ref,
                u_s, w_s, qg_s, kdt_s, a_s, gl_s, st_s, stage_ref, *, seq):
    npair = seq // PAIR
    ri = lax.broadcasted_iota(jnp.int32, (PAIR, PAIR), 0)
    ci = lax.broadcasted_iota(jnp.int32, (PAIR, PAIR), 1)
    same = (ri // CHUNK) == (ci // CHUNK)
    lower_bd = jnp.where(jnp.logical_and(same, ci <= ri), 1.0, 0.0)
    upper_bd = jnp.where(jnp.logical_and(same, ri <= ci), 1.0, 0.0)
    causal_bd = jnp.logical_and(same, ci <= ri)
    diag = ci == ri
    eye = jnp.where(diag, 1.0, 0.0)
    neg_exp_a_col = -jnp.exp(pcol_ref[:, 0:1])
    dt_col = pcol_ref[:, 1:2]
    neg_exp_a_row = -jnp.exp(prow_ref[0:1, :])
    dt_row = prow_ref[1:2, :]
    convw = convw_ref[...]
    first_chunk = lax.broadcasted_iota(jnp.int32, (PAIR, 1), 0) < CHUNK

    def aligned(x, m):
        return x if isinstance(x, int) else pl.multiple_of(x, m)

    def conv_silu(i, p, col):
        slot = i * 3 * B_HEADS + col // B_HEAD_DIM
        r0 = aligned(p * PAIR, PAIR)
        cur = qkv_ref[pl.ds(r0, PAIR), col:col + B_HEAD_DIM].astype(F32)
        lead = BF16_ROWS
        prev0 = (max(r0 - lead, 0) if isinstance(r0, int)
                 else pl.multiple_of(jnp.maximum(r0 - lead, 0), lead))
        prev = qkv_ref[pl.ds(prev0, lead), col:col + B_HEAD_DIM].astype(F32)
        prev = jnp.where(p > 0, prev, 0.0)
        stage = stage_ref.at[slot]
        stage[0:lead, :] = prev
        stage[lead:lead + PAIR, :] = cur
        y = cur * convw[CONV_WIDTH - 1:CONV_WIDTH, col:col + B_HEAD_DIM]
        for back in range(1, CONV_WIDTH):
            sh = stage[lead - back:lead - back + PAIR, :]
            y = y + sh * convw[CONV_WIDTH - 1 - back:CONV_WIDTH - back, col:col + B_HEAD_DIM]
        return _silu(y)

    def l2n(x):
        return x * lax.rsqrt(jnp.sum(x * x, axis=-1, keepdims=True) + EPS)

    heads = range(B_HEADS)
    pairs = range(PAIRS_PER_STEP)
    units = [(i, h) for i in pairs for h in heads]

    def slot_of(step, i):
        return (step % 2) * PAIRS_PER_STEP + i

    def phase1(step):
        ps = [step * PAIRS_PER_STEP + i for i in pairs]
        gc_col, gc_row, beta_col = [], [], []
        for p in ps:
            ab = ab_ref[pl.ds(aligned(p * PAIR, PAIR), PAIR), :]
            g_col = neg_exp_a_row * _softplus(ab + dt_row)
            gc_col.append(_dot(lower_bd, g_col, HI))
            beta_col.append(_sigmoid(ab))
            g_row = neg_exp_a_col * _softplus(abt_ref[p] + dt_col)
            gc_row.append(_dot(g_row, upper_bd, HI))
        q = {(i, h): l2n(conv_silu(i, ps[i], h * B_HEAD_DIM)) * (B_HEAD_DIM ** -0.5) for i, h in units}
        k = {(i, h): l2n(conv_silu(i, ps[i], B_WIDTH + h * B_HEAD_DIM)) for i, h in units}
        v = {(i, h): conv_silu(i, ps[i], 2 * B_WIDTH + h * B_HEAD_DIM) for i, h in units}
        gcol = {(i, h): gc_col[i][:, h:h + 1] for i, h in units}
        beta = {(i, h): beta_col[i][:, B_HEADS + h:B_HEADS + h + 1] for i, h in units}
        kb = {u: k[u] * beta[u] for u in units}
        qk_kk = {u: _dot_nt(jnp.concatenate([q[u], kb[u]], axis=0).astype(BF16), k[u].astype(BF16))
                 for u in units}
        decay = {(i, h): jnp.exp(jnp.where(causal_bd, gcol[i, h] - gc_row[i][h:h + 1, :], -jnp.inf))
                 for i, h in units}
        intra = {u: (qk_kk[u][:PAIR] * decay[u]).astype(BF16) for u in units}
        low = {u: jnp.where(diag, 0.0, qk_kk[u][PAIR:] * decay[u]) for u in units}

        x = {u: eye - low[u] for u in units}
        low16 = {u: low[u].astype(BF16) for u in units}
        pw = {u: _dot(low16[u], low16[u]) for u in units}
        for _ in range(int(math.log2(CHUNK)) - 2):
            pw16 = {u: pw[u].astype(BF16) for u in units}
            r = {u: _dot(jnp.concatenate([x[u].astype(BF16), pw16[u]], axis=0), pw16[u]) for u in units}
            x = {u: x[u] + r[u][:PAIR] for u in units}
            pw = {u: r[u][PAIR:] for u in units}
        tinv = {u: x[u] + _dot(x[u].astype(BF16), pw[u].astype(BF16)) for u in units}

        eg = {u: jnp.exp(gcol[u]) for u in units}
        uw = {u: _dot(tinv[u].astype(BF16),
                      jnp.concatenate([v[u] * beta[u], kb[u] * eg[u]], axis=1).astype(BF16))
              for u in units}
        for i, h in units:
            u, slot = (i, h), slot_of(step, i)
            g_last = jnp.where(first_chunk, gcol[u][CHUNK - 1:CHUNK], gcol[u][PAIR - 1:PAIR])
            kd = k[u] * jnp.exp(g_last - gcol[u])
            u_s[slot, h] = uw[u][:, :B_HEAD_DIM]
            w_s[slot, h] = uw[u][:, B_HEAD_DIM:].astype(BF16)
            qg_s[slot, h] = (q[u] * eg[u]).astype(BF16)
            for cc in range(2):
                rs = slice(cc * CHUNK, (cc + 1) * CHUNK)
                kdt_s[slot, h, cc] = kd[rs].T.astype(BF16)
                a_s[slot, h, cc] = intra[u][rs, rs]
                gl_s[slot, h, cc] = jnp.broadcast_to(
                    jnp.exp(gcol[u][(cc + 1) * CHUNK - 1:(cc + 1) * CHUNK]), (SUBLANES, LANES))

    st_s[...] = jnp.zeros(st_s.shape, F32)
    onorm = onorm_ref[...]

    def phase2(step):
        for i in pairs:
            p, slot = step * PAIRS_PER_STEP + i, slot_of(step, i)
            for cc in range(2):
                rs = slice(cc * CHUNK, (cc + 1) * CHUNK)
                rows = pl.ds(aligned(p * PAIR + cc * CHUNK, CHUNK), CHUNK)
                state = [st_s[h] for h in heads]
                sb = [state[h].astype(BF16) for h in heads]
                wq = [_dot(jnp.concatenate([w_s[slot, h, rs, :], qg_s[slot, h, rs, :]], axis=0), sb[h])
                      for h in heads]
                vb = [(u_s[slot, h, rs, :] - wq[h][:CHUNK]).astype(BF16) for h in heads]
                upd = [_dot(kdt_s[slot, h, cc], vb[h]) for h in heads]
                av = [_dot(a_s[slot, h, cc], vb[h]) for h in heads]
                for h in heads:
                    st_s[h] = state[h] * gl_s[slot, h, cc][0:1, :] + upd[h]
                for h in heads:
                    o = wq[h][CHUNK:] + av[h]
                    z = z_ref[rows, h * B_HEAD_DIM:(h + 1) * B_HEAD_DIM].astype(F32)
                    o_ref[rows, h * B_HEAD_DIM:(h + 1) * B_HEAD_DIM] = (
                        _rms(o, onorm) * _silu(z)).astype(o_ref.dtype)

    def body(step, carry):
        phase2(step - 1)
        phase1(step)
        return carry

    nstep = npair // PAIRS_PER_STEP
    phase1(0)
    lax.fori_loop(1, nstep, body, 0)
    phase2(nstep - 1)


def _gdn(qkv, z, ab, abt, conv_w, a_log, dt_bias, o_norm, batch, seq):
    m = batch * seq
    npair = seq // PAIR
    assert npair % PAIRS_PER_STEP == 0
    nslot = 2 * PAIRS_PER_STEP
    abt3 = jnp.swapaxes(abt.reshape(2 * B_HEADS, m // PAIR, PAIR), 0, 1)
    zeros = jnp.zeros((B_HEADS,), F32)
    alog = jnp.concatenate([a_log.astype(F32), zeros])
    dtb = jnp.concatenate([dt_bias.astype(F32), zeros])
    pcol = jnp.stack([alog, dtb], axis=1)
    prow = jnp.stack([alog, dtb], axis=0)
    const = lambda b: (0, 0)
    rows = lambda b: (b, 0)
    return pl.pallas_call(
        functools.partial(_gdn_kernel, seq=seq),
        grid=(batch,),
        in_specs=[
            pl.BlockSpec((seq, 3 * B_WIDTH), rows),
            pl.BlockSpec((seq, B_WIDTH), rows),
            pl.BlockSpec((seq, 2 * B_HEADS), rows),
            pl.BlockSpec((npair, 2 * B_HEADS, PAIR), lambda b: (b, 0, 0)),
            pl.BlockSpec((CONV_WIDTH, 3 * B_WIDTH), const),
            pl.BlockSpec((2 * B_HEADS, 2), const),
            pl.BlockSpec((2, 2 * B_HEADS), const),
            pl.BlockSpec((1, B_HEAD_DIM), const),
        ],
        out_specs=pl.BlockSpec((seq, B_WIDTH), rows),
        out_shape=jax.ShapeDtypeStruct((m, B_WIDTH), BF16),
        scratch_shapes=[
            pltpu.VMEM((nslot, B_HEADS, PAIR, B_HEAD_DIM), F32),
            pltpu.VMEM((nslot, B_HEADS, PAIR, B_HEAD_DIM), BF16),
            pltpu.VMEM((nslot, B_HEADS, PAIR, B_HEAD_DIM), BF16),
            pltpu.VMEM((nslot, B_HEADS, 2, B_HEAD_DIM, CHUNK), BF16),
            pltpu.VMEM((nslot, B_HEADS, 2, CHUNK, CHUNK), BF16),
            pltpu.VMEM((nslot, B_HEADS, 2, SUBLANES, LANES), F32),
            pltpu.VMEM((B_HEADS, B_HEAD_DIM, B_HEAD_DIM), F32),
            pltpu.VMEM((PAIRS_PER_STEP * 3 * B_HEADS, BF16_ROWS + PAIR, B_HEAD_DIM), F32),
        ],
        compiler_params=pltpu.CompilerParams(
            dimension_semantics=("arbitrary",), vmem_limit_bytes=VMEM_LIMIT),
        name="gdn",
    )(qkv, z, ab, abt3, conv_w.astype(F32), pcol, prow, o_norm.reshape(1, B_HEAD_DIM).astype(F32))


def _out_mlp_kernel(x_ref, oa_ref, ob_ref, wo_ref, g_post_mix_ref, g_pre_mlp_ref, w1_ref, w2_ref,
                    g_post_mlp_ref, o_ref, *, ff_chunk):
    mix = _dot(oa_ref[...], wo_ref[:A_WIDTH, :]) + _dot(ob_ref[...], wo_ref[A_WIDTH:, :])
    x1 = x_ref[...] + _rms(mix, g_post_mix_ref[...])
    h = _rms(x1, g_pre_mlp_ref[...]).astype(BF16)
    d_ff = w1_ref.shape[1]
    y = jnp.zeros(x1.shape, F32)
    for lo in range(0, d_ff, ff_chunk):
        a = jnp.maximum(_dot(h, w1_ref[:, lo:lo + ff_chunk]), 0.0)
        y = y + _dot((a * a).astype(BF16), w2_ref[lo:lo + ff_chunk, :])
    o_ref[...] = x1 + _rms(y, g_post_mlp_ref[...])


def _out_mlp(x2, oa, ob, w_out, g_post_mix, g_pre_mlp, w1, w2, g_post_mlp, tm, ff_chunk):
    m, d = x2.shape
    d_ff = w1.shape[1]
    const = lambda i: (0, 0)
    row = lambda i: (i, 0)
    once = dict(pipeline_mode=pl.Buffered(1))
    vec = lambda g: g.reshape(1, d).astype(F32)
    return pl.pallas_call(
        functools.partial(_out_mlp_kernel, ff_chunk=ff_chunk),
        grid=(m // tm,),
        in_specs=[
            pl.BlockSpec((tm, d), row),
            pl.BlockSpec((tm, A_WIDTH), row),
            pl.BlockSpec((tm, B_WIDTH), row),
            pl.BlockSpec((A_WIDTH + B_WIDTH, d), const, **once),
            pl.BlockSpec((1, d), const),
            pl.BlockSpec((1, d), const),
            pl.BlockSpec((d, d_ff), const, **once),
            pl.BlockSpec((d_ff, d), const, **once),
            pl.BlockSpec((1, d), const),
        ],
        out_specs=pl.BlockSpec((tm, d), row),
        out_shape=jax.ShapeDtypeStruct((m, d), F32),
        compiler_params=pltpu.CompilerParams(
            dimension_semantics=("arbitrary",), vmem_limit_bytes=VMEM_LIMIT),
        name="out_mlp",
    )(x2, oa, ob, w_out.astype(BF16), vec(g_post_mix), vec(g_pre_mlp),
      w1.astype(BF16), w2.astype(BF16), vec(g_post_mlp))


def kernel(x, w_in, c_norm, w_uk, w_uv, rel_bias, conv_w, a_log, dt_bias, o_norm, w_out,
           pre_norm_mix, post_norm_mix, pre_norm_mlp, post_norm_mlp, w_mlp_in, w_mlp_out):
    batch, seq, d = x.shape
    m = batch * seq
    tm = min(ROW_TILE, m)
    tm_in = min(IN_ROW_TILE, m)
    tq = min(QUERY_TILE, seq)
    assert m % tm == 0 and m % tm_in == 0 and seq % tq == 0 and seq % (PAIR * PAIRS_PER_STEP) == 0
    xs = x.reshape(m, d)
    for l in range(w_in.shape[0]):
        qat, qit, qkv, z, c, ct3, kidx, wit, ab, abt = _in_proj(
            xs, pre_norm_mix[l], w_in[l], c_norm[l], tm_in, tq)
        oa = _dsa(qat, qit, wit, kidx, c, ct3, w_uk[l], w_uv[l], rel_bias, batch, seq, tq, N_BISECT)
        ob = _gdn(qkv, z, ab, abt, conv_w[l], a_log[l], dt_bias[l], o_norm[l], batch, seq)
        xs = _out_mlp(xs, oa, ob, w_out[l], post_norm_mix[l], pre_norm_mlp[l],
                      w_mlp_in[l], w_mlp_out[l], post_norm_mlp[l], tm, FF_CHUNK)
    return xs.reshape(batch, seq, d)
```

```python
import functools
import math

import numpy as np
import jax
import jax.numpy as jnp
from jax import lax
from jax.experimental import pallas as pl
from jax.experimental.pallas import tpu as pltpu

F32 = jnp.float32
BF16 = jnp.bfloat16

EPS = 1e-6
CHUNK = 64
A_HEADS = 8
A_HEAD_DIM = 64
A_WIDTH = A_HEADS * A_HEAD_DIM
A_LATENT = 128
IDX_HEADS = 8
IDX_DIM = 64
TOPK_MAX = 256
REL_BUCKETS = 32
REL_MAX_DIST = 128
B_HEADS = 4
B_HEAD_DIM = 128
B_WIDTH = B_HEADS * B_HEAD_DIM
CONV_WIDTH = 4
IN_SPLITS = (A_WIDTH, A_LATENT, IDX_HEADS * IDX_DIM, IDX_DIM, IDX_HEADS,
             B_WIDTH, B_WIDTH, B_WIDTH, B_HEADS, B_HEADS, B_WIDTH)

LANES = 128
SUBLANES = 8
BF16_ROWS = 16
VMEM_LIMIT = 56 * 1024 * 1024
ROW_TILE = 1024
IN_ROW_TILE = 1024
QUERY_TILE = 256
FF_CHUNK = 1024
N_COARSE = 10
N_BISECT = 7
NEG_BIG = -1e30
LOG2E = 1.0 / math.log(2.0)
DENOM_ROWS = 16
FINISH_UNROLL = 2

NT_DIMS = (((1,), (1,)), ((), ()))


def _dot(a, b, precision=None):
    return jnp.dot(a, b, preferred_element_type=F32, precision=precision)


def _dot_nt(a, b, precision=None):
    return lax.dot_general(a, b, NT_DIMS, preferred_element_type=F32, precision=precision)


def _rms(x, g):
    return x * lax.rsqrt(jnp.mean(x * x, axis=-1, keepdims=True) + EPS) * g


def _sigmoid(x):
    return 1.0 / (1.0 + jnp.exp(-x))


def _silu(x):
    half = 0.5 * x
    return half + half * jnp.tanh(half)


def _softplus(x):
    return jnp.maximum(x, 0.0) + jnp.log1p(jnp.exp(-jnp.abs(x)))


_C_QKV = 0
_C_Z = _C_QKV + 3 * B_WIDTH
_C_CKV = _C_Z + B_WIDTH
_C_SMALL = _C_CKV + A_LATENT
_C_END = _C_SMALL + LANES
_R_QA = 0
_R_QI = _R_QA + A_WIDTH
_R_CKV = _R_QI + IDX_HEADS * IDX_DIM
_R_SMALL = _R_CKV + A_LATENT
_R_END = _R_SMALL + IDX_HEADS + 2 * B_HEADS


def _in_proj_kernel(x_ref, g_ref, w_ref, wt_ref, cn_ref, cnt_ref,
                    qat_ref, qit_ref, qkv_ref, z_ref, c_ref, ct_ref, kidx_ref, wit_ref, ab_ref, abt_ref,
                    *, ts):
    h = _rms(x_ref[...], g_ref[...]).astype(BF16)

    def proj(lo, hi):
        return _dot(h, w_ref[:, lo:hi])

    def proj_t(lo, hi):
        return _dot_nt(wt_ref[lo:hi, :], h)

    qkv_ref[...] = proj(_C_QKV, _C_Z).astype(BF16)
    z_ref[...] = proj(_C_Z, _C_CKV).astype(BF16)
    c_ref[...] = _rms(proj(_C_CKV, _C_SMALL), cn_ref[...]).astype(BF16)
    small = proj(_C_SMALL, _C_END)
    kidx_ref[...] = small[:, :IDX_DIM].astype(BF16)
    ab_ref[...] = small[:, IDX_DIM:IDX_DIM + 2 * B_HEADS]

    qat_ref[...] = proj_t(_R_QA, _R_QI).astype(BF16)
    qit_ref[...] = (proj_t(_R_QI, _R_CKV) * (IDX_DIM ** -0.5)).astype(BF16)
    ckv_t = proj_t(_R_CKV, _R_SMALL)
    c_t = (ckv_t * lax.rsqrt(jnp.mean(ckv_t * ckv_t, axis=0, keepdims=True) + EPS)
           * cnt_ref[...]).astype(BF16)
    for r in range(ct_ref.shape[0]):
        ct_ref[r] = c_t[:, r * ts:(r + 1) * ts]
    small_t = proj_t(_R_SMALL, _R_END)
    wit_ref[...] = small_t[:IDX_HEADS] * (IDX_HEADS ** -0.5)
    abt_ref[...] = small_t[IDX_HEADS:]


def _in_proj(x2, g, w_in, c_norm, tm, ts):
    m, d = x2.shape
    offs = np.cumsum((0,) + IN_SPLITS)
    seg = [w_in[:, offs[i]:offs[i + 1]] for i in range(len(IN_SPLITS))]
    q_a, c_kv, q_idx, k_idx, w_idx, q_b, k_b, v_b, a_b, b_b, z_b = seg
    pad = jnp.zeros((d, LANES - IDX_DIM - 2 * B_HEADS), w_in.dtype)
    w_all = jnp.concatenate([q_b, k_b, v_b, z_b, c_kv, k_idx, a_b, b_b, pad], axis=1).astype(BF16)
    w_t = jnp.concatenate([q_a, q_idx, c_kv, w_idx, a_b, b_b], axis=1).T.astype(BF16)
    const = lambda i: (0, 0)
    row = lambda i: (i, 0)
    col = lambda i: (0, i)
    out_shape = (
        jax.ShapeDtypeStruct((A_WIDTH, m), BF16),
        jax.ShapeDtypeStruct((IDX_HEADS * IDX_DIM, m), BF16),
        jax.ShapeDtypeStruct((m, 3 * B_WIDTH), BF16),
        jax.ShapeDtypeStruct((m, B_WIDTH), BF16),
        jax.ShapeDtypeStruct((m, A_LATENT), BF16),
        jax.ShapeDtypeStruct((m // ts, A_LATENT, ts), BF16),
        jax.ShapeDtypeStruct((m, IDX_DIM), BF16),
        jax.ShapeDtypeStruct((IDX_HEADS, m), F32),
        jax.ShapeDtypeStruct((m, 2 * B_HEADS), F32),
        jax.ShapeDtypeStruct((2 * B_HEADS, m), F32),
    )
    out_specs = (
        pl.BlockSpec((A_WIDTH, tm), col),
        pl.BlockSpec((IDX_HEADS * IDX_DIM, tm), col),
        pl.BlockSpec((tm, 3 * B_WIDTH), row),
        pl.BlockSpec((tm, B_WIDTH), row),
        pl.BlockSpec((tm, A_LATENT), row),
        pl.BlockSpec((tm // ts, A_LATENT, ts), lambda i: (i, 0, 0)),
        pl.BlockSpec((tm, IDX_DIM), row),
        pl.BlockSpec((IDX_HEADS, tm), col),
        pl.BlockSpec((tm, 2 * B_HEADS), row),
        pl.BlockSpec((2 * B_HEADS, tm), col),
    )
    return pl.pallas_call(
        functools.partial(_in_proj_kernel, ts=ts),
        grid=(m // tm,),
        in_specs=[
            pl.BlockSpec((tm, d), row),
            pl.BlockSpec((1, d), const),
            pl.BlockSpec((d, _C_END), const),
            pl.BlockSpec((_R_END, d), const),
            pl.BlockSpec((1, A_LATENT), const),
            pl.BlockSpec((A_LATENT, 1), const),
        ],
        out_specs=out_specs,
        out_shape=out_shape,
        compiler_params=pltpu.CompilerParams(
            dimension_semantics=("arbitrary",), vmem_limit_bytes=VMEM_LIMIT),
        name="in_proj",
    )(x2, g.reshape(1, d), w_all, w_t, c_norm.reshape(1, A_LATENT), c_norm.reshape(A_LATENT, 1))


def _fold_rows(x, op, stop=SUBLANES):
    r = x.shape[0]
    while r > stop:
        r //= 2
        x = op(x[:r], x[r:])
    return x


def _dsa_kernel(qat_ref, qit_ref, wit_ref, kidx_ref, c_ref, ct_ref, wuk_ref, wuvt_ref, bias_ref, o_ref,
                sc_ref, qabs_ref, m_ref, acc_ref, tie_base_ref, range_ref, sc16_ref, *, tq, ts, topk,
                ncoarse, nbisect):
    i = pl.program_id(1)
    nkb = i + 1
    q_pos = i * tq + lax.broadcasted_iota(jnp.int32, (1, tq), 1)
    limit = (q_pos // CHUNK + 1) * CHUNK
    kf = float(topk)
    ninf = -jnp.inf

    def key_ids(j):
        return j * ts + lax.broadcasted_iota(jnp.int32, (ts, 1), 0)

    def col_reduce(x8, op):
        return op(x8, axis=0, keepdims=True)

    wit = wit_ref[...]

    range_ref[0] = jnp.full((SUBLANES, tq), jnp.inf, F32)
    range_ref[1] = jnp.full((SUBLANES, tq), ninf, F32)

    def score_blocks(js, diagonal):
        k = jnp.concatenate([kidx_ref[j] for j in js], axis=0)
        r = [_dot(k, qit_ref[h * IDX_DIM:(h + 1) * IDX_DIM, :]) for h in range(IDX_HEADS)]
        s = jnp.maximum(r[0], 0.0) * wit[0:1, :]
        for h in range(1, IDX_HEADS):
            s = s + jnp.maximum(r[h], 0.0) * wit[h:h + 1, :]
        lo, hi = s, s
        if diagonal:
            adm = key_ids(js[0]) < limit
            s = jnp.where(adm, s, ninf)
            lo, hi = jnp.where(adm, s, jnp.inf), s
        for n, j in enumerate(js):
            sc_ref[j] = s[n * ts:(n + 1) * ts]
            sc16_ref[j] = s[n * ts:(n + 1) * ts].astype(BF16)
        range_ref[0] = jnp.minimum(range_ref[0], _fold_rows(lo, jnp.minimum))
        range_ref[1] = jnp.maximum(range_ref[1], _fold_rows(hi, jnp.maximum))

    def score_pair(jj, carry):
        score_blocks([2 * jj, 2 * jj + 1], False)
        return carry

    lax.fori_loop(0, i // 2, score_pair, 0)

    @pl.when(i % 2 == 1)
    def _():
        score_blocks([i - 1], False)

    score_blocks([i], True)
    rmin = col_reduce(range_ref[0], jnp.min)
    rmax = col_reduce(range_ref[1], jnp.max)

    def count(pred):
        def body(j, acc):
            return acc + _fold_rows(jnp.where(pred(sc_ref[j], j), 1.0, 0.0), jnp.add)
        acc = lax.fori_loop(0, nkb, body, jnp.zeros((SUBLANES, tq), F32))
        return col_reduce(acc, jnp.sum)

    def count_ge(v):
        return count(lambda s, j: s >= v)

    need = limit > topk

    def bisect_body(_, carry):
        lo, hi, n_hi = carry
        mid = lo + 0.5 * (hi - lo)
        c = count_ge(mid)
        ge = c >= kf
        return jnp.where(ge, mid, lo), jnp.where(ge, hi, mid), jnp.where(ge, n_hi, c)

    def count16_ge(v16):
        def body(j, acc):
            hit = jnp.where(sc16_ref[j] >= v16, jnp.ones((), BF16), jnp.zeros((), BF16))
            return acc + _fold_rows(hit, jnp.add, stop=BF16_ROWS).astype(F32)
        acc = lax.fori_loop(0, nkb, body, jnp.zeros((BF16_ROWS, tq), F32))
        return col_reduce(acc, jnp.sum)

    def coarse_body(_, carry):
        a, b = carry
        mid = (a + 0.5 * (b - a)).astype(BF16)
        ge = count16_ge(mid) >= kf
        mid = mid.astype(F32)
        return jnp.where(ge, mid, a), jnp.where(ge, b, mid)


    def max_below(v):
        def body(j, acc):
            s = sc_ref[j]
            return jnp.maximum(acc, _fold_rows(jnp.where(s < v, s, ninf), jnp.maximum))
        return col_reduce(lax.fori_loop(0, nkb, body, jnp.full((SUBLANES, tq), ninf, F32)), jnp.max)

    def finish_cond(state):
        return jnp.min(state[-1]) < 0.5

    def finish_body(state):
        m, n_hi, thr, n_thr, done = state

        def body(j, carry):
            cnt, nxt = carry
            s = sc_ref[j]
            below = s < m
            return (cnt + _fold_rows(jnp.where(below, 0.0, 1.0), jnp.add),
                    jnp.maximum(nxt, _fold_rows(jnp.where(below, s, ninf), jnp.maximum)))

        cnt, nxt = lax.fori_loop(0, nkb, body, (jnp.zeros((SUBLANES, tq), F32),
                                               jnp.full((SUBLANES, tq), ninf, F32)))
        c = col_reduce(cnt, jnp.sum)
        open_ = done < 0.5
        hit = jnp.logical_and(open_, c >= kf)
        miss = jnp.logical_and(open_, c < kf)
        thr = jnp.where(hit, m, thr)
        n_thr = jnp.where(hit, c, n_thr)
        n_hi = jnp.where(miss, c, n_hi)
        m = jnp.where(miss, col_reduce(nxt, jnp.max), m)
        return m, n_hi, thr, n_thr, jnp.where(hit, 1.0, done)

    lowest = float(jnp.finfo(F32).min)
    zero = jnp.zeros((1, tq), F32)
    keep_all = jnp.full((1, tq), lowest, F32)

    def search(_):
        a0 = rmin.astype(BF16).astype(F32)
        top = rmax.astype(BF16).astype(F32)
        b0 = (top + jnp.abs(top) * 2.0 ** -6 + 1e-30).astype(BF16).astype(F32)
        a, b = lax.fori_loop(0, ncoarse, coarse_body, (a0, b0))
        lo0 = a - jnp.abs(a) * 2.0 ** -7 - 1e-30
        _, hi, n_hi = lax.fori_loop(0, nbisect, bisect_body, (lo0, b, count_ge(b)))
        state = (max_below(hi), n_hi, keep_all, zero, jnp.where(need, 0.0, 1.0))
        for _ in range(FINISH_UNROLL):
            state = finish_body(state)
        _, n_hi, thr, n_thr, _ = lax.while_loop(finish_cond, finish_body, state)
        return n_hi, thr, n_thr

    n_hi, thr, n_thr = lax.cond((i + 1) * tq > topk, search, lambda _: (zero, keep_all, zero), 0)

    n_tie = jnp.where(need, kf - n_hi, 0.0)
    surplus = jnp.max(jnp.where(jnp.logical_and(need, n_thr - n_hi > n_tie), 1.0, 0.0)) > 0.5
    tie_base_ref[...] = zero
    kr = lax.broadcasted_iota(jnp.int32, (ts, ts), 0)
    kc = lax.broadcasted_iota(jnp.int32, (ts, ts), 1)

    def mask_bias(j):
        def plain(_):
            return jnp.where(sc_ref[j] >= thr, 0.0, NEG_BIG)

        def with_ties(_):
            s = sc_ref[j]
            tie = s == thr
            prefix_ones = jnp.where(kc <= kr, 1.0, 0.0).astype(BF16)
            rank = _dot(prefix_ones, jnp.where(tie, 1.0, 0.0).astype(BF16)) + tie_base_ref[...]
            tie_base_ref[...] = rank[ts - 1:ts, :]
            keep = jnp.where(tie, jnp.where(rank <= n_tie, 0.0, NEG_BIG), NEG_BIG)
            return jnp.where(s > thr, 0.0, keep)

        return lax.cond(surplus, with_ties, plain, 0)

    for h in range(A_HEADS):
        qh = _dot(wuk_ref[h], qat_ref[h * A_HEAD_DIM:(h + 1) * A_HEAD_DIM, :])
        qabs_ref[h] = (qh * (A_HEAD_DIM ** -0.5 * LOG2E)).astype(BF16)
    m_ref[...] = jnp.full(m_ref.shape, NEG_BIG, F32)
    acc_ref[...] = jnp.zeros(acc_ref.shape, F32)
    heads = range(A_HEADS)

    def attend(js, bias_of_head):
        n = len(js) * ts
        cj = jnp.concatenate([c_ref[j] for j in js], axis=0)
        ctj = jnp.concatenate([jnp.concatenate([ct_ref[j] for j in js], axis=1),
                               jnp.ones((DENOM_ROWS, n), BF16)], axis=0)
        mb = jnp.concatenate([mask_bias(j) for j in js], axis=0)
        s = [_dot(cj, qabs_ref[h]) for h in heads]
        for h in heads:
            b = bias_of_head(h)
            s[h] = s[h] + (mb if b is None else mb + b)
        m_old = [m_ref[h] for h in heads]
        m_new = [jnp.maximum(m_old[h], col_reduce(_fold_rows(s[h], jnp.maximum), jnp.max))
                 for h in heads]
        alpha = [jnp.exp2(m_old[h] - m_new[h]) for h in heads]
        p = [jnp.exp2(s[h] - m_new[h]).astype(BF16) for h in heads]
        pv = [_dot(ctj, p[h]) for h in heads]
        for h in heads:
            acc_ref[h] = alpha[h] * acc_ref[h] + pv[h]
            m_ref[h] = m_new[h]

    n_far = jnp.maximum(i - 1, 0)

    def far_body(jj, carry):
        attend([2 * jj, 2 * jj + 1], lambda h: None)
        return carry

    lax.fori_loop(0, n_far // 2, far_body, 0)

    @pl.when(n_far % 2 == 1)
    def _():
        attend([n_far - 1], lambda h: None)

    @pl.when(i > 0)
    def _():
        attend([i - 1, i], lambda h: bias_ref[h])

    @pl.when(i == 0)
    def _():
        attend([i], lambda h: bias_ref[h, ts:, :])

    outs = [_dot(wuvt_ref[h],
                 (acc_ref[h, :A_LATENT, :] / acc_ref[h, A_LATENT:A_LATENT + 1, :]).astype(BF16))
            for h in range(A_HEADS)]
    o_ref[...] = jnp.concatenate(outs, axis=0).T.astype(o_ref.dtype)


def _t5_bucket(rel):
    nb = REL_BUCKETS // 2
    max_exact = nb // 2
    side = jnp.where(rel > 0, nb, 0)
    n = jnp.abs(rel)
    nf = jnp.maximum(n, 1).astype(jnp.float32)
    large = max_exact + (jnp.log(nf / max_exact) / math.log(REL_MAX_DIST / max_exact)
                         * (nb - max_exact)).astype(jnp.int32)
    large = jnp.minimum(large, nb - 1)
    return side + jnp.where(n < max_exact, n, large)


def _near_bias_kernel(far_ref, rb_ref, bucket_ref, o_ref):
    bucket = bucket_ref[...]
    for h in range(A_HEADS):
        acc = jnp.zeros(bucket.shape, F32)
        for b in range(REL_BUCKETS):
            acc = jnp.where(bucket == b, rb_ref[b, h], acc)
        o_ref[h] = (acc - rb_ref[far_ref[0], h]) * LOG2E


def _near_bias(rel_bias, tq, ts):
    assert ts >= REL_MAX_DIST
    rel = (jnp.arange(2 * ts)[:, None] - ts) - jnp.arange(tq)[None, :]
    far = _t5_bucket(jnp.full((1,), -(ts + 1), jnp.int32))
    smem = pl.BlockSpec(memory_space=pltpu.SMEM)
    return pl.pallas_call(
        _near_bias_kernel,
        in_specs=[smem, smem, pl.BlockSpec(memory_space=pltpu.VMEM)],
        out_specs=pl.BlockSpec(memory_space=pltpu.VMEM),
        out_shape=jax.ShapeDtypeStruct((A_HEADS, 2 * ts, tq), F32),
        name="near_bias",
    )(far, rel_bias.astype(F32), _t5_bucket(rel).astype(jnp.int32))


def _dsa(qat, qit, wit, kidx, c, ct3, w_uk, w_uv, rel_bias, batch, seq, tq, nbisect):
    ts = tq
    m = batch * seq
    nq = seq // tq
    nks = seq // ts
    topk = min(TOPK_MAX, seq // 4)
    kidx3 = kidx.reshape(m // ts, ts, IDX_DIM)
    c3 = c.reshape(m // ts, ts, A_LATENT)
    wuk = w_uk.astype(BF16)
    wuv_t = jnp.swapaxes(w_uv, 1, 2).astype(BF16)
    bias = _near_bias(rel_bias, tq, ts)
    qcol = lambda b, i: (0, b * nq + i)
    kv = lambda b, i: (b, 0, 0)
    const3 = lambda b, i: (0, 0, 0)
    kern = functools.partial(_dsa_kernel, tq=tq, ts=ts, topk=topk, ncoarse=N_COARSE, nbisect=nbisect)
    return pl.pallas_call(
        kern,
        grid=(batch, nq),
        in_specs=[
            pl.BlockSpec((A_WIDTH, tq), qcol),
            pl.BlockSpec((IDX_HEADS * IDX_DIM, tq), qcol),
            pl.BlockSpec((IDX_HEADS, tq), qcol),
            pl.BlockSpec((nks, ts, IDX_DIM), kv),
            pl.BlockSpec((nks, ts, A_LATENT), kv),
            pl.BlockSpec((nks, A_LATENT, ts), kv),
            pl.BlockSpec((A_HEADS, A_LATENT, A_HEAD_DIM), const3),
            pl.BlockSpec((A_HEADS, A_HEAD_DIM, A_LATENT), const3),
            pl.BlockSpec((A_HEADS, 2 * ts, tq), const3),
        ],
        out_specs=pl.BlockSpec((tq, A_WIDTH), lambda b, i: (b * nq + i, 0)),
        out_shape=jax.ShapeDtypeStruct((m, A_WIDTH), BF16),
        scratch_shapes=[
            pltpu.VMEM((nks, ts, tq), F32),
            pltpu.VMEM((A_HEADS, A_LATENT, tq), BF16),
            pltpu.VMEM((A_HEADS, 1, tq), F32),
            pltpu.VMEM((A_HEADS, A_LATENT + DENOM_ROWS, tq), F32),
            pltpu.VMEM((1, tq), F32),
            pltpu.VMEM((2, SUBLANES, tq), F32),
            pltpu.VMEM((nks, ts, tq), BF16),
        ],
        compiler_params=pltpu.CompilerParams(
            dimension_semantics=("arbitrary", "arbitrary"), vmem_limit_bytes=VMEM_LIMIT),
        name="dsa",
    )(qat, qit, wit, kidx3, c3, ct3, wuk, wuv_t, bias)


PAIR = 2 * CHUNK
PAIRS_PER_STEP = 2
HI = lax.Precision.HIGHEST


def _gdn_kernel(qkv_ref, z_ref, ab_ref, abt_ref, convw_ref, pcol_ref, prow_ref, onorm_ref, o_ref,
                u_s, w_s, qg_s, kdt_s, a_s, gl_s, st_s, stage_ref, *, seq):
    npair = seq // PAIR
    ri = lax.broadcasted_iota(jnp.int32, (PAIR, PAIR), 0)
    ci = lax.broadcasted_iota(jnp.int32, (PAIR, PAIR), 1)
    same = (ri // CHUNK) == (ci // CHUNK)
    lower_bd = jnp.where(jnp.logical_and(same, ci <= ri), 1.0, 0.0)
    upper_bd = jnp.where(jnp.logical_and(same, ri <= ci), 1.0, 0.0)
    causal_bd = jnp.logical_and(same, ci <= ri)
    diag = ci == ri
    eye = jnp.where(diag, 1.0, 0.0)
    neg_exp_a_col = -jnp.exp(pcol_ref[:, 0:1])
    dt_col = pcol_ref[:, 1:2]
    neg_exp_a_row = -jnp.exp(prow_ref[0:1, :])
    dt_row = prow_ref[1:2, :]
    convw = convw_ref[...]
    first_chunk = lax.broadcasted_iota(jnp.int32, (PAIR, 1), 0) < CHUNK

    def aligned(x, m):
        return x if isinstance(x, int) else pl.multiple_of(x, m)

    def conv_silu(i, p, col):
        slot = i * 3 * B_HEADS + col // B_HEAD_DIM
        r0 = aligned(p * PAIR, PAIR)
        cur = qkv_ref[pl.ds(r0, PAIR), col:col + B_HEAD_DIM].astype(F32)
        lead = BF16_ROWS
        prev0 = (max(r0 - lead, 0) if isinstance(r0, int)
                 else pl.multiple_of(jnp.maximum(r0 - lead, 0), lead))
        prev = qkv_ref[pl.ds(prev0, lead), col:col + B_HEAD_DIM].astype(F32)
        prev = jnp.where(p > 0, prev, 0.0)
        stage = stage_ref.at[slot]
        stage[0:lead, :] = prev
        stage[lead:lead + PAIR, :] = cur
        y = cur * convw[CONV_WIDTH - 1:CONV_WIDTH, col:col + B_HEAD_DIM]
        for back in range(1, CONV_WIDTH):
            sh = stage[lead - back:lead - back + PAIR, :]
            y = y + sh * convw[CONV_WIDTH - 1 - back:CONV_WIDTH - back, col:col + B_HEAD_DIM]
        return _silu(y)

    def l2n(x):
        return x * lax.rsqrt(jnp.sum(x * x, axis=-1, keepdims=True) + EPS)

    heads = range(B_HEADS)
    pairs = range(PAIRS_PER_STEP)
    units = [(i, h) for i in pairs for h in heads]

    def slot_of(step, i):
        return (step % 2) * PAIRS_PER_STEP + i

    def phase1(step):
        ps = [step * PAIRS_PER_STEP + i for i in pairs]
        gc_col, gc_row, beta_col = [], [], []
        for p in ps:
            ab = ab_ref[pl.ds(aligned(p * PAIR, PAIR), PAIR), :]
            g_col = neg_exp_a_row * _softplus(ab + dt_row)
            gc_col.append(_dot(lower_bd, g_col, HI))
            beta_col.append(_sigmoid(ab))
            g_row = neg_exp_a_col * _softplus(abt_ref[p] + dt_col)
            gc_row.append(_dot(g_row, upper_bd, HI))
        q = {(i, h): l2n(conv_silu(i, ps[i], h * B_HEAD_DIM)) * (B_HEAD_DIM ** -0.5) for i, h in units}
        k = {(i, h): l2n(conv_silu(i, ps[i], B_WIDTH + h * B_HEAD_DIM)) for i, h in units}
        v = {(i, h): conv_silu(i, ps[i], 2 * B_WIDTH + h * B_HEAD_DIM) for i, h in units}
        gcol = {(i, h): gc_col[i][:, h:h + 1] for i, h in units}
        beta = {(i, h): beta_col[i][:, B_HEADS + h:B_HEADS + h + 1] for i, h in units}
        kb = {u: k[u] * beta[u] for u in units}
        qk_kk = {u: _dot_nt(jnp.concatenate([q[u], kb[u]], axis=0).astype(BF16), k[u].astype(BF16))
                 for u in units}
        decay = {(i, h): jnp.exp(jnp.where(causal_bd, gcol[i, h] - gc_row[i][h:h + 1, :], -jnp.inf))
                 for i, h in units}
        intra = {u: (qk_kk[u][:PAIR] * decay[u]).astype(BF16) for u in units}
        low = {u: jnp.where(diag, 0.0, qk_kk[u][PAIR:] * decay[u]) for u in units}

        x = {u: eye - low[u] for u in units}
        low16 = {u: low[u].astype(BF16) for u in units}
        pw = {u: _dot(low16[u], low16[u]) for u in units}
        for _ in range(int(math.log2(CHUNK)) - 2):
            pw16 = {u: pw[u].astype(BF16) for u in units}
            r = {u: _dot(jnp.concatenate([x[u].astype(BF16), pw16[u]], axis=0), pw16[u]) for u in units}
            x = {u: x[u] + r[u][:PAIR] for u in units}
            pw = {u: r[u][PAIR:] for u in units}
        tinv = {u: x[u] + _dot(x[u].astype(BF16), pw[u].astype(BF16)) for u in units}

        eg = {u: jnp.exp(gcol[u]) for u in units}
        uw = {u: _dot(tinv[u].astype(BF16),
                      jnp.concatenate([v[u] * beta[u], kb[u] * eg[u]], axis=1).astype(BF16))
              for u in units}
        for i, h in units:
            u, slot = (i, h), slot_of(step, i)
            g_last = jnp.where(first_chunk, gcol[u][CHUNK - 1:CHUNK], gcol[u][PAIR - 1:PAIR])
            kd = k[u] * jnp.exp(g_last - gcol[u])
            u_s[slot, h] = uw[u][:, :B_HEAD_DIM]
            w_s[slot, h] = uw[u][:, B_HEAD_DIM:].astype(BF16)
            qg_s[slot, h] = (q[u] * eg[u]).astype(BF16)
            for cc in range(2):
                rs = slice(cc * CHUNK, (cc + 1) * CHUNK)
                kdt_s[slot, h, cc] = kd[rs].T.astype(BF16)
                a_s[slot, h, cc] = intra[u][rs, rs]
                gl_s[slot, h, cc] = jnp.broadcast_to(
                    jnp.exp(gcol[u][(cc + 1) * CHUNK - 1:(cc + 1) * CHUNK]), (SUBLANES, LANES))

    st_s[...] = jnp.zeros(st_s.shape, F32)
    onorm = onorm_ref[...]

    def phase2(step):
        for i in pairs:
            p, slot = step * PAIRS_PER_STEP + i, slot_of(step, i)
            for cc in range(2):
                rs = slice(cc * CHUNK, (cc + 1) * CHUNK)
                rows = pl.ds(aligned(p * PAIR + cc * CHUNK, CHUNK), CHUNK)
                state = [st_s[h] for h in heads]
                sb = [state[h].astype(BF16) for h in heads]
                wq = [_dot(jnp.concatenate([w_s[slot, h, rs, :], qg_s[slot, h, rs, :]], axis=0), sb[h])
                      for h in heads]
                vb = [(u_s[slot, h, rs, :] - wq[h][:CHUNK]).astype(BF16) for h in heads]
                upd = [_dot(kdt_s[slot, h, cc], vb[h]) for h in heads]
                av = [_dot(a_s[slot, h, cc], vb[h]) for h in heads]
                for h in heads:
                    st_s[h] = state[h] * gl_s[slot, h, cc][0:1, :] + upd[h]
                for h in heads:
                    o = wq[h][CHUNK:] + av[h]
                    z = z_ref[rows, h * B_HEAD_DIM:(h + 1) * B_HEAD_DIM].astype(F32)
                    o_ref[rows, h * B_HEAD_DIM:(h + 1) * B_HEAD_DIM] = (
                        _rms(o, onorm) * _silu(z)).astype(o_ref.dtype)

    def body(step, carry):
        phase2(step - 1)
        phase1(step)
        return carry

    nstep = npair // PAIRS_PER_STEP
    phase1(0)
    lax.fori_loop(1, nstep, body, 0)
    phase2(nstep - 1)


def _gdn(qkv, z, ab, abt, conv_w, a_log, dt_bias, o_norm, batch, seq):
    m = batch * seq
    npair = seq // PAIR
    assert npair % PAIRS_PER_STEP == 0
    nslot = 2 * PAIRS_PER_STEP
    abt3 = jnp.swapaxes(abt.reshape(2 * B_HEADS, m // PAIR, PAIR), 0, 1)
    zeros = jnp.zeros((B_HEADS,), F32)
    alog = jnp.concatenate([a_log.astype(F32), zeros])
    dtb = jnp.concatenate([dt_bias.astype(F32), zeros])
    pcol = jnp.stack([alog, dtb], axis=1)
    prow = jnp.stack([alog, dtb], axis=0)
    const = lambda b: (0, 0)
    rows = lambda b: (b, 0)
    return pl.pallas_call(
        functools.partial(_gdn_kernel, seq=seq),
        grid=(batch,),
        in_specs=[
            pl.BlockSpec((seq, 3 * B_WIDTH), rows),
            pl.BlockSpec((seq, B_WIDTH), rows),
            pl.BlockSpec((seq, 2 * B_HEADS), rows),
            pl.BlockSpec((npair, 2 * B_HEADS, PAIR), lambda b: (b, 0, 0)),
            pl.BlockSpec((CONV_WIDTH, 3 * B_WIDTH), const),
            pl.BlockSpec((2 * B_HEADS, 2), const),
            pl.BlockSpec((2, 2 * B_HEADS), const),
            pl.BlockSpec((1, B_HEAD_DIM), const),
        ],
        out_specs=pl.BlockSpec((seq, B_WIDTH), rows),
        out_shape=jax.ShapeDtypeStruct((m, B_WIDTH), BF16),
        scratch_shapes=[
            pltpu.VMEM((nslot, B_HEADS, PAIR, B_HEAD_DIM), F32),
            pltpu.VMEM((nslot, B_HEADS, PAIR, B_HEAD_DIM), BF16),
            pltpu.VMEM((nslot, B_HEADS, PAIR, B_HEAD_DIM), BF16),
            pltpu.VMEM((nslot, B_HEADS, 2, B_HEAD_DIM, CHUNK), BF16),
            pltpu.VMEM((nslot, B_HEADS, 2, CHUNK, CHUNK), BF16),
            pltpu.VMEM((nslot, B_HEADS, 2, SUBLANES, LANES), F32),
            pltpu.VMEM((B_HEADS, B_HEAD_DIM, B_HEAD_DIM), F32),
            pltpu.VMEM((PAIRS_PER_STEP * 3 * B_HEADS, BF16_ROWS + PAIR, B_HEAD_DIM), F32),
        ],
        compiler_params=pltpu.CompilerParams(
            dimension_semantics=("arbitrary",), vmem_limit_bytes=VMEM_LIMIT),
        name="gdn",
    )(qkv, z, ab, abt3, conv_w.astype(F32), pcol, prow, o_norm.reshape(1, B_HEAD_DIM).astype(F32))


def _out_mlp_kernel(x_ref, oa_ref, ob_ref, wo_ref, g_post_mix_ref, g_pre_mlp_ref, w1_ref, w2_ref,
                    g_post_mlp_ref, o_ref, *, ff_chunk):
    mix = _dot(oa_ref[...], wo_ref[:A_WIDTH, :]) + _dot(ob_ref[...], wo_ref[A_WIDTH:, :])
    x1 = x_ref[...] + _rms(mix, g_post_mix_ref[...])
    h = _rms(x1, g_pre_mlp_ref[...]).astype(BF16)
    d_ff = w1_ref.shape[1]
    y = jnp.zeros(x1.shape, F32)
    for lo in range(0, d_ff, ff_chunk):
        a = jnp.maximum(_dot(h, w1_ref[:, lo:lo + ff_chunk]), 0.0)
        y = y + _dot((a * a).astype(BF16), w2_ref[lo:lo + ff_chunk, :])
    o_ref[...] = x1 + _rms(y, g_post_mlp_ref[...])


def _out_mlp(x2, oa, ob, w_out, g_post_mix, g_pre_mlp, w1, w2, g_post_mlp, tm, ff_chunk):
    m, d = x2.shape
    d_ff = w1.shape[1]
    const = lambda i: (0, 0)
    row = lambda i: (i, 0)
    once = dict(pipeline_mode=pl.Buffered(1))
    vec = lambda g: g.reshape(1, d).astype(F32)
    return pl.pallas_call(
        functools.partial(_out_mlp_kernel, ff_chunk=ff_chunk),
        grid=(m // tm,),
        in_specs=[
            pl.BlockSpec((tm, d), row),
            pl.BlockSpec((tm, A_WIDTH), row),
            pl.BlockSpec((tm, B_WIDTH), row),
            pl.BlockSpec((A_WIDTH + B_WIDTH, d), const, **once),
            pl.BlockSpec((1, d), const),
            pl.BlockSpec((1, d), const),
            pl.BlockSpec((d, d_ff), const, **once),
            pl.BlockSpec((d_ff, d), const, **once),
            pl.BlockSpec((1, d), const),
        ],
        out_specs=pl.BlockSpec((tm, d), row),
        out_shape=jax.ShapeDtypeStruct((m, d), F32),
        compiler_params=pltpu.CompilerParams(
            dimension_semantics=("arbitrary",), vmem_limit_bytes=VMEM_LIMIT),
        name="out_mlp",
    )(x2, oa, ob, w_out.astype(BF16), vec(g_post_mix), vec(g_pre_mlp),
      w1.astype(BF16), w2.astype(BF16), vec(g_post_mlp))


def kernel(x, w_in, c_norm, w_uk, w_uv, rel_bias, conv_w, a_log, dt_bias, o_norm, w_out,
           pre_norm_mix, post_norm_mix, pre_norm_mlp, post_norm_mlp, w_mlp_in, w_mlp_out):
    batch, seq, d = x.shape
    m = batch * seq
    tm = min(ROW_TILE, m)
    tm_in = min(IN_ROW_TILE, m)
    tq = min(QUERY_TILE, seq)
    assert m % tm == 0 and m % tm_in == 0 and seq % tq == 0 and seq % (PAIR * PAIRS_PER_STEP) == 0
    xs = x.reshape(m, d)
    for l in range(w_in.shape[0]):
        qat, qit, qkv, z, c, ct3, kidx, wit, ab, abt = _in_proj(
            xs, pre_norm_mix[l], w_in[l], c_norm[l], tm_in, tq)
        oa = _dsa(qat, qit, wit, kidx, c, ct3, w_uk[l], w_uv[l], rel_bias, batch, seq, tq, N_BISECT)
        ob = _gdn(qkv, z, ab, abt, conv_w[l], a_log[l], dt_bias[l], o_norm[l], batch, seq)
        xs = _out_mlp(xs, oa, ob, w_out[l], post_norm_mix[l], pre_norm_mlp[l],
                      w_mlp_in[l], w_mlp_out[l], post_norm_mlp[l], tm, FF_CHUNK)
    return xs.reshape(batch, seq, d)
```

```python
import functools
import math

import numpy as np
import jax
import jax.numpy as jnp
from jax import lax
from jax.experimental import pallas as pl
from jax.experimental.pallas import tpu as pltpu

F32 = jnp.float32
BF16 = jnp.bfloat16

EPS = 1e-6
CHUNK = 64
A_HEADS = 8
A_HEAD_DIM = 64
A_WIDTH = A_HEADS * A_HEAD_DIM
A_LATENT = 128
IDX_HEADS = 8
IDX_DIM = 64
TOPK_MAX = 256
REL_BUCKETS = 32
REL_MAX_DIST = 128
B_HEADS = 4
B_HEAD_DIM = 128
B_WIDTH = B_HEADS * B_HEAD_DIM
CONV_WIDTH = 4
IN_SPLITS = (A_WIDTH, A_LATENT, IDX_HEADS * IDX_DIM, IDX_DIM, IDX_HEADS,
             B_WIDTH, B_WIDTH, B_WIDTH, B_HEADS, B_HEADS, B_WIDTH)

LANES = 128
SUBLANES = 8
BF16_ROWS = 16
VMEM_LIMIT = 56 * 1024 * 1024
ROW_TILE = 1024
IN_ROW_TILE = 1024
QUERY_TILE = 256
FF_CHUNK = 1024
N_COARSE = 10
N_BISECT = 7
NEG_BIG = -1e30
LOG2E = 1.0 / math.log(2.0)
DENOM_ROWS = 16
FINISH_UNROLL = 2

NT_DIMS = (((1,), (1,)), ((), ()))


def _dot(a, b, precision=None):
    return jnp.dot(a, b, preferred_element_type=F32, precision=precision)


def _dot_nt(a, b, precision=None):
    return lax.dot_general(a, b, NT_DIMS, preferred_element_type=F32, precision=precision)


def _rms(x, g):
    return x * lax.rsqrt(jnp.mean(x * x, axis=-1, keepdims=True) + EPS) * g


def _sigmoid(x):
    return 1.0 / (1.0 + jnp.exp(-x))


def _silu(x):
    half = 0.5 * x
    return half + half * jnp.tanh(half)


def _softplus(x):
    return jnp.maximum(x, 0.0) + jnp.log1p(jnp.exp(-jnp.abs(x)))


_C_QKV = 0
_C_Z = _C_QKV + 3 * B_WIDTH
_C_CKV = _C_Z + B_WIDTH
_C_SMALL = _C_CKV + A_LATENT
_C_END = _C_SMALL + LANES
_R_QA = 0
_R_QI = _R_QA + A_WIDTH
_R_CKV = _R_QI + IDX_HEADS * IDX_DIM
_R_SMALL = _R_CKV + A_LATENT
_R_END = _R_SMALL + IDX_HEADS + 2 * B_HEADS


def _in_proj_kernel(x_ref, g_ref, w_ref, wt_ref, cn_ref, cnt_ref,
                    qat_ref, qit_ref, qkv_ref, z_ref, c_ref, ct_ref, kidx_ref, wit_ref, ab_ref, abt_ref,
                    *, ts):
    h = _rms(x_ref[...], g_ref[...]).astype(BF16)

    def proj(lo, hi):
        return _dot(h, w_ref[:, lo:hi])

    def proj_t(lo, hi):
        return _dot_nt(wt_ref[lo:hi, :], h)

    qkv_ref[...] = proj(_C_QKV, _C_Z).astype(BF16)
    z_ref[...] = proj(_C_Z, _C_CKV).astype(BF16)
    c_ref[...] = _rms(proj(_C_CKV, _C_SMALL), cn_ref[...]).astype(BF16)
    small = proj(_C_SMALL, _C_END)
    kidx_ref[...] = small[:, :IDX_DIM].astype(BF16)
    ab_ref[...] = small[:, IDX_DIM:IDX_DIM + 2 * B_HEADS]

    qat_ref[...] = proj_t(_R_QA, _R_QI).astype(BF16)
    qit_ref[...] = (proj_t(_R_QI, _R_CKV) * (IDX_DIM ** -0.5)).astype(BF16)
    ckv_t = proj_t(_R_CKV, _R_SMALL)
    c_t = (ckv_t * lax.rsqrt(jnp.mean(ckv_t * ckv_t, axis=0, keepdims=True) + EPS)
           * cnt_ref[...]).astype(BF16)
    for r in range(ct_ref.shape[0]):
        ct_ref[r] = c_t[:, r * ts:(r + 1) * ts]
    small_t = proj_t(_R_SMALL, _R_END)
    wit_ref[...] = small_t[:IDX_HEADS] * (IDX_HEADS ** -0.5)
    abt_ref[...] = small_t[IDX_HEADS:]


def _in_proj(x2, g, w_in, c_norm, tm, ts):
    m, d = x2.shape
    offs = np.cumsum((0,) + IN_SPLITS)
    seg = [w_in[:, offs[i]:offs[i + 1]] for i in range(len(IN_SPLITS))]
    q_a, c_kv, q_idx, k_idx, w_idx, q_b, k_b, v_b, a_b, b_b, z_b = seg
    pad = jnp.zeros((d, LANES - IDX_DIM - 2 * B_HEADS), w_in.dtype)
    w_all = jnp.concatenate([q_b, k_b, v_b, z_b, c_kv, k_idx, a_b, b_b, pad], axis=1).astype(BF16)
    w_t = jnp.concatenate([q_a, q_idx, c_kv, w_idx, a_b, b_b], axis=1).T.astype(BF16)
    const = lambda i: (0, 0)
    row = lambda i: (i, 0)
    col = lambda i: (0, i)
    out_shape = (
        jax.ShapeDtypeStruct((A_WIDTH, m), BF16),
        jax.ShapeDtypeStruct((IDX_HEADS * IDX_DIM, m), BF16),
        jax.ShapeDtypeStruct((m, 3 * B_WIDTH), BF16),
        jax.ShapeDtypeStruct((m, B_WIDTH), BF16),
        jax.ShapeDtypeStruct((m, A_LATENT), BF16),
        jax.ShapeDtypeStruct((m // ts, A_LATENT, ts), BF16),
        jax.ShapeDtypeStruct((m, IDX_DIM), BF16),
        jax.ShapeDtypeStruct((IDX_HEADS, m), F32),
        jax.ShapeDtypeStruct((m, 2 * B_HEADS), F32),
        jax.ShapeDtypeStruct((2 * B_HEADS, m), F32),
    )
    out_specs = (
        pl.BlockSpec((A_WIDTH, tm), col),
        pl.BlockSpec((IDX_HEADS * IDX_DIM, tm), col),
        pl.BlockSpec((tm, 3 * B_WIDTH), row),
        pl.BlockSpec((tm, B_WIDTH), row),
        pl.BlockSpec((tm, A_LATENT), row),
        pl.BlockSpec((tm // ts, A_LATENT, ts), lambda i: (i, 0, 0)),
        pl.BlockSpec((tm, IDX_DIM), row),
        pl.BlockSpec((IDX_HEADS, tm), col),
        pl.BlockSpec((tm, 2 * B_HEADS), row),
        pl.BlockSpec((2 * B_HEADS, tm), col),
    )
    return pl.pallas_call(
        functools.partial(_in_proj_kernel, ts=ts),
        grid=(m // tm,),
        in_specs=[
            pl.BlockSpec((tm, d), row),
            pl.BlockSpec((1, d), const),
            pl.BlockSpec((d, _C_END), const),
            pl.BlockSpec((_R_END, d), const),
            pl.BlockSpec((1, A_LATENT), const),
            pl.BlockSpec((A_LATENT, 1), const),
        ],
        out_specs=out_specs,
        out_shape=out_shape,
        compiler_params=pltpu.CompilerParams(
            dimension_semantics=("arbitrary",), vmem_limit_bytes=VMEM_LIMIT),
        name="in_proj",
    )(x2, g.reshape(1, d), w_all, w_t, c_norm.reshape(1, A_LATENT), c_norm.reshape(A_LATENT, 1))


def _fold_rows(x, op, stop=SUBLANES):
    r = x.shape[0]
    while r > stop:
        r //= 2
        x = op(x[:r], x[r:])
    return x


def _dsa_kernel(qat_ref, qit_ref, wit_ref, kidx_ref, c_ref, ct_ref, wuk_ref, wuvt_ref, bias_ref, o_ref,
                sc_ref, qabs_ref, m_ref, acc_ref, tie_base_ref, range_ref, sc16_ref, *, tq, ts, topk,
                ncoarse, nbisect, tile):
    i = tile
    nkb = i + 1
    q_pos = i * tq + lax.broadcasted_iota(jnp.int32, (1, tq), 1)
    limit = (q_pos // CHUNK + 1) * CHUNK
    kf = float(topk)
    ninf = -jnp.inf

    def over_blocks(body, carry):
        for j in range(nkb):
            carry = body(j, carry)
        return carry

    def key_ids(j):
        return j * ts + lax.broadcasted_iota(jnp.int32, (ts, 1), 0)

    def col_reduce(x8, op):
        return op(x8, axis=0, keepdims=True)

    wit = wit_ref[...]

    range_ref[0] = jnp.full((SUBLANES, tq), jnp.inf, F32)
    range_ref[1] = jnp.full((SUBLANES, tq), ninf, F32)

    def score_blocks(js, diagonal):
        k = jnp.concatenate([kidx_ref[j] for j in js], axis=0)
        r = [_dot(k, qit_ref[h * IDX_DIM:(h + 1) * IDX_DIM, :]) for h in range(IDX_HEADS)]
        s = jnp.maximum(r[0], 0.0) * wit[0:1, :]
        for h in range(1, IDX_HEADS):
            s = s + jnp.maximum(r[h], 0.0) * wit[h:h + 1, :]
        lo, hi = s, s
        if diagonal:
            adm = key_ids(js[0]) < limit
            s = jnp.where(adm, s, ninf)
            lo, hi = jnp.where(adm, s, jnp.inf), s
        for n, j in enumerate(js):
            sc_ref[j] = s[n * ts:(n + 1) * ts]
            sc16_ref[j] = s[n * ts:(n + 1) * ts].astype(BF16)
        range_ref[0] = jnp.minimum(range_ref[0], _fold_rows(lo, jnp.minimum))
        range_ref[1] = jnp.maximum(range_ref[1], _fold_rows(hi, jnp.maximum))

    for jj in range(i // 2):
        score_blocks([2 * jj, 2 * jj + 1], False)
    if i % 2 == 1:
        score_blocks([i - 1], False)
    score_blocks([i], True)
    rmin = col_reduce(range_ref[0], jnp.min)
    rmax = col_reduce(range_ref[1], jnp.max)

    def count(pred):
        def body(j, acc):
            return acc + _fold_rows(jnp.where(pred(sc_ref[j], j), 1.0, 0.0), jnp.add)
        acc = over_blocks(body, jnp.zeros((SUBLANES, tq), F32))
        return col_reduce(acc, jnp.sum)

    def count_ge(v):
        return count(lambda s, j: s >= v)

    need = limit > topk

    def bisect_body(_, carry):
        lo, hi, n_hi = carry
        mid = lo + 0.5 * (hi - lo)
        c = count_ge(mid)
        ge = c >= kf
        return jnp.where(ge, mid, lo), jnp.where(ge, hi, mid), jnp.where(ge, n_hi, c)

    def count16_ge(v16):
        def body(j, acc):
            hit = jnp.where(sc16_ref[j] >= v16, jnp.ones((), BF16), jnp.zeros((), BF16))
            return acc + _fold_rows(hit, jnp.add, stop=BF16_ROWS).astype(F32)
        acc = over_blocks(body, jnp.zeros((BF16_ROWS, tq), F32))
        return col_reduce(acc, jnp.sum)

    def coarse_body(_, carry):
        a, b = carry
        mid = (a + 0.5 * (b - a)).astype(BF16)
        ge = count16_ge(mid) >= kf
        mid = mid.astype(F32)
        return jnp.where(ge, mid, a), jnp.where(ge, b, mid)


    def max_below(v):
        def body(j, acc):
            s = sc_ref[j]
            return jnp.maximum(acc, _fold_rows(jnp.where(s < v, s, ninf), jnp.maximum))
        return col_reduce(over_blocks(body, jnp.full((SUBLANES, tq), ninf, F32)), jnp.max)

    def finish_cond(state):
        return jnp.min(state[-1]) < 0.5

    def finish_body(state):
        m, n_hi, thr, n_thr, done = state

        def body(j, carry):
            cnt, nxt = carry
            s = sc_ref[j]
            below = s < m
            return (cnt + _fold_rows(jnp.where(below, 0.0, 1.0), jnp.add),
                    jnp.maximum(nxt, _fold_rows(jnp.where(below, s, ninf), jnp.maximum)))

        cnt, nxt = over_blocks(body, (jnp.zeros((SUBLANES, tq), F32),
                                               jnp.full((SUBLANES, tq), ninf, F32)))
        c = col_reduce(cnt, jnp.sum)
        open_ = done < 0.5
        hit = jnp.logical_and(open_, c >= kf)
        miss = jnp.logical_and(open_, c < kf)
        thr = jnp.where(hit, m, thr)
        n_thr = jnp.where(hit, c, n_thr)
        n_hi = jnp.where(miss, c, n_hi)
        m = jnp.where(miss, col_reduce(nxt, jnp.max), m)
        return m, n_hi, thr, n_thr, jnp.where(hit, 1.0, done)

    lowest = float(jnp.finfo(F32).min)
    zero = jnp.zeros((1, tq), F32)
    keep_all = jnp.full((1, tq), lowest, F32)

    def search(_):
        a0 = rmin.astype(BF16).astype(F32)
        top = rmax.astype(BF16).astype(F32)
        b0 = (top + jnp.abs(top) * 2.0 ** -6 + 1e-30).astype(BF16).astype(F32)
        a, b = lax.fori_loop(0, ncoarse, coarse_body, (a0, b0))
        lo0 = a - jnp.abs(a) * 2.0 ** -7 - 1e-30
        _, hi, n_hi = lax.fori_loop(0, nbisect, bisect_body, (lo0, b, count_ge(b)))
        state = (max_below(hi), n_hi, keep_all, zero, jnp.where(need, 0.0, 1.0))
        for _ in range(FINISH_UNROLL):
            state = finish_body(state)
        _, n_hi, thr, n_thr, _ = lax.while_loop(finish_cond, finish_body, state)
        return n_hi, thr, n_thr

    n_hi, thr, n_thr = search(0) if (i + 1) * tq > topk else (zero, keep_all, zero)

    n_tie = jnp.where(need, kf - n_hi, 0.0)
    surplus = jnp.max(jnp.where(jnp.logical_and(need, n_thr - n_hi > n_tie), 1.0, 0.0)) > 0.5
    tie_base_ref[...] = zero
    kr = lax.broadcasted_iota(jnp.int32, (ts, ts), 0)
    kc = lax.broadcasted_iota(jnp.int32, (ts, ts), 1)

    def mask_bias(j):
        def plain(_):
            return jnp.where(sc_ref[j] >= thr, 0.0, NEG_BIG)

        def with_ties(_):
            s = sc_ref[j]
            tie = s == thr
            prefix_ones = jnp.where(kc <= kr, 1.0, 0.0).astype(BF16)
            rank = _dot(prefix_ones, jnp.where(tie, 1.0, 0.0).astype(BF16)) + tie_base_ref[...]
            tie_base_ref[...] = rank[ts - 1:ts, :]
            keep = jnp.where(tie, jnp.where(rank <= n_tie, 0.0, NEG_BIG), NEG_BIG)
            return jnp.where(s > thr, 0.0, keep)

        return lax.cond(surplus, with_ties, plain, 0)

    for h in range(A_HEADS):
        qh = _dot(wuk_ref[h], qat_ref[h * A_HEAD_DIM:(h + 1) * A_HEAD_DIM, :])
        qabs_ref[h] = (qh * (A_HEAD_DIM ** -0.5 * LOG2E)).astype(BF16)
    m_ref[...] = jnp.full(m_ref.shape, NEG_BIG, F32)
    acc_ref[...] = jnp.zeros(acc_ref.shape, F32)
    heads = range(A_HEADS)

    def attend(js, bias_of_head):
        n = len(js) * ts
        cj = jnp.concatenate([c_ref[j] for j in js], axis=0)
        ctj = jnp.concatenate([jnp.concatenate([ct_ref[j] for j in js], axis=1),
                               jnp.ones((DENOM_ROWS, n), BF16)], axis=0)
        mb = jnp.concatenate([mask_bias(j) for j in js], axis=0)
        s = [_dot(cj, qabs_ref[h]) for h in heads]
        for h in heads:
            b = bias_of_head(h)
            s[h] = s[h] + (mb if b is None else mb + b)
        m_old = [m_ref[h] for h in heads]
        m_new = [jnp.maximum(m_old[h], col_reduce(_fold_rows(s[h], jnp.maximum), jnp.max))
                 for h in heads]
        alpha = [jnp.exp2(m_old[h] - m_new[h]) for h in heads]
        p = [jnp.exp2(s[h] - m_new[h]).astype(BF16) for h in heads]
        pv = [_dot(ctj, p[h]) for h in heads]
        for h in heads:
            acc_ref[h] = alpha[h] * acc_ref[h] + pv[h]
            m_ref[h] = m_new[h]

    n_far = max(i - 1, 0)
    for jj in range(n_far // 2):
        attend([2 * jj, 2 * jj + 1], lambda h: None)
    if n_far % 2 == 1:
        attend([n_far - 1], lambda h: None)
    if i > 0:
        attend([i - 1, i], lambda h: bias_ref[h])
    else:
        attend([i], lambda h: bias_ref[h, ts:, :])

    outs = [_dot(wuvt_ref[h],
                 (acc_ref[h, :A_LATENT, :] / acc_ref[h, A_LATENT:A_LATENT + 1, :]).astype(BF16))
            for h in range(A_HEADS)]
    o_ref[...] = jnp.concatenate(outs, axis=0).T.astype(o_ref.dtype)


def _t5_bucket(rel):
    nb = REL_BUCKETS // 2
    max_exact = nb // 2
    side = jnp.where(rel > 0, nb, 0)
    n = jnp.abs(rel)
    nf = jnp.maximum(n, 1).astype(jnp.float32)
    large = max_exact + (jnp.log(nf / max_exact) / math.log(REL_MAX_DIST / max_exact)
                         * (nb - max_exact)).astype(jnp.int32)
    large = jnp.minimum(large, nb - 1)
    return side + jnp.where(n < max_exact, n, large)


def _near_bias_kernel(far_ref, rb_ref, bucket_ref, o_ref):
    bucket = bucket_ref[...]
    for h in range(A_HEADS):
        acc = jnp.zeros(bucket.shape, F32)
        for b in range(REL_BUCKETS):
            acc = jnp.where(bucket == b, rb_ref[b, h], acc)
        o_ref[h] = (acc - rb_ref[far_ref[0], h]) * LOG2E


def _near_bias(rel_bias, tq, ts):
    assert ts >= REL_MAX_DIST
    rel = (jnp.arange(2 * ts)[:, None] - ts) - jnp.arange(tq)[None, :]
    far = _t5_bucket(jnp.full((1,), -(ts + 1), jnp.int32))
    smem = pl.BlockSpec(memory_space=pltpu.SMEM)
    return pl.pallas_call(
        _near_bias_kernel,
        in_specs=[smem, smem, pl.BlockSpec(memory_space=pltpu.VMEM)],
        out_specs=pl.BlockSpec(memory_space=pltpu.VMEM),
        out_shape=jax.ShapeDtypeStruct((A_HEADS, 2 * ts, tq), F32),
        name="near_bias",
    )(far, rel_bias.astype(F32), _t5_bucket(rel).astype(jnp.int32))


def _dsa(qat, qit, wit, kidx, c, ct3, w_uk, w_uv, rel_bias, batch, seq, tq, nbisect):
    ts = tq
    m = batch * seq
    nq = seq // tq
    nks = seq // ts
    topk = min(TOPK_MAX, seq // 4)
    kidx3 = kidx.reshape(m // ts, ts, IDX_DIM)
    c3 = c.reshape(m // ts, ts, A_LATENT)
    wuk = w_uk.astype(BF16)
    wuv_t = jnp.swapaxes(w_uv, 1, 2).astype(BF16)
    bias = _near_bias(rel_bias, tq, ts)
    kv = lambda b: (b, 0, 0)
    const3 = lambda b: (0, 0, 0)
    outs = [_dsa_tile(i, qat, qit, wit, kidx3, c3, ct3, wuk, wuv_t, bias, kv, const3,
                      batch, nq, nks, tq, ts, topk, nbisect) for i in range(nq)]
    return jnp.stack(outs, axis=1).reshape(m, A_WIDTH)


def _dsa_tile(i, qat, qit, wit, kidx3, c3, ct3, wuk, wuv_t, bias, kv, const3,
              batch, nq, nks, tq, ts, topk, nbisect):
    qcol = lambda b: (0, b * nq + i)
    kern = functools.partial(_dsa_kernel, tq=tq, ts=ts, topk=topk, ncoarse=N_COARSE, nbisect=nbisect,
                             tile=i)
    return pl.pallas_call(
        kern,
        grid=(batch,),
        in_specs=[
            pl.BlockSpec((A_WIDTH, tq), qcol),
            pl.BlockSpec((IDX_HEADS * IDX_DIM, tq), qcol),
            pl.BlockSpec((IDX_HEADS, tq), qcol),
            pl.BlockSpec((nks, ts, IDX_DIM), kv),
            pl.BlockSpec((nks, ts, A_LATENT), kv),
            pl.BlockSpec((nks, A_LATENT, ts), kv),
            pl.BlockSpec((A_HEADS, A_LATENT, A_HEAD_DIM), const3),
            pl.BlockSpec((A_HEADS, A_HEAD_DIM, A_LATENT), const3),
            pl.BlockSpec((A_HEADS, 2 * ts, tq), const3),
        ],
        out_specs=pl.BlockSpec((None, tq, A_WIDTH), lambda b: (b, 0, 0)),
        out_shape=jax.ShapeDtypeStruct((batch, tq, A_WIDTH), BF16),
        scratch_shapes=[
            pltpu.VMEM((nks, ts, tq), F32),
            pltpu.VMEM((A_HEADS, A_LATENT, tq), BF16),
            pltpu.VMEM((A_HEADS, 1, tq), F32),
            pltpu.VMEM((A_HEADS, A_LATENT + DENOM_ROWS, tq), F32),
            pltpu.VMEM((1, tq), F32),
            pltpu.VMEM((2, SUBLANES, tq), F32),
            pltpu.VMEM((nks, ts, tq), BF16),
        ],
        compiler_params=pltpu.CompilerParams(
            dimension_semantics=("arbitrary",), vmem_limit_bytes=VMEM_LIMIT),
        name=f"dsa_t{i}",
    )(qat, qit, wit, kidx3, c3, ct3, wuk, wuv_t, bias)


PAIR = 2 * CHUNK
PAIRS_PER_STEP = 2
HI = lax.Precision.HIGHEST


def _gdn_kernel(qkv_ref, z_ref, ab_ref, abt_ref, convw_ref, pcol_ref, prow_ref, onorm_ref, o_ref,
                u_s, w_s, qg_s, kdt_s, a_s, gl_s, st_s, stage_ref, *, seq):
    npair = seq // PAIR
    ri = lax.broadcasted_iota(jnp.int32, (PAIR, PAIR), 0)
    ci = lax.broadcasted_iota(jnp.int32, (PAIR, PAIR), 1)
    same = (ri // CHUNK) == (ci // CHUNK)
    lower_bd = jnp.where(jnp.logical_and(same, ci <= ri), 1.0, 0.0)
    upper_bd = jnp.where(jnp.logical_and(same, ri <= ci), 1.0, 0.0)
    causal_bd = jnp.logical_and(same, ci <= ri)
    diag = ci == ri
    eye = jnp.where(diag, 1.0, 0.0)
    neg_exp_a_col = -jnp.exp(pcol_ref[:, 0:1])
    dt_col = pcol_ref[:, 1:2]
    neg_exp_a_row = -jnp.exp(prow_ref[0:1, :])
    dt_row = prow_ref[1:2, :]
    convw = convw_ref[...]
    first_chunk = lax.broadcasted_iota(jnp.int32, (PAIR, 1), 0) < CHUNK

    def aligned(x, m):
        return x if isinstance(x, int) else pl.multiple_of(x, m)

    def conv_silu(i, p, col):
        slot = i * 3 * B_HEADS + col // B_HEAD_DIM
        r0 = aligned(p * PAIR, PAIR)
        cur = qkv_ref[pl.ds(r0, PAIR), col:col + B_HEAD_DIM].astype(F32)
        lead = BF16_ROWS
        prev0 = (max(r0 - lead, 0) if isinstance(r0, int)
                 else pl.multiple_of(jnp.maximum(r0 - lead, 0), lead))
        prev = qkv_ref[pl.ds(prev0, lead), col:col + B_HEAD_DIM].astype(F32)
        prev = jnp.where(p > 0, prev, 0.0)
        stage = stage_ref.at[slot]
        stage[0:lead, :] = prev
        stage[lead:lead + PAIR, :] = cur
        y = cur * convw[CONV_WIDTH - 1:CONV_WIDTH, col:col + B_HEAD_DIM]
        for back in range(1, CONV_WIDTH):
            sh = stage[lead - back:lead - back + PAIR, :]
            y = y + sh * convw[CONV_WIDTH - 1 - back:CONV_WIDTH - back, col:col + B_HEAD_DIM]
        return _silu(y)

    def l2n(x):
        return x * lax.rsqrt(jnp.sum(x * x, axis=-1, keepdims=True) + EPS)

    heads = range(B_HEADS)
    pairs = range(PAIRS_PER_STEP)
    units = [(i, h) for i in pairs for h in heads]

    def slot_of(step, i):
        return (step % 2) * PAIRS_PER_STEP + i

    def phase1(step):
        ps = [step * PAIRS_PER_STEP + i for i in pairs]
        gc_col, gc_row, beta_col = [], [], []
        for p in ps:
            ab = ab_ref[pl.ds(aligned(p * PAIR, PAIR), PAIR), :]
            g_col = neg_exp_a_row * _softplus(ab + dt_row)
            gc_col.append(_dot(lower_bd, g_col, HI))
            beta_col.append(_sigmoid(ab))
            g_row = neg_exp_a_col * _softplus(abt_ref[p] + dt_col)
            gc_row.append(_dot(g_row, upper_bd, HI))
        q = {(i, h): l2n(conv_silu(i, ps[i], h * B_HEAD_DIM)) * (B_HEAD_DIM ** -0.5) for i, h in units}
        k = {(i, h): l2n(conv_silu(i, ps[i], B_WIDTH + h * B_HEAD_DIM)) for i, h in units}
        v = {(i, h): conv_silu(i, ps[i], 2 * B_WIDTH + h * B_HEAD_DIM) for i, h in units}
        gcol = {(i, h): gc_col[i][:, h:h + 1] for i, h in units}
        beta = {(i, h): beta_col[i][:, B_HEADS + h:B_HEADS + h + 1] for i, h in units}
        kb = {u: k[u] * beta[u] for u in units}
        qk_kk = {u: _dot_nt(jnp.concatenate([q[u], kb[u]], axis=0).astype(BF16), k[u].astype(BF16))
                 for u in units}
        decay = {(i, h): jnp.exp(jnp.where(causal_bd, gcol[i, h] - gc_row[i][h:h + 1, :], -jnp.inf))
                 for i, h in units}
        intra = {u: (qk_kk[u][:PAIR] * decay[u]).astype(BF16) for u in units}
        low = {u: jnp.where(diag, 0.0, qk_kk[u][PAIR:] * decay[u]) for u in units}

        x = {u: eye - low[u] for u in units}
        low16 = {u: low[u].astype(BF16) for u in units}
        pw = {u: _dot(low16[u], low16[u]) for u in units}
        for _ in range(int(math.log2(CHUNK)) - 2):
            pw16 = {u: pw[u].astype(BF16) for u in units}
            r = {u: _dot(jnp.concatenate([x[u].astype(BF16), pw16[u]], axis=0), pw16[u]) for u in units}
            x = {u: x[u] + r[u][:PAIR] for u in units}
            pw = {u: r[u][PAIR:] for u in units}
        tinv = {u: x[u] + _dot(x[u].astype(BF16), pw[u].astype(BF16)) for u in units}

        eg = {u: jnp.exp(gcol[u]) for u in units}
        uw = {u: _dot(tinv[u].astype(BF16),
                      jnp.concatenate([v[u] * beta[u], kb[u] * eg[u]], axis=1).astype(BF16))
              for u in units}
        for i, h in units:
            u, slot = (i, h), slot_of(step, i)
            g_last = jnp.where(first_chunk, gcol[u][CHUNK - 1:CHUNK], gcol[u][PAIR - 1:PAIR])
            kd = k[u] * jnp.exp(g_last - gcol[u])
            u_s[slot, h] = uw[u][:, :B_HEAD_DIM]
            w_s[slot, h] = uw[u][:, B_HEAD_DIM:].astype(BF16)
            qg_s[slot, h] = (q[u] * eg[u]).astype(BF16)
            for cc in range(2):
                rs = slice(cc * CHUNK, (cc + 1) * CHUNK)
                kdt_s[slot, h, cc] = kd[rs].T.astype(BF16)
                a_s[slot, h, cc] = intra[u][rs, rs]
                gl_s[slot, h, cc] = jnp.broadcast_to(
                    jnp.exp(gcol[u][(cc + 1) * CHUNK - 1:(cc + 1) * CHUNK]), (SUBLANES, LANES))

    st_s[...] = jnp.zeros(st_s.shape, F32)
    onorm = onorm_ref[...]

    def phase2(step):
        for i in pairs:
            p, slot = step * PAIRS_PER_STEP + i, slot_of(step, i)
            for cc in range(2):
                rs = slice(cc * CHUNK, (cc + 1) * CHUNK)
                rows = pl.ds(aligned(p * PAIR + cc * CHUNK, CHUNK), CHUNK)
                state = [st_s[h] for h in heads]
                sb = [state[h].astype(BF16) for h in heads]
                wq = [_dot(jnp.concatenate([w_s[slot, h, rs, :], qg_s[slot, h, rs, :]], axis=0), sb[h])
                      for h in heads]
                vb = [(u_s[slot, h, rs, :] - wq[h][:CHUNK]).astype(BF16) for h in heads]
                upd = [_dot(kdt_s[slot, h, cc], vb[h]) for h in heads]
                av = [_dot(a_s[slot, h, cc], vb[h]) for h in heads]
                for h in heads:
                    st_s[h] = state[h] * gl_s[slot, h, cc][0:1, :] + upd[h]
                for h in heads:
                    o = wq[h][CHUNK:] + av[h]
                    z = z_ref[rows, h * B_HEAD_DIM:(h + 1) * B_HEAD_DIM].astype(F32)
                    o_ref[rows, h * B_HEAD_DIM:(h + 1) * B_HEAD_DIM] = (
                        _rms(o, onorm) * _silu(z)).astype(o_ref.dtype)

    def body(step, carry):
        phase2(step - 1)
        phase1(step)
        return carry

    nstep = npair // PAIRS_PER_STEP
    phase1(0)
    lax.fori_loop(1, nstep, body, 0)
    phase2(nstep - 1)


def _gdn(qkv, z, ab, abt, conv_w, a_log, dt_bias, o_norm, batch, seq):
    m = batch * seq
    npair = seq // PAIR
    assert npair % PAIRS_PER_STEP == 0
    nslot = 2 * PAIRS_PER_STEP
    abt3 = jnp.swapaxes(abt.reshape(2 * B_HEADS, m // PAIR, PAIR), 0, 1)
    zeros = jnp.zeros((B_HEADS,), F32)
    alog = jnp.concatenate([a_log.astype(F32), zeros])
    dtb = jnp.concatenate([dt_bias.astype(F32), zeros])
    pcol = jnp.stack([alog, dtb], axis=1)
    prow = jnp.stack([alog, dtb], axis=0)
    const = lambda b: (0, 0)
    rows = lambda b: (b, 0)
    return pl.pallas_call(
        functools.partial(_gdn_kernel, seq=seq),
        grid=(batch,),
        in_specs=[
            pl.BlockSpec((seq, 3 * B_WIDTH), rows),
            pl.BlockSpec((seq, B_WIDTH), rows),
            pl.BlockSpec((seq, 2 * B_HEADS), rows),
            pl.BlockSpec((npair, 2 * B_HEADS, PAIR), lambda b: (b, 0, 0)),
            pl.BlockSpec((CONV_WIDTH, 3 * B_WIDTH), const),
            pl.BlockSpec((2 * B_HEADS, 2), const),
            pl.BlockSpec((2, 2 * B_HEADS), const),
            pl.BlockSpec((1, B_HEAD_DIM), const),
        ],
        out_specs=pl.BlockSpec((seq, B_WIDTH), rows),
        out_shape=jax.ShapeDtypeStruct((m, B_WIDTH), BF16),
        scratch_shapes=[
            pltpu.VMEM((nslot, B_HEADS, PAIR, B_HEAD_DIM), F32),
            pltpu.VMEM((nslot, B_HEADS, PAIR, B_HEAD_DIM), BF16),
            pltpu.VMEM((nslot, B_HEADS, PAIR, B_HEAD_DIM), BF16),
            pltpu.VMEM((nslot, B_HEADS, 2, B_HEAD_DIM, CHUNK), BF16),
            pltpu.VMEM((nslot, B_HEADS, 2, CHUNK, CHUNK), BF16),
            pltpu.VMEM((nslot, B_HEADS, 2, SUBLANES, LANES), F32),
            pltpu.VMEM((B_HEADS, B_HEAD_DIM, B_HEAD_DIM), F32),
            pltpu.VMEM((PAIRS_PER_STEP * 3 * B_HEADS, BF16_ROWS + PAIR, B_HEAD_DIM), F32),
        ],
        compiler_params=pltpu.CompilerParams(
            dimension_semantics=("arbitrary",), vmem_limit_bytes=VMEM_LIMIT),
        name="gdn",
    )(qkv, z, ab, abt3, conv_w.astype(F32), pcol, prow, o_norm.reshape(1, B_HEAD_DIM).astype(F32))


def _out_mlp_kernel(x_ref, oa_ref, ob_ref, wo_ref, g_post_mix_ref, g_pre_mlp_ref, w1_ref, w2_ref,
                    g_post_mlp_ref, o_ref, *, ff_chunk):
    mix = _dot(oa_ref[...], wo_ref[:A_WIDTH, :]) + _dot(ob_ref[...], wo_ref[A_WIDTH:, :])
    x1 = x_ref[...] + _rms(mix, g_post_mix_ref[...])
    h = _rms(x1, g_pre_mlp_ref[...]).astype(BF16)
    d_ff = w1_ref.shape[1]
    y = jnp.zeros(x1.shape, F32)
    for lo in range(0, d_ff, ff_chunk):
        a = jnp.maximum(_dot(h, w1_ref[:, lo:lo + ff_chunk]), 0.0)
        y = y + _dot((a * a).astype(BF16), w2_ref[lo:lo + ff_chunk, :])
    o_ref[...] = x1 + _rms(y, g_post_mlp_ref[...])


def _out_mlp(x2, oa, ob, w_out, g_post_mix, g_pre_mlp, w1, w2, g_post_mlp, tm, ff_chunk):
    m, d = x2.shape
    d_ff = w1.shape[1]
    const = lambda i: (0, 0)
    row = lambda i: (i, 0)
    once = dict(pipeline_mode=pl.Buffered(1))
    vec = lambda g: g.reshape(1, d).astype(F32)
    return pl.pallas_call(
        functools.partial(_out_mlp_kernel, ff_chunk=ff_chunk),
        grid=(m // tm,),
        in_specs=[
            pl.BlockSpec((tm, d), row),
            pl.BlockSpec((tm, A_WIDTH), row),
            pl.BlockSpec((tm, B_WIDTH), row),
            pl.BlockSpec((A_WIDTH + B_WIDTH, d), const, **once),
            pl.BlockSpec((1, d), const),
            pl.BlockSpec((1, d), const),
            pl.BlockSpec((d, d_ff), const, **once),
            pl.BlockSpec((d_ff, d), const, **once),
            pl.BlockSpec((1, d), const),
        ],
        out_specs=pl.BlockSpec((tm, d), row),
        out_shape=jax.ShapeDtypeStruct((m, d), F32),
        compiler_params=pltpu.CompilerParams(
            dimension_semantics=("arbitrary",), vmem_limit_bytes=VMEM_LIMIT),
        name="out_mlp",
    )(x2, oa, ob, w_out.astype(BF16), vec(g_post_mix), vec(g_pre_mlp),
      w1.astype(BF16), w2.astype(BF16), vec(g_post_mlp))


def kernel(x, w_in, c_norm, w_uk, w_uv, rel_bias, conv_w, a_log, dt_bias, o_norm, w_out,
           pre_norm_mix, post_norm_mix, pre_norm_mlp, post_norm_mlp, w_mlp_in, w_mlp_out):
    batch, seq, d = x.shape
    m = batch * seq
    tm = min(ROW_TILE, m)
    tm_in = min(IN_ROW_TILE, m)
    tq = min(QUERY_TILE, seq)
    assert m % tm == 0 and m % tm_in == 0 and seq % tq == 0 and seq % (PAIR * PAIRS_PER_STEP) == 0
    xs = x.reshape(m, d)
    for l in range(w_in.shape[0]):
        qat, qit, qkv, z, c, ct3, kidx, wit, ab, abt = _in_proj(
            xs, pre_norm_mix[l], w_in[l], c_norm[l], tm_in, tq)
        oa = _dsa(qat, qit, wit, kidx, c, ct3, w_uk[l], w_uv[l], rel_bias, batch, seq, tq, N_BISECT)
        ob = _gdn(qkv, z, ab, abt, conv_w[l], a_log[l], dt_bias[l], o_norm[l], batch, seq)
        xs = _out_mlp(xs, oa, ob, w_out[l], post_norm_mix[l], pre_norm_mlp[l],
                      w_mlp_in[l], w_mlp_out[l], post_norm_mlp[l], tm, FF_CHUNK)
    return xs.reshape(batch, seq, d)
```

```python
import functools
import math

import numpy as np
import jax
import jax.numpy as jnp
from jax import lax
from jax.experimental import pallas as pl
from jax.experimental.pallas import tpu as pltpu

F32 = jnp.float32
BF16 = jnp.bfloat16

EPS = 1e-6
CHUNK = 64
A_HEADS = 8
A_HEAD_DIM = 64
A_WIDTH = A_HEADS * A_HEAD_DIM
A_LATENT = 128
IDX_HEADS = 8
IDX_DIM = 64
TOPK_MAX = 256
REL_BUCKETS = 32
REL_MAX_DIST = 128
B_HEADS = 4
B_HEAD_DIM = 128
B_WIDTH = B_HEADS * B_HEAD_DIM
CONV_WIDTH = 4
IN_SPLITS = (A_WIDTH, A_LATENT, IDX_HEADS * IDX_DIM, IDX_DIM, IDX_HEADS,
             B_WIDTH, B_WIDTH, B_WIDTH, B_HEADS, B_HEADS, B_WIDTH)

LANES = 128
SUBLANES = 8
BF16_ROWS = 16
VMEM_LIMIT = 56 * 1024 * 1024
ROW_TILE = 1024
IN_ROW_TILE = 1024
QUERY_TILE = 256
FF_CHUNK = 1024
N_COARSE = 10
N_BISECT = 7
NEG_BIG = -1e30
LOG2E = 1.0 / math.log(2.0)
DENOM_ROWS = 16
FINISH_UNROLL = 2

NT_DIMS = (((1,), (1,)), ((), ()))


def _dot(a, b, precision=None):
    return jnp.dot(a, b, preferred_element_type=F32, precision=precision)


def _dot_nt(a, b, precision=None):
    return lax.dot_general(a, b, NT_DIMS, preferred_element_type=F32, precision=precision)


def _rms(x, g):
    return x * lax.rsqrt(jnp.mean(x * x, axis=-1, keepdims=True) + EPS) * g


def _sigmoid(x):
    return 1.0 / (1.0 + jnp.exp(-x))


def _silu(x):
    half = 0.5 * x
    return half + half * jnp.tanh(half)


def _softplus(x):
    return jnp.maximum(x, 0.0) + jnp.log1p(jnp.exp(-jnp.abs(x)))


_C_QKV = 0
_C_Z = _C_QKV + 3 * B_WIDTH
_C_CKV = _C_Z + B_WIDTH
_C_SMALL = _C_CKV + A_LATENT
_C_END = _C_SMALL + LANES
_R_QA = 0
_R_QI = _R_QA + A_WIDTH
_R_CKV = _R_QI + IDX_HEADS * IDX_DIM
_R_SMALL = _R_CKV + A_LATENT
_R_END = _R_SMALL + IDX_HEADS + 2 * B_HEADS


def _in_proj_kernel(x_ref, g_ref, w_ref, wt_ref, cn_ref, cnt_ref,
                    qat_ref, qit_ref, qkv_ref, z_ref, c_ref, ct_ref, kidx_ref, wit_ref, ab_ref, abt_ref,
                    *, ts):
    h = _rms(x_ref[...], g_ref[...]).astype(BF16)

    def proj(lo, hi):
        return _dot(h, w_ref[:, lo:hi])

    def proj_t(lo, hi):
        return _dot_nt(wt_ref[lo:hi, :], h)

    qkv_ref[...] = proj(_C_QKV, _C_Z).astype(BF16)
    z_ref[...] = proj(_C_Z, _C_CKV).astype(BF16)
    c_ref[...] = _rms(proj(_C_CKV, _C_SMALL), cn_ref[...]).astype(BF16)
    small = proj(_C_SMALL, _C_END)
    kidx_ref[...] = small[:, :IDX_DIM].astype(BF16)
    ab_ref[...] = small[:, IDX_DIM:IDX_DIM + 2 * B_HEADS]

    qat_ref[...] = proj_t(_R_QA, _R_QI).astype(BF16)
    qit_ref[...] = (proj_t(_R_QI, _R_CKV) * (IDX_DIM ** -0.5)).astype(BF16)
    ckv_t = proj_t(_R_CKV, _R_SMALL)
    c_t = (ckv_t * lax.rsqrt(jnp.mean(ckv_t * ckv_t, axis=0, keepdims=True) + EPS)
           * cnt_ref[...]).astype(BF16)
    for r in range(ct_ref.shape[0]):
        ct_ref[r] = c_t[:, r * ts:(r + 1) * ts]
    small_t = proj_t(_R_SMALL, _R_END)
    wit_ref[...] = small_t[:IDX_HEADS] * (IDX_HEADS ** -0.5)
    abt_ref[...] = small_t[IDX_HEADS:]


def _in_proj(x2, g, w_in, c_norm, tm, ts):
    m, d = x2.shape
    offs = np.cumsum((0,) + IN_SPLITS)
    seg = [w_in[:, offs[i]:offs[i + 1]] for i in range(len(IN_SPLITS))]
    q_a, c_kv, q_idx, k_idx, w_idx, q_b, k_b, v_b, a_b, b_b, z_b = seg
    pad = jnp.zeros((d, LANES - IDX_DIM - 2 * B_HEADS), w_in.dtype)
    w_all = jnp.concatenate([q_b, k_b, v_b, z_b, c_kv, k_idx, a_b, b_b, pad], axis=1).astype(BF16)
    w_t = jnp.concatenate([q_a, q_idx, c_kv, w_idx, a_b, b_b], axis=1).T.astype(BF16)
    const = lambda i: (0, 0)
    row = lambda i: (i, 0)
    col = lambda i: (0, i)
    out_shape = (
        jax.ShapeDtypeStruct((A_WIDTH, m), BF16),
        jax.ShapeDtypeStruct((IDX_HEADS * IDX_DIM, m), BF16),
        jax.ShapeDtypeStruct((m, 3 * B_WIDTH), BF16),
        jax.ShapeDtypeStruct((m, B_WIDTH), BF16),
        jax.ShapeDtypeStruct((m, A_LATENT), BF16),
        jax.ShapeDtypeStruct((m // ts, A_LATENT, ts), BF16),
        jax.ShapeDtypeStruct((m, IDX_DIM), BF16),
        jax.ShapeDtypeStruct((IDX_HEADS, m), F32),
        jax.ShapeDtypeStruct((m, 2 * B_HEADS), F32),
        jax.ShapeDtypeStruct((2 * B_HEADS, m), F32),
    )
    out_specs = (
        pl.BlockSpec((A_WIDTH, tm), col),
        pl.BlockSpec((IDX_HEADS * IDX_DIM, tm), col),
        pl.BlockSpec((tm, 3 * B_WIDTH), row),
        pl.BlockSpec((tm, B_WIDTH), row),
        pl.BlockSpec((tm, A_LATENT), row),
        pl.BlockSpec((tm // ts, A_LATENT, ts), lambda i: (i, 0, 0)),
        pl.BlockSpec((tm, IDX_DIM), row),
        pl.BlockSpec((IDX_HEADS, tm), col),
        pl.BlockSpec((tm, 2 * B_HEADS), row),
        pl.BlockSpec((2 * B_HEADS, tm), col),
    )
    return pl.pallas_call(
        functools.partial(_in_proj_kernel, ts=ts),
        grid=(m // tm,),
        in_specs=[
            pl.BlockSpec((tm, d), row),
            pl.BlockSpec((1, d), const),
            pl.BlockSpec((d, _C_END), const),
            pl.BlockSpec((_R_END, d), const),
            pl.BlockSpec((1, A_LATENT), const),
            pl.BlockSpec((A_LATENT, 1), const),
        ],
        out_specs=out_specs,
        out_shape=out_shape,
        compiler_params=pltpu.CompilerParams(
            dimension_semantics=("arbitrary",), vmem_limit_bytes=VMEM_LIMIT),
        name="in_proj",
    )(x2, g.reshape(1, d), w_all, w_t, c_norm.reshape(1, A_LATENT), c_norm.reshape(A_LATENT, 1))


def _fold_rows(x, op, stop=SUBLANES):
    r = x.shape[0]
    while r > stop:
        r //= 2
        x = op(x[:r], x[r:])
    return x


def _dsa_kernel(qat_ref, qit_ref, wit_ref, kidx_ref, c_ref, ct_ref, wuk_ref, wuvt_ref, bias_ref, o_ref,
                sc_ref, qabs_ref, m_ref, acc_ref, tie_base_ref, range_ref, sc16_ref, *, tq, ts, topk,
                ncoarse, nbisect, tile):
    i = tile
    nkb = i + 1
    q_pos = i * tq + lax.broadcasted_iota(jnp.int32, (1, tq), 1)
    limit = (q_pos // CHUNK + 1) * CHUNK
    kf = float(topk)
    ninf = -jnp.inf

    def over_blocks(body, carry):
        for j in range(nkb):
            carry = body(j, carry)
        return carry

    def key_ids(j):
        return j * ts + lax.broadcasted_iota(jnp.int32, (ts, 1), 0)

    def col_reduce(x8, op):
        return op(x8, axis=0, keepdims=True)

    wit = wit_ref[...]

    range_ref[0] = jnp.full((SUBLANES, tq), jnp.inf, F32)
    range_ref[1] = jnp.full((SUBLANES, tq), ninf, F32)

    def score_blocks(js, diagonal):
        k = jnp.concatenate([kidx_ref[j] for j in js], axis=0)
        r = [_dot(k, qit_ref[h * IDX_DIM:(h + 1) * IDX_DIM, :]) for h in range(IDX_HEADS)]
        s = jnp.maximum(r[0], 0.0) * wit[0:1, :]
        for h in range(1, IDX_HEADS):
            s = s + jnp.maximum(r[h], 0.0) * wit[h:h + 1, :]
        lo, hi = s, s
        if diagonal:
            adm = (js[0] * ts + lax.broadcasted_iota(jnp.int32, (len(js) * ts, 1), 0)) < limit
            s = jnp.where(adm, s, ninf)
            lo, hi = jnp.where(adm, s, jnp.inf), s
        for n, j in enumerate(js):
            sc_ref[j] = s[n * ts:(n + 1) * ts]
            sc16_ref[j] = s[n * ts:(n + 1) * ts].astype(BF16)
        range_ref[0] = jnp.minimum(range_ref[0], _fold_rows(lo, jnp.minimum))
        range_ref[1] = jnp.maximum(range_ref[1], _fold_rows(hi, jnp.maximum))

    for jj in range(nkb // 2):
        score_blocks([2 * jj, 2 * jj + 1], 2 * jj + 1 == i)
    if nkb % 2 == 1:
        score_blocks([i], True)
    rmin = col_reduce(range_ref[0], jnp.min)
    rmax = col_reduce(range_ref[1], jnp.max)

    def count(pred):
        def body(j, acc):
            return acc + _fold_rows(jnp.where(pred(sc_ref[j], j), 1.0, 0.0), jnp.add)
        acc = over_blocks(body, jnp.zeros((SUBLANES, tq), F32))
        return col_reduce(acc, jnp.sum)

    def count_ge(v):
        return count(lambda s, j: s >= v)

    need = limit > topk

    def bisect_body(_, carry):
        lo, hi, n_hi = carry
        mid = lo + 0.5 * (hi - lo)
        c = count_ge(mid)
        ge = c >= kf
        return jnp.where(ge, mid, lo), jnp.where(ge, hi, mid), jnp.where(ge, n_hi, c)

    def count16_ge(v16):
        def body(j, acc):
            hit = jnp.where(sc16_ref[j] >= v16, jnp.ones((), BF16), jnp.zeros((), BF16))
            return acc + _fold_rows(hit, jnp.add, stop=BF16_ROWS).astype(F32)
        acc = over_blocks(body, jnp.zeros((BF16_ROWS, tq), F32))
        return col_reduce(acc, jnp.sum)

    def coarse_body(_, carry):
        a, b = carry
        mid = (a + 0.5 * (b - a)).astype(BF16)
        ge = count16_ge(mid) >= kf
        mid = mid.astype(F32)
        return jnp.where(ge, mid, a), jnp.where(ge, b, mid)


    def max_below(v):
        def body(j, acc):
            s = sc_ref[j]
            return jnp.maximum(acc, _fold_rows(jnp.where(s < v, s, ninf), jnp.maximum))
        return col_reduce(over_blocks(body, jnp.full((SUBLANES, tq), ninf, F32)), jnp.max)

    def finish_cond(state):
        return jnp.min(state[-1]) < 0.5

    def finish_body(state):
        m, n_hi, thr, n_thr, done = state

        def body(j, carry):
            cnt, nxt = carry
            s = sc_ref[j]
            below = s < m
            return (cnt + _fold_rows(jnp.where(below, 0.0, 1.0), jnp.add),
                    jnp.maximum(nxt, _fold_rows(jnp.where(below, s, ninf), jnp.maximum)))

        cnt, nxt = over_blocks(body, (jnp.zeros((SUBLANES, tq), F32),
                                               jnp.full((SUBLANES, tq), ninf, F32)))
        c = col_reduce(cnt, jnp.sum)
        open_ = done < 0.5
        hit = jnp.logical_and(open_, c >= kf)
        miss = jnp.logical_and(open_, c < kf)
        thr = jnp.where(hit, m, thr)
        n_thr = jnp.where(hit, c, n_thr)
        n_hi = jnp.where(miss, c, n_hi)
        m = jnp.where(miss, col_reduce(nxt, jnp.max), m)
        return m, n_hi, thr, n_thr, jnp.where(hit, 1.0, done)

    lowest = float(jnp.finfo(F32).min)
    zero = jnp.zeros((1, tq), F32)
    keep_all = jnp.full((1, tq), lowest, F32)

    def search(_):
        a0 = rmin.astype(BF16).astype(F32)
        top = rmax.astype(BF16).astype(F32)
        b0 = (top + jnp.abs(top) * 2.0 ** -6 + 1e-30).astype(BF16).astype(F32)
        a, b = lax.fori_loop(0, ncoarse, coarse_body, (a0, b0))
        lo0 = a - jnp.abs(a) * 2.0 ** -7 - 1e-30
        _, hi, n_hi = lax.fori_loop(0, nbisect, bisect_body, (lo0, b, count_ge(b)))
        state = (max_below(hi), n_hi, keep_all, zero, jnp.where(need, 0.0, 1.0))
        for _ in range(FINISH_UNROLL):
            state = finish_body(state)
        _, n_hi, thr, n_thr, _ = lax.while_loop(finish_cond, finish_body, state)
        return n_hi, thr, n_thr

    n_hi, thr, n_thr = search(0) if (i + 1) * tq > topk else (zero, keep_all, zero)

    n_tie = jnp.where(need, kf - n_hi, 0.0)
    surplus = jnp.max(jnp.where(jnp.logical_and(need, n_thr - n_hi > n_tie), 1.0, 0.0)) > 0.5
    tie_base_ref[...] = zero
    kr = lax.broadcasted_iota(jnp.int32, (ts, ts), 0)
    kc = lax.broadcasted_iota(jnp.int32, (ts, ts), 1)

    def mask_bias(j):
        def plain(_):
            return jnp.where(sc_ref[j] >= thr, 0.0, NEG_BIG)

        def with_ties(_):
            s = sc_ref[j]
            tie = s == thr
            prefix_ones = jnp.where(kc <= kr, 1.0, 0.0).astype(BF16)
            rank = _dot(prefix_ones, jnp.where(tie, 1.0, 0.0).astype(BF16)) + tie_base_ref[...]
            tie_base_ref[...] = rank[ts - 1:ts, :]
            keep = jnp.where(tie, jnp.where(rank <= n_tie, 0.0, NEG_BIG), NEG_BIG)
            return jnp.where(s > thr, 0.0, keep)

        return lax.cond(surplus, with_ties, plain, 0)

    for h in range(A_HEADS):
        qh = _dot(wuk_ref[h], qat_ref[h * A_HEAD_DIM:(h + 1) * A_HEAD_DIM, :])
        qabs_ref[h] = (qh * (A_HEAD_DIM ** -0.5 * LOG2E)).astype(BF16)
    m_ref[...] = jnp.full(m_ref.shape, NEG_BIG, F32)
    acc_ref[...] = jnp.zeros(acc_ref.shape, F32)
    heads = range(A_HEADS)

    def attend(js, bias_of_head):
        n = len(js) * ts
        cj = jnp.concatenate([c_ref[j] for j in js], axis=0)
        ctj = jnp.concatenate([jnp.concatenate([ct_ref[j] for j in js], axis=1),
                               jnp.ones((DENOM_ROWS, n), BF16)], axis=0)
        mb = jnp.concatenate([mask_bias(j) for j in js], axis=0)
        s = [_dot(cj, qabs_ref[h]) for h in heads]
        for h in heads:
            b = bias_of_head(h)
            s[h] = s[h] + (mb if b is None else mb + b)
        m_old = [m_ref[h] for h in heads]
        m_new = [jnp.maximum(m_old[h], col_reduce(_fold_rows(s[h], jnp.maximum), jnp.max))
                 for h in heads]
        alpha = [jnp.exp2(m_old[h] - m_new[h]) for h in heads]
        p = [jnp.exp2(s[h] - m_new[h]).astype(BF16) for h in heads]
        pv = [_dot(ctj, p[h]) for h in heads]
        for h in heads:
            acc_ref[h] = alpha[h] * acc_ref[h] + pv[h]
            m_ref[h] = m_new[h]

    n_far = max(i - 1, 0)
    for jj in range(n_far // 2):
        attend([2 * jj, 2 * jj + 1], lambda h: None)
    if n_far % 2 == 1:
        attend([n_far - 1], lambda h: None)
    if i > 0:
        attend([i - 1, i], lambda h: bias_ref[h])
    else:
        attend([i], lambda h: bias_ref[h, ts:, :])

    outs = [_dot(wuvt_ref[h],
                 (acc_ref[h, :A_LATENT, :] / acc_ref[h, A_LATENT:A_LATENT + 1, :]).astype(BF16))
            for h in range(A_HEADS)]
    o_ref[...] = jnp.concatenate(outs, axis=0).T.astype(o_ref.dtype)


def _t5_bucket(rel):
    nb = REL_BUCKETS // 2
    max_exact = nb // 2
    side = jnp.where(rel > 0, nb, 0)
    n = jnp.abs(rel)
    nf = jnp.maximum(n, 1).astype(jnp.float32)
    large = max_exact + (jnp.log(nf / max_exact) / math.log(REL_MAX_DIST / max_exact)
                         * (nb - max_exact)).astype(jnp.int32)
    large = jnp.minimum(large, nb - 1)
    return side + jnp.where(n < max_exact, n, large)


def _near_bias_kernel(far_ref, rb_ref, bucket_ref, o_ref):
    bucket = bucket_ref[...]
    for h in range(A_HEADS):
        acc = jnp.zeros(bucket.shape, F32)
        for b in range(REL_BUCKETS):
            acc = jnp.where(bucket == b, rb_ref[b, h], acc)
        o_ref[h] = (acc - rb_ref[far_ref[0], h]) * LOG2E


def _near_bias(rel_bias, tq, ts):
    assert ts >= REL_MAX_DIST
    rel = (jnp.arange(2 * ts)[:, None] - ts) - jnp.arange(tq)[None, :]
    far = _t5_bucket(jnp.full((1,), -(ts + 1), jnp.int32))
    smem = pl.BlockSpec(memory_space=pltpu.SMEM)
    return pl.pallas_call(
        _near_bias_kernel,
        in_specs=[smem, smem, pl.BlockSpec(memory_space=pltpu.VMEM)],
        out_specs=pl.BlockSpec(memory_space=pltpu.VMEM),
        out_shape=jax.ShapeDtypeStruct((A_HEADS, 2 * ts, tq), F32),
        name="near_bias",
    )(far, rel_bias.astype(F32), _t5_bucket(rel).astype(jnp.int32))


def _dsa(qat, qit, wit, kidx, c, ct3, w_uk, w_uv, rel_bias, batch, seq, tq, nbisect):
    ts = tq
    m = batch * seq
    nq = seq // tq
    nks = seq // ts
    topk = min(TOPK_MAX, seq // 4)
    kidx3 = kidx.reshape(m // ts, ts, IDX_DIM)
    c3 = c.reshape(m // ts, ts, A_LATENT)
    wuk = w_uk.astype(BF16)
    wuv_t = jnp.swapaxes(w_uv, 1, 2).astype(BF16)
    bias = _near_bias(rel_bias, tq, ts)
    kv = lambda b: (b, 0, 0)
    const3 = lambda b: (0, 0, 0)
    outs = [_dsa_tile(i, qat, qit, wit, kidx3, c3, ct3, wuk, wuv_t, bias, kv, const3,
                      batch, nq, nks, tq, ts, topk, nbisect) for i in range(nq)]
    return jnp.stack(outs, axis=1).reshape(m, A_WIDTH)


def _dsa_tile(i, qat, qit, wit, kidx3, c3, ct3, wuk, wuv_t, bias, kv, const3,
              batch, nq, nks, tq, ts, topk, nbisect):
    qcol = lambda b: (0, b * nq + i)
    kern = functools.partial(_dsa_kernel, tq=tq, ts=ts, topk=topk, ncoarse=N_COARSE, nbisect=nbisect,
                             tile=i)
    return pl.pallas_call(
        kern,
        grid=(batch,),
        in_specs=[
            pl.BlockSpec((A_WIDTH, tq), qcol),
            pl.BlockSpec((IDX_HEADS * IDX_DIM, tq), qcol),
            pl.BlockSpec((IDX_HEADS, tq), qcol),
            pl.BlockSpec((nks, ts, IDX_DIM), kv),
            pl.BlockSpec((nks, ts, A_LATENT), kv),
            pl.BlockSpec((nks, A_LATENT, ts), kv),
            pl.BlockSpec((A_HEADS, A_LATENT, A_HEAD_DIM), const3),
            pl.BlockSpec((A_HEADS, A_HEAD_DIM, A_LATENT), const3),
            pl.BlockSpec((A_HEADS, 2 * ts, tq), const3),
        ],
        out_specs=pl.BlockSpec((None, tq, A_WIDTH), lambda b: (b, 0, 0)),
        out_shape=jax.ShapeDtypeStruct((batch, tq, A_WIDTH), BF16),
        scratch_shapes=[
            pltpu.VMEM((nks, ts, tq), F32),
            pltpu.VMEM((A_HEADS, A_LATENT, tq), BF16),
            pltpu.VMEM((A_HEADS, 1, tq), F32),
            pltpu.VMEM((A_HEADS, A_LATENT + DENOM_ROWS, tq), F32),
            pltpu.VMEM((1, tq), F32),
            pltpu.VMEM((2, SUBLANES, tq), F32),
            pltpu.VMEM((nks, ts, tq), BF16),
        ],
        compiler_params=pltpu.CompilerParams(
            dimension_semantics=("arbitrary",), vmem_limit_bytes=VMEM_LIMIT),
        name=f"dsa_t{i}",
    )(qat, qit, wit, kidx3, c3, ct3, wuk, wuv_t, bias)


PAIR = 2 * CHUNK
PAIRS_PER_STEP = 2
HI = lax.Precision.HIGHEST


def _gdn_kernel(qkv_ref, z_ref, ab_ref, abt_ref, convw_ref, pcol_ref, prow_ref, onorm_ref, o_ref,
                u_s, w_s, qg_s, kdt_s, a_s, gl_s, st_s, stage_ref, *, seq):
    npair = seq // PAIR
    ri = lax.broadcasted_iota(jnp.int32, (PAIR, PAIR), 0)
    ci = lax.broadcasted_iota(jnp.int32, (PAIR, PAIR), 1)
    same = (ri // CHUNK) == (ci // CHUNK)
    lower_bd = jnp.where(jnp.logical_and(same, ci <= ri), 1.0, 0.0)
    upper_bd = jnp.where(jnp.logical_and(same, ri <= ci), 1.0, 0.0)
    causal_bd = jnp.logical_and(same, ci <= ri)
    diag = ci == ri
    eye = jnp.where(diag, 1.0, 0.0)
    neg_exp_a_col = -jnp.exp(pcol_ref[:, 0:1])
    dt_col = pcol_ref[:, 1:2]
    neg_exp_a_row = -jnp.exp(prow_ref[0:1, :])
    dt_row = prow_ref[1:2, :]
    convw = convw_ref[...]
    first_chunk = lax.broadcasted_iota(jnp.int32, (PAIR, 1), 0) < CHUNK

    def aligned(x, m):
        return x if isinstance(x, int) else pl.multiple_of(x, m)

    def conv_silu(i, p, col):
        slot = i * 3 * B_HEADS + col // B_HEAD_DIM
        r0 = aligned(p * PAIR, PAIR)
        cur = qkv_ref[pl.ds(r0, PAIR), col:col + B_HEAD_DIM].astype(F32)
        lead = BF16_ROWS
        prev0 = (max(r0 - lead, 0) if isinstance(r0, int)
                 else pl.multiple_of(jnp.maximum(r0 - lead, 0), lead))
        prev = qkv_ref[pl.ds(prev0, lead), col:col + B_HEAD_DIM].astype(F32)
        prev = jnp.where(p > 0, prev, 0.0)
        stage = stage_ref.at[slot]
        stage[0:lead, :] = prev
        stage[lead:lead + PAIR, :] = cur
        y = cur * convw[CONV_WIDTH - 1:CONV_WIDTH, col:col + B_HEAD_DIM]
        for back in range(1, CONV_WIDTH):
            sh = stage[lead - back:lead - back + PAIR, :]
            y = y + sh * convw[CONV_WIDTH - 1 - back:CONV_WIDTH - back, col:col + B_HEAD_DIM]
        return _silu(y)

    def l2n(x):
        return x * lax.rsqrt(jnp.sum(x * x, axis=-1, keepdims=True) + EPS)

    heads = range(B_HEADS)
    pairs = range(PAIRS_PER_STEP)
    units = [(i, h) for i in pairs for h in heads]

    def slot_of(step, i):
        return (step % 2) * PAIRS_PER_STEP + i

    def phase1(step):
        ps = [step * PAIRS_PER_STEP + i for i in pairs]
        gc_col, gc_row, beta_col = [], [], []
        for p in ps:
            ab = ab_ref[pl.ds(aligned(p * PAIR, PAIR), PAIR), :]
            g_col = neg_exp_a_row * _softplus(ab + dt_row)
            gc_col.append(_dot(lower_bd, g_col, HI))
            beta_col.append(_sigmoid(ab))
            g_row = neg_exp_a_col * _softplus(abt_ref[p] + dt_col)
            gc_row.append(_dot(g_row, upper_bd, HI))
        q = {(i, h): l2n(conv_silu(i, ps[i], h * B_HEAD_DIM)) * (B_HEAD_DIM ** -0.5) for i, h in units}
        k = {(i, h): l2n(conv_silu(i, ps[i], B_WIDTH + h * B_HEAD_DIM)) for i, h in units}
        v = {(i, h): conv_silu(i, ps[i], 2 * B_WIDTH + h * B_HEAD_DIM) for i, h in units}
        gcol = {(i, h): gc_col[i][:, h:h + 1] for i, h in units}
        beta = {(i, h): beta_col[i][:, B_HEADS + h:B_HEADS + h + 1] for i, h in units}
        kb = {u: k[u] * beta[u] for u in units}
        qk_kk = {u: _dot_nt(jnp.concatenate([q[u], kb[u]], axis=0).astype(BF16), k[u].astype(BF16))
                 for u in units}
        decay = {(i, h): jnp.exp(jnp.where(causal_bd, gcol[i, h] - gc_row[i][h:h + 1, :], -jnp.inf))
                 for i, h in units}
        intra = {u: (qk_kk[u][:PAIR] * decay[u]).astype(BF16) for u in units}
        low = {u: jnp.where(diag, 0.0, qk_kk[u][PAIR:] * decay[u]) for u in units}

        x = {u: eye - low[u] for u in units}
        low16 = {u: low[u].astype(BF16) for u in units}
        pw = {u: _dot(low16[u], low16[u]) for u in units}
        for _ in range(int(math.log2(CHUNK)) - 2):
            pw16 = {u: pw[u].astype(BF16) for u in units}
            r = {u: _dot(jnp.concatenate([x[u].astype(BF16), pw16[u]], axis=0), pw16[u]) for u in units}
            x = {u: x[u] + r[u][:PAIR] for u in units}
            pw = {u: r[u][PAIR:] for u in units}
        tinv = {u: x[u] + _dot(x[u].astype(BF16), pw[u].astype(BF16)) for u in units}

        eg = {u: jnp.exp(gcol[u]) for u in units}
        uw = {u: _dot(tinv[u].astype(BF16),
                      jnp.concatenate([v[u] * beta[u], kb[u] * eg[u]], axis=1).astype(BF16))
              for u in units}
        for i, h in units:
            u, slot = (i, h), slot_of(step, i)
            g_last = jnp.where(first_chunk, gcol[u][CHUNK - 1:CHUNK], gcol[u][PAIR - 1:PAIR])
            kd = k[u] * jnp.exp(g_last - gcol[u])
            u_s[slot, h] = uw[u][:, :B_HEAD_DIM]
            w_s[slot, h] = uw[u][:, B_HEAD_DIM:].astype(BF16)
            qg_s[slot, h] = (q[u] * eg[u]).astype(BF16)
            for cc in range(2):
                rs = slice(cc * CHUNK, (cc + 1) * CHUNK)
                kdt_s[slot, h, cc] = kd[rs].T.astype(BF16)
                a_s[slot, h, cc] = intra[u][rs, rs]
                gl_s[slot, h, cc] = jnp.broadcast_to(
                    jnp.exp(gcol[u][(cc + 1) * CHUNK - 1:(cc + 1) * CHUNK]), (SUBLANES, LANES))

    st_s[...] = jnp.zeros(st_s.shape, F32)
    onorm = onorm_ref[...]

    def phase2(step):
        for i in pairs:
            p, slot = step * PAIRS_PER_STEP + i, slot_of(step, i)
            for cc in range(2):
                rs = slice(cc * CHUNK, (cc + 1) * CHUNK)
                rows = pl.ds(aligned(p * PAIR + cc * CHUNK, CHUNK), CHUNK)
                state = [st_s[h] for h in heads]
                sb = [state[h].astype(BF16) for h in heads]
                wq = [_dot(jnp.concatenate([w_s[slot, h, rs, :], qg_s[slot, h, rs, :]], axis=0), sb[h])
                      for h in heads]
                vb = [(u_s[slot, h, rs, :] - wq[h][:CHUNK]).astype(BF16) for h in heads]
                upd = [_dot(kdt_s[slot, h, cc], vb[h]) for h in heads]
                av = [_dot(a_s[slot, h, cc], vb[h]) for h in heads]
                for h in heads:
                    st_s[h] = state[h] * gl_s[slot, h, cc][0:1, :] + upd[h]
                for h in heads:
                    o = wq[h][CHUNK:] + av[h]
                    z = z_ref[rows, h * B_HEAD_DIM:(h + 1) * B_HEAD_DIM].astype(F32)
                    o_ref[rows, h * B_HEAD_DIM:(h + 1) * B_HEAD_DIM] = (
                        _rms(o, onorm) * _silu(z)).astype(o_ref.dtype)

    def body(step, carry):
        phase2(step - 1)
        phase1(step)
        return carry

    nstep = npair // PAIRS_PER_STEP
    phase1(0)
    lax.fori_loop(1, nstep, body, 0)
    phase2(nstep - 1)


def _gdn(qkv, z, ab, abt, conv_w, a_log, dt_bias, o_norm, batch, seq):
    m = batch * seq
    npair = seq // PAIR
    assert npair % PAIRS_PER_STEP == 0
    nslot = 2 * PAIRS_PER_STEP
    abt3 = jnp.swapaxes(abt.reshape(2 * B_HEADS, m // PAIR, PAIR), 0, 1)
    zeros = jnp.zeros((B_HEADS,), F32)
    alog = jnp.concatenate([a_log.astype(F32), zeros])
    dtb = jnp.concatenate([dt_bias.astype(F32), zeros])
    pcol = jnp.stack([alog, dtb], axis=1)
    prow = jnp.stack([alog, dtb], axis=0)
    const = lambda b: (0, 0)
    rows = lambda b: (b, 0)
    return pl.pallas_call(
        functools.partial(_gdn_kernel, seq=seq),
        grid=(batch,),
        in_specs=[
            pl.BlockSpec((seq, 3 * B_WIDTH), rows),
            pl.BlockSpec((seq, B_WIDTH), rows),
            pl.BlockSpec((seq, 2 * B_HEADS), rows),
            pl.BlockSpec((npair, 2 * B_HEADS, PAIR), lambda b: (b, 0, 0)),
            pl.BlockSpec((CONV_WIDTH, 3 * B_WIDTH), const),
            pl.BlockSpec((2 * B_HEADS, 2), const),
            pl.BlockSpec((2, 2 * B_HEADS), const),
            pl.BlockSpec((1, B_HEAD_DIM), const),
        ],
        out_specs=pl.BlockSpec((seq, B_WIDTH), rows),
        out_shape=jax.ShapeDtypeStruct((m, B_WIDTH), BF16),
        scratch_shapes=[
            pltpu.VMEM((nslot, B_HEADS, PAIR, B_HEAD_DIM), F32),
            pltpu.VMEM((nslot, B_HEADS, PAIR, B_HEAD_DIM), BF16),
            pltpu.VMEM((nslot, B_HEADS, PAIR, B_HEAD_DIM), BF16),
            pltpu.VMEM((nslot, B_HEADS, 2, B_HEAD_DIM, CHUNK), BF16),
            pltpu.VMEM((nslot, B_HEADS, 2, CHUNK, CHUNK), BF16),
            pltpu.VMEM((nslot, B_HEADS, 2, SUBLANES, LANES), F32),
            pltpu.VMEM((B_HEADS, B_HEAD_DIM, B_HEAD_DIM), F32),
            pltpu.VMEM((PAIRS_PER_STEP * 3 * B_HEADS, BF16_ROWS + PAIR, B_HEAD_DIM), F32),
        ],
        compiler_params=pltpu.CompilerParams(
            dimension_semantics=("arbitrary",), vmem_limit_bytes=VMEM_LIMIT),
        name="gdn",
    )(qkv, z, ab, abt3, conv_w.astype(F32), pcol, prow, o_norm.reshape(1, B_HEAD_DIM).astype(F32))


def _out_mlp_kernel(x_ref, oa_ref, ob_ref, wo_ref, g_post_mix_ref, g_pre_mlp_ref, w1_ref, w2_ref,
                    g_post_mlp_ref, o_ref, *, ff_chunk):
    mix = _dot(oa_ref[...], wo_ref[:A_WIDTH, :]) + _dot(ob_ref[...], wo_ref[A_WIDTH:, :])
    x1 = x_ref[...] + _rms(mix, g_post_mix_ref[...])
    h = _rms(x1, g_pre_mlp_ref[...]).astype(BF16)
    d_ff = w1_ref.shape[1]
    y = jnp.zeros(x1.shape, F32)
    for lo in range(0, d_ff, ff_chunk):
        a = jnp.maximum(_dot(h, w1_ref[:, lo:lo + ff_chunk]), 0.0)
        y = y + _dot((a * a).astype(BF16), w2_ref[lo:lo + ff_chunk, :])
    o_ref[...] = x1 + _rms(y, g_post_mlp_ref[...])


def _out_mlp(x2, oa, ob, w_out, g_post_mix, g_pre_mlp, w1, w2, g_post_mlp, tm, ff_chunk):
    m, d = x2.shape
    d_ff = w1.shape[1]
    const = lambda i: (0, 0)
    row = lambda i: (i, 0)
    once = dict(pipeline_mode=pl.Buffered(1))
    vec = lambda g: g.reshape(1, d).astype(F32)
    return pl.pallas_call(
        functools.partial(_out_mlp_kernel, ff_chunk=ff_chunk),
        grid=(m // tm,),
        in_specs=[
            pl.BlockSpec((tm, d), row),
            pl.BlockSpec((tm, A_WIDTH), row),
            pl.BlockSpec((tm, B_WIDTH), row),
            pl.BlockSpec((A_WIDTH + B_WIDTH, d), const, **once),
            pl.BlockSpec((1, d), const),
            pl.BlockSpec((1, d), const),
            pl.BlockSpec((d, d_ff), const, **once),
            pl.BlockSpec((d_ff, d), const, **once),
            pl.BlockSpec((1, d), const),
        ],
        out_specs=pl.BlockSpec((tm, d), row),
        out_shape=jax.ShapeDtypeStruct((m, d), F32),
        compiler_params=pltpu.CompilerParams(
            dimension_semantics=("arbitrary",), vmem_limit_bytes=VMEM_LIMIT),
        name="out_mlp",
    )(x2, oa, ob, w_out.astype(BF16), vec(g_post_mix), vec(g_pre_mlp),
      w1.astype(BF16), w2.astype(BF16), vec(g_post_mlp))


def kernel(x, w_in, c_norm, w_uk, w_uv, rel_bias, conv_w, a_log, dt_bias, o_norm, w_out,
           pre_norm_mix, post_norm_mix, pre_norm_mlp, post_norm_mlp, w_mlp_in, w_mlp_out):
    batch, seq, d = x.shape
    m = batch * seq
    tm = min(ROW_TILE, m)
    tm_in = min(IN_ROW_TILE, m)
    tq = min(QUERY_TILE, seq)
    assert m % tm == 0 and m % tm_in == 0 and seq % tq == 0 and seq % (PAIR * PAIRS_PER_STEP) == 0
    xs = x.reshape(m, d)
    for l in range(w_in.shape[0]):
        qat, qit, qkv, z, c, ct3, kidx, wit, ab, abt = _in_proj(
            xs, pre_norm_mix[l], w_in[l], c_norm[l], tm_in, tq)
        oa = _dsa(qat, qit, wit, kidx, c, ct3, w_uk[l], w_uv[l], rel_bias, batch, seq, tq, N_BISECT)
        ob = _gdn(qkv, z, ab, abt, conv_w[l], a_log[l], dt_bias[l], o_norm[l], batch, seq)
        xs = _out_mlp(xs, oa, ob, w_out[l], post_norm_mix[l], pre_norm_mlp[l],
                      w_mlp_in[l], w_mlp_out[l], post_norm_mlp[l], tm, FF_CHUNK)
    return xs.reshape(batch, seq, d)
```
